```python
import math
import jax
import jax.numpy as jnp
from jax import lax
import numpy as np

D_MODEL = 1024
BATCH = 2
SEQ = 8192
DEPTH = 1

ATTN_HEADS = 8
ATTN_HEAD_DIM = 64
ATTN_WIDTH = ATTN_HEADS * ATTN_HEAD_DIM
HGRN_HEADS = 8
HGRN_HEAD_DIM = 64
HGRN_WIDTH = HGRN_HEADS * HGRN_HEAD_DIM
MIX_WIDTH = ATTN_WIDTH + HGRN_WIDTH
IN_SIZES = (ATTN_WIDTH, ATTN_WIDTH, ATTN_WIDTH, HGRN_WIDTH, HGRN_WIDTH, HGRN_WIDTH, HGRN_WIDTH)
IN_WIDTH = sum(IN_SIZES)
MOBA_BLOCK = 256
MOBA_TOPK = 3
Q_BLOCK = 64
HGRN_CHUNK = 64
N_BUCKETS = 32
MAX_DISTANCE = 128
N_GROUPS = 4
EXPERTS_PER_GROUP = 4
N_EXPERTS = N_GROUPS * EXPERTS_PER_GROUP
EXPERT_TOPK = 2
D_EXPERT = 512
LN_EPS = 1e-5
RMS_EPS = 1e-6
DEEPNORM_ALPHA = (2.0 * DEPTH) ** 0.25
DEEPNORM_BETA = (8.0 * DEPTH) ** -0.25

kernel_name = 'hybrid_moba_hgrn2_hiermoe_deepnorm'


def layer_norm(x, g, b):
    xf = x.astype(jnp.float32)
    mu = jnp.mean(xf, axis=-1, keepdims=True)
    var = jnp.mean(jnp.square(xf - mu), axis=-1, keepdims=True)
    y = (xf - mu) * lax.rsqrt(var + LN_EPS) * g.astype(jnp.float32) + b.astype(jnp.float32)
    return y.astype(x.dtype)


def t5_bucket(n):
    n = jnp.maximum(n, 0)
    max_exact = N_BUCKETS // 2
    nf = jnp.maximum(n, 1).astype(jnp.float32)
    large = max_exact + (jnp.log(nf / max_exact) / math.log(MAX_DISTANCE / max_exact)
                         * (N_BUCKETS - max_exact)).astype(jnp.int32)
    large = jnp.minimum(large, N_BUCKETS - 1)
    return jnp.where(n < max_exact, n, large)


def moba_attention(q, k, v, bias_table):
    B_, H, S, dh = q.shape
    n_blocks = -(-S // MOBA_BLOCK)
    pad = n_blocks * MOBA_BLOCK - S
    k_pad = jnp.pad(k, ((0, 0), (0, 0), (0, pad), (0, 0)))
    v_pad = jnp.pad(v, ((0, 0), (0, 0), (0, pad), (0, 0)))
    k_blocks = k_pad.reshape(B_, H, n_blocks, MOBA_BLOCK, dh)
    v_blocks = v_pad.reshape(B_, H, n_blocks, MOBA_BLOCK, dh)
    k_mean = jnp.mean(k_blocks.astype(jnp.float32), axis=3)
    n_sel = min(MOBA_TOPK, n_blocks)
    scale = dh ** -0.5
    bias_hb = bias_table.T.astype(jnp.float32)
    bi = jnp.arange(B_)[:, None, None, None]
    hi = jnp.arange(H)[None, :, None, None]
    blk_ids = jnp.arange(n_blocks)
    offs = jnp.arange(MOBA_BLOCK)

    def per_query_block(qb):
        start = qb * Q_BLOCK
        q_c = lax.dynamic_slice_in_dim(q, start, Q_BLOCK, axis=2).astype(jnp.float32)
        q_pos = start + jnp.arange(Q_BLOCK)
        own = start // MOBA_BLOCK
        gate = jnp.einsum('bhqd,bhnd->bhqn', q_c, k_mean)
        gate = jnp.where(blk_ids < own, gate, -jnp.inf)
        _, sel = lax.top_k(gate, n_sel)
        k_sel = k_blocks[bi, hi, sel].astype(jnp.float32)
        v_sel = v_blocks[bi, hi, sel].astype(jnp.float32)
        kpos_sel = sel[..., None] * MOBA_BLOCK + offs
        l_sel = (jnp.einsum('bhqd,bhqjkd->bhqjk', q_c, k_sel) * scale
                 + bias_hb[hi[..., None], t5_bucket(q_pos[:, None, None] - kpos_sel)])
        l_sel = jnp.where((sel < own)[..., None], l_sel, -jnp.inf)
        l_sel = l_sel.reshape(B_, H, Q_BLOCK, n_sel * MOBA_BLOCK)
        k_own = lax.dynamic_slice_in_dim(k_pad, own * MOBA_BLOCK, MOBA_BLOCK, axis=2).astype(jnp.float32)
        v_own = lax.dynamic_slice_in_dim(v_pad, own * MOBA_BLOCK, MOBA_BLOCK, axis=2).astype(jnp.float32)
        rel = q_pos[:, None] - (own * MOBA_BLOCK + offs)[None, :]
        l_own = jnp.einsum('bhqd,bhkd->bhqk', q_c, k_own) * scale + bias_hb[:, t5_bucket(rel)][None]
        l_own = jnp.where(rel >= 0, l_own, -jnp.inf)
        p = jax.nn.softmax(jnp.concatenate([l_sel, l_own], axis=-1), axis=-1)
        p_sel = p[..., :n_sel * MOBA_BLOCK].reshape(B_, H, Q_BLOCK, n_sel, MOBA_BLOCK)
        p_own = p[..., n_sel * MOBA_BLOCK:]
        return (jnp.einsum('bhqjk,bhqjkd->bhqd', p_sel, v_sel)
                + jnp.einsum('bhqk,bhkd->bhqd', p_own, v_own))

    out = lax.map(per_query_block, jnp.arange(S // Q_BLOCK))
    out = out.transpose(1, 2, 0, 3, 4).reshape(B_, H, S, dh)
    return out.astype(q.dtype)


def hgrn2(q, f_logit, i, lb):
    B_, S, H, dk = q.shape
    C = HGRN_CHUNK
    nc = S // C
    f = lb + (1.0 - lb) * jax.nn.sigmoid(f_logit.astype(jnp.float32))
    log_f = jnp.log(f)
    kk = 1.0 - f
    qq = jax.nn.silu(q.astype(jnp.float32))
    vv = i.astype(jnp.float32)

    def chunks(t):
        return t.reshape(B_, nc, C, H, dk).transpose(1, 0, 3, 2, 4)

    causal = jnp.tril(jnp.ones((C, C), dtype=bool))

    def step(state, inp):
        qc, kc, vc, gc = inp
        b = jnp.cumsum(gc, axis=2)
        o_inter = jnp.einsum('bhck,bhkv->bhcv', qc * jnp.exp(b), state)
        diff = b[:, :, :, None, :] - b[:, :, None, :, :]
        decay = jnp.exp(jnp.where(causal[:, :, None], diff, -jnp.inf))
        A = jnp.einsum('bhtk,bhtsk,bhsk->bhts', qc, decay, kc)
        o_intra = jnp.einsum('bhts,bhsv->bhtv', A, vc)
        b_last = b[:, :, -1:, :]
        new_state = (jnp.exp(b_last[:, :, 0, :])[..., None] * state
                     + jnp.einsum('bhsk,bhsv->bhkv', kc * jnp.exp(b_last - b), vc))
        return new_state, o_inter + o_intra

    state0 = jnp.zeros((B_, H, dk, dk), jnp.float32)
    _, o = lax.scan(step, state0, (chunks(qq), chunks(kk), chunks(vv), chunks(log_f)))
    return o.transpose(1, 0, 3, 2, 4).reshape(B_, S, H, dk)


def hier_moe(x, w_group, b_group, w_expert, b_expert, w_gate, w_up, w_down):
    B_, S, D = x.shape
    xt = x.reshape(B_ * S, D)
    g_prob = jax.nn.softmax((xt @ w_group + b_group).astype(jnp.float32), axis=-1)
    g_w, g_idx = lax.top_k(g_prob, 1)
    g_w, g_idx = g_w[:, 0], g_idx[:, 0]
    e_logits_all = jnp.einsum('td,gde->tge', xt, w_expert) + b_expert
    e_logits = jnp.take_along_axis(e_logits_all, g_idx[:, None, None], axis=1)[:, 0]
    e_prob = jax.nn.softmax(e_logits.astype(jnp.float32), axis=-1)
    e_w, e_idx = lax.top_k(e_prob, EXPERT_TOPK)
    e_w = e_w / jnp.sum(e_w, axis=-1, keepdims=True)
    gid = g_idx[:, None] * EXPERTS_PER_GROUP + e_idx
    w_tok = g_w[:, None] * e_w
    combine = jnp.einsum('tk,tke->te', w_tok, jax.nn.one_hot(gid, N_EXPERTS, dtype=jnp.float32)).astype(x.dtype)
    y = jnp.zeros_like(xt)
    for e in range(N_EXPERTS):
        h = jax.nn.silu(xt @ w_gate[e]) * (xt @ w_up[e])
        y = y + combine[:, e:e + 1] * (h @ w_down[e])
    return y.reshape(B_, S, D)


def setup_inputs(seed: int = 0) -> dict:
    key = jax.random.key(seed)
    ks = jax.random.split(key, 20)
    f32 = jnp.float32
    nrm = lambda k, shape: jax.random.normal(k, shape, f32)
    beta = DEEPNORM_BETA
    col_scale = jnp.concatenate([jnp.full((s,), sc, f32) for s, sc in
                                 zip(IN_SIZES, (1.0, 1.0, beta, 1.0, 1.0, beta, 1.0))])
    return {
        'x': nrm(ks[0], (BATCH, SEQ, D_MODEL)),
        'w_in': nrm(ks[1], (DEPTH, D_MODEL, IN_WIDTH)) * D_MODEL ** -0.5 * col_scale,
        't5_bias': nrm(ks[2], (N_BUCKETS, ATTN_HEADS)) * 0.5,
        'hgrn_lb_logits': nrm(ks[3], (DEPTH + 1, HGRN_WIDTH)) * 0.5,
        'hgrn_norm_w': 1.0 + 0.01 * nrm(ks[4], (DEPTH, HGRN_WIDTH)),
        'w_o': nrm(ks[5], (DEPTH, MIX_WIDTH, D_MODEL)) * MIX_WIDTH ** -0.5 * beta,
        'ln1_g': 1.0 + 0.01 * nrm(ks[6], (DEPTH, D_MODEL)),
        'ln1_b': 0.01 * nrm(ks[7], (DEPTH, D_MODEL)),
        'w_group': nrm(ks[8], (DEPTH, D_MODEL, N_GROUPS)) * D_MODEL ** -0.5,
        'b_group': 0.01 * nrm(ks[9], (DEPTH, N_GROUPS)),
        'w_expert': nrm(ks[10], (DEPTH, N_GROUPS, D_MODEL, EXPERTS_PER_GROUP)) * D_MODEL ** -0.5,
        'b_expert': 0.01 * nrm(ks[11], (DEPTH, N_GROUPS, EXPERTS_PER_GROUP)),
        'w_gate': nrm(ks[12], (DEPTH, N_EXPERTS, D_MODEL, D_EXPERT)) * D_MODEL ** -0.5,
        'w_up': nrm(ks[13], (DEPTH, N_EXPERTS, D_MODEL, D_EXPERT)) * D_MODEL ** -0.5,
        'w_down': nrm(ks[14], (DEPTH, N_EXPERTS, D_EXPERT, D_MODEL)) * D_EXPERT ** -0.5 * beta,
        'ln2_g': 1.0 + 0.01 * nrm(ks[15], (DEPTH, D_MODEL)),
        'ln2_b': 0.01 * nrm(ks[16], (DEPTH, D_MODEL)),
    }


def reference(x, w_in, t5_bias, hgrn_lb_logits, hgrn_norm_w, w_o, ln1_g, ln1_b,
              w_group, b_group, w_expert, b_expert, w_gate, w_up, w_down, ln2_g, ln2_b):
    B_, S, D = x.shape
    lb_all = jnp.cumsum(jax.nn.softmax(hgrn_lb_logits.astype(jnp.float32), axis=0), axis=0)
    offsets = np.cumsum((0,) + IN_SIZES)
    for l in range(DEPTH):
        proj = jnp.einsum('bsd,de->bse', x, w_in[l])
        aq, ak, av, hq, hf, hi_, hg = [proj[..., offsets[j]:offsets[j + 1]] for j in range(len(IN_SIZES))]
        to_heads = lambda t: t.reshape(B_, S, ATTN_HEADS, ATTN_HEAD_DIM).transpose(0, 2, 1, 3)
        a = moba_attention(to_heads(aq), to_heads(ak), to_heads(av), t5_bias)
        a = a.transpose(0, 2, 1, 3).reshape(B_, S, ATTN_WIDTH)
        hshape = (B_, S, HGRN_HEADS, HGRN_HEAD_DIM)
        lb = lb_all[l].reshape(HGRN_HEADS, HGRN_HEAD_DIM)
        r = hgrn2(hq.reshape(hshape), hf.reshape(hshape), hi_.reshape(hshape), lb)
        r = r * lax.rsqrt(jnp.mean(jnp.square(r), axis=-1, keepdims=True) + RMS_EPS)
        r = r.reshape(B_, S, HGRN_WIDTH) * hgrn_norm_w[l].astype(jnp.float32)
        r = (r * jax.nn.sigmoid(hg.astype(jnp.float32))).astype(x.dtype)
        mix = jnp.einsum('bse,ed->bsd', jnp.concatenate([a, r], axis=-1), w_o[l])
        x = layer_norm(DEEPNORM_ALPHA * x + mix, ln1_g[l], ln1_b[l])
        moe = hier_moe(x, w_group[l], b_group[l], w_expert[l], b_expert[l], w_gate[l], w_up[l], w_down[l])
        x = layer_norm(DEEPNORM_ALPHA * x + moe, ln2_g[l], ln2_b[l])
    return x
```

```python
import functools
import math

import jax
import jax.numpy as jnp
from jax import lax
from jax.experimental import pallas as pl
from jax.experimental.pallas import tpu as pltpu

F32 = jnp.float32
BF16 = jnp.bfloat16

D_MODEL = 1024
BATCH = 2
SEQ = 8192
N_TOK = BATCH * SEQ
HEADS = 8
HEAD_DIM = 64
WIDTH = HEADS * HEAD_DIM
IN_WIDTH = 7 * WIDTH
MOBA_BLOCK = 256
N_BLOCKS = SEQ // MOBA_BLOCK
MOBA_TOPK = 3
HGRN_CHUNK = 64
N_BUCKETS = 32
MAX_DISTANCE = 128
N_GROUPS = 4
EXPERTS_PER_GROUP = 4
N_EXPERTS = 16
D_EXPERT = 512
LN_EPS = 1e-5
RMS_EPS = 1e-6
ALPHA = 2.0 ** 0.25

LANES = 128
NEG = -1e30
VMEM_LIMIT = 48 * 1024 * 1024

PROJ_TM = 1024
OUT_TM = 512
MOE_TM = 512
HGRN_ROWS = 256

BIAS_HI_LANE = N_BLOCKS
BIAS_LO_LANE = N_BLOCKS + 1


def _nt_dot(a, b):
    return lax.dot_general(a, b, (((1,), (1,)), ((), ())), preferred_element_type=F32)


def _tn_dot(a, b):
    return lax.dot_general(a, b, (((0,), (0,)), ((), ())), preferred_element_type=F32)


def _layer_norm(y, g, b):
    mu = jnp.mean(y, axis=-1, keepdims=True)
    d = y - mu
    var = jnp.mean(d * d, axis=-1, keepdims=True)
    return d * lax.rsqrt(var + LN_EPS) * g + b


def _bias_tile_kernel(tab_ref, out_ref):
    h = pl.program_id(0)
    qi = lax.broadcasted_iota(jnp.int32, (MOBA_BLOCK, MOBA_BLOCK), 0)
    ki = lax.broadcasted_iota(jnp.int32, (MOBA_BLOCK, MOBA_BLOCK), 1)
    max_exact = N_BUCKETS // 2
    for w in range(2):
        n = jnp.maximum(qi - ki + MOBA_BLOCK * w, 0)
        nf = jnp.maximum(n, 1).astype(F32)
        large = max_exact + (jnp.log(nf / max_exact) / math.log(MAX_DISTANCE / max_exact)
                             * (N_BUCKETS - max_exact)).astype(jnp.int32)
        large = jnp.minimum(large, N_BUCKETS - 1)
        bucket = jnp.where(n < max_exact, n, large)
        acc = jnp.zeros((MOBA_BLOCK, MOBA_BLOCK), F32)
        for bk in range(N_BUCKETS):
            acc = jnp.where(bucket == bk, tab_ref[bk, h], acc)
        out_ref[0, w] = acc


def _bias_tiles(t5_bias):
    return pl.pallas_call(
        _bias_tile_kernel,
        grid=(HEADS,),
        in_specs=[pl.BlockSpec(memory_space=pltpu.SMEM)],
        out_specs=pl.BlockSpec((1, 2, MOBA_BLOCK, MOBA_BLOCK), lambda h: (h, 0, 0, 0)),
        out_shape=jax.ShapeDtypeStruct((HEADS, 2, MOBA_BLOCK, MOBA_BLOCK), F32),
        name="t5_bias_tiles",
    )(t5_bias)


def _proj_kernel(x_ref, w_ref, o_ref, f_ref, xb_ref):
    n = pl.program_id(1)

    @pl.when(n == 0)
    def _():
        xb_ref[...] = x_ref[...].astype(BF16)

    acc = jnp.dot(xb_ref[...], w_ref[...], preferred_element_type=F32)
    o_ref[...] = acc.astype(BF16)

    @pl.when(n == 4)
    def _():
        f_ref[...] = acc


def _proj(x2, w_in_b):
    return pl.pallas_call(
        _proj_kernel,
        grid=(N_TOK // PROJ_TM, IN_WIDTH // WIDTH),
        in_specs=[
            pl.BlockSpec((PROJ_TM, D_MODEL), lambda i, n: (i, 0)),
            pl.BlockSpec((D_MODEL, WIDTH), lambda i, n: (0, n)),
        ],
        out_specs=[
            pl.BlockSpec((PROJ_TM, WIDTH), lambda i, n: (i, n)),
            pl.BlockSpec((PROJ_TM, WIDTH), lambda i, n: (i, 0)),
        ],
        out_shape=[
            jax.ShapeDtypeStruct((N_TOK, IN_WIDTH), BF16),
            jax.ShapeDtypeStruct((N_TOK, WIDTH), F32),
        ],
        scratch_shapes=[pltpu.VMEM((PROJ_TM, D_MODEL), BF16)],
        compiler_params=pltpu.CompilerParams(
            dimension_semantics=("arbitrary", "arbitrary"), vmem_limit_bytes=VMEM_LIMIT),
        name="in_proj",
    )(x2, w_in_b)


def _attn_kernel(tab_ref, q_ref, k_ref, v_ref, bias_ref, o_ref, kmh_ref, kml_ref):
    assert MOBA_BLOCK == 1 << 8
    pair = pl.program_id(1)
    own = pl.program_id(2)
    lane = lax.broadcasted_iota(jnp.int32, (MOBA_BLOCK, LANES), 1)
    lane_f = lane.astype(F32)

    @pl.when(own == 0)
    def _():
        r = lax.broadcasted_iota(jnp.int32, (LANES, SEQ), 1)
        nb = lax.broadcasted_iota(jnp.int32, (LANES, SEQ), 0)
        avg = jnp.where((r >> 8) == nb, 1.0 / MOBA_BLOCK, 0.0).astype(BF16)
        km = jnp.dot(avg, k_ref[...], preferred_element_type=F32)
        hi = km.astype(BF16)
        kmh_ref[...] = hi
        kml_ref[...] = (km - hi.astype(F32)).astype(BF16)

    q2 = q_ref[...]
    row = lax.broadcasted_iota(jnp.int32, (MOBA_BLOCK, MOBA_BLOCK), 0)
    col = lax.broadcasted_iota(jnp.int32, (MOBA_BLOCK, MOBA_BLOCK), 1)
    own_start = pl.multiple_of(own * MOBA_BLOCK, MOBA_BLOCK)
    prev_start = pl.multiple_of(jnp.maximum(own - 1, 0) * MOBA_BLOCK, MOBA_BLOCK)

    outs = []
    for hh in range(2):
        head = pair * 2 + hh
        in_head = (lane >= hh * HEAD_DIM) & (lane < (hh + 1) * HEAD_DIM)
        qs = jnp.where(in_head, q2, jnp.zeros_like(q2)) * jnp.asarray(HEAD_DIM ** -0.5, BF16)

        gate = _nt_dot(qs, kmh_ref[...]) + _nt_dot(qs, kml_ref[...])
        eligible = lane < own
        g = jnp.where(eligible, gate, -jnp.inf)
        sel = jnp.zeros((MOBA_BLOCK, LANES), jnp.bool_)
        for _ in range(MOBA_TOPK):
            mx = jnp.max(g, axis=-1, keepdims=True)
            idx = jnp.min(jnp.where(g == mx, lane_f, float(LANES)), axis=-1, keepdims=True)
            hit = lane_f == idx
            sel = sel | hit
            g = jnp.where(hit, -jnp.inf, g)
        sel = sel & eligible

        b31 = jnp.full((MOBA_BLOCK, LANES), tab_ref[N_BUCKETS - 1, head], F32)
        b31_hi = b31.astype(BF16).astype(F32)
        aug = jnp.where(sel, 0.0, NEG)
        aug = jnp.where(lane == BIAS_HI_LANE, b31_hi, aug)
        aug = jnp.where(lane == BIAS_LO_LANE, b31 - b31_hi, aug)
        aug = jnp.where(lane < BIAS_LO_LANE + 1, aug, 0.0)
        q_aug = jnp.concatenate([qs, aug.astype(BF16)], axis=1)

        k_own = k_ref[pl.ds(own_start, MOBA_BLOCK), :]
        v_own = v_ref[pl.ds(own_start, MOBA_BLOCK), :]
        s = _nt_dot(qs, k_own) + bias_ref[hh, 0]
        s = jnp.where(row >= col, s, NEG)
        m = jnp.max(s, axis=-1, keepdims=True)
        p = jnp.exp(s - m)
        l = jnp.sum(p, axis=-1, keepdims=True)
        acc = jnp.dot(p.astype(BF16), v_own, preferred_element_type=F32)

        def update(s, v_t, m, l, acc):
            m_new = jnp.maximum(m, jnp.max(s, axis=-1, keepdims=True))
            alpha = jnp.exp(m - m_new)
            p = jnp.exp(s - m_new)
            l = alpha * l + jnp.sum(p, axis=-1, keepdims=True)
            acc = alpha * acc + jnp.dot(p.astype(BF16), v_t, preferred_element_type=F32)
            return m_new, l, acc

        k_prev = k_ref[pl.ds(prev_start, MOBA_BLOCK), :]
        v_prev = v_ref[pl.ds(prev_start, MOBA_BLOCK), :]
        ext = jnp.where(lane == own - 1, 1.0, 0.0).astype(BF16)
        s = _nt_dot(q_aug, jnp.concatenate([k_prev, ext], axis=1)) + bias_ref[hh, 1]
        s = jnp.where(own >= 1, s, NEG)
        m, l, acc = update(s, v_prev, m, l, acc)

        def far_body(j, carry):
            m, l, acc = carry
            start = pl.multiple_of(j * MOBA_BLOCK, MOBA_BLOCK)
            k_t = k_ref[pl.ds(start, MOBA_BLOCK), :]
            v_t = v_ref[pl.ds(start, MOBA_BLOCK), :]
            ext = jnp.where((lane == j) | (lane == BIAS_HI_LANE) | (lane == BIAS_LO_LANE),
                            1.0, 0.0).astype(BF16)
            s = _nt_dot(q_aug, jnp.concatenate([k_t, ext], axis=1))
            return update(s, v_t, m, l, acc)

        m, l, acc = lax.fori_loop(0, jnp.maximum(own - 1, 0), far_body, (m, l, acc))
        outs.append(acc / l)

    o_ref[...] = jnp.where(lane < HEAD_DIM, outs[0], outs[1]).astype(BF16)


def _attention(t5_bias, proj3, bias_tiles):
    n_pairs = HEADS // 2
    return pl.pallas_call(
        _attn_kernel,
        grid=(BATCH, n_pairs, N_BLOCKS),
        in_specs=[
            pl.BlockSpec(memory_space=pltpu.SMEM),
            pl.BlockSpec((None, MOBA_BLOCK, LANES), lambda b, p, i: (b, i, p)),
            pl.BlockSpec((None, SEQ, LANES), lambda b, p, i: (b, 0, n_pairs + p)),
            pl.BlockSpec((None, SEQ, LANES), lambda b, p, i: (b, 0, 2 * n_pairs + p)),
            pl.BlockSpec((2, 2, MOBA_BLOCK, MOBA_BLOCK), lambda b, p, i: (p, 0, 0, 0)),
        ],
        out_specs=pl.BlockSpec((None, MOBA_BLOCK, LANES), lambda b, p, i: (b, i, p)),
        out_shape=jax.ShapeDtypeStruct((BATCH, SEQ, WIDTH), BF16),
        scratch_shapes=[pltpu.VMEM((LANES, LANES), BF16), pltpu.VMEM((LANES, LANES), BF16)],
        compiler_params=pltpu.CompilerParams(
            dimension_semantics=("arbitrary", "arbitrary", "arbitrary"),
            vmem_limit_bytes=VMEM_LIMIT),
        name="moba_attention",
    )(t5_bias, proj3, proj3, proj3, bias_tiles)


def _hgrn_kernel(q_ref, f_ref, i_ref, g_ref, lbl_ref, nw_ref, o_ref,
                 state_ref, bpad_ref, kpad_ref, vpad_ref):
    C = HGRN_CHUNK
    half = WIDTH // 2

    @pl.when(pl.program_id(1) == 0)
    def _():
        state_ref[...] = jnp.zeros_like(state_ref)

    zpad = jnp.zeros((8, WIDTH), F32)
    bpad_ref[0:8, :] = zpad
    kpad_ref[0:8, :] = zpad
    vpad_ref[0:8, :] = zpad

    l0 = lbl_ref[0:1, :]
    l1 = lbl_ref[1:2, :]
    lmx = jnp.maximum(l0, l1)
    e0 = jnp.exp(l0 - lmx)
    lb = e0 / (e0 + jnp.exp(l1 - lmx))
    nw = nw_ref[...]

    ri = lax.broadcasted_iota(jnp.int32, (C, C), 0)
    ci = lax.broadcasted_iota(jnp.int32, (C, C), 1)
    tril = jnp.where(ri >= ci, 1.0, 0.0).astype(F32)
    assert HEAD_DIM == 1 << 6
    hr = lax.broadcasted_iota(jnp.int32, (WIDTH, WIDTH), 0) >> 6
    hc = lax.broadcasted_iota(jnp.int32, (WIDTH, WIDTH), 1) >> 6
    head_ones = jnp.where(hr == hc, 1.0, 0.0).astype(BF16)
    sr = lax.broadcasted_iota(jnp.int32, (half, half), 0) >> 6
    sc = lax.broadcasted_iota(jnp.int32, (half, half), 1) >> 6
    same_head = sr == sc

    def chunk(c, _):
        r0 = pl.multiple_of(c * C, C)
        qq = q_ref[pl.ds(r0, C), :].astype(F32)
        qq = qq * jax.nn.sigmoid(qq)
        f = lb + (1.0 - lb) * jax.nn.sigmoid(f_ref[pl.ds(r0, C), :])
        gl = jnp.log(f)
        kk = 1.0 - f
        vv = i_ref[pl.ds(r0, C), :].astype(F32)
        b = jnp.dot(tril, gl, preferred_element_type=F32, precision=lax.Precision.HIGHEST)

        bpad_ref[8:8 + C, :] = b
        kpad_ref[8:8 + C, :] = kk
        vpad_ref[8:8 + C, :] = vv

        o_parts = []
        for j in range(C // 8):
            rows = C - 8 * j
            bt = b[8 * j:, :]
            qt = qq[8 * j:, :]
            ii = lax.broadcasted_iota(jnp.int32, (rows, WIDTH), 0)
            terms = []
            vs_all = []
            for dd in range(8):
                bs = bpad_ref[8 - dd:8 - dd + rows, :]
                ks = kpad_ref[8 - dd:8 - dd + rows, :]
                vs_all.append(vpad_ref[8 - dd:8 - dd + rows, :])
                t = qt * ks * jnp.exp(bt - bs)
                terms.append(jnp.where(ii >= dd, t, 0.0))
            a = jnp.dot(jnp.concatenate(terms, axis=0).astype(BF16), head_ones,
                        preferred_element_type=F32)
            av = a * jnp.concatenate(vs_all, axis=0)
            part = av[0:rows]
            for dd in range(1, 8):
                part = part + av[dd * rows:(dd + 1) * rows]
            o_parts.append(part)
        o = o_parts[0]
        for j in range(1, C // 8):
            o = o + jnp.concatenate([jnp.zeros((8 * j, WIDTH), F32), o_parts[j]], axis=0)

        qe = (qq * jnp.exp(b)).astype(BF16)
        b_last = b[C - 1:C, :]
        kd = kk * jnp.exp(b_last - b)
        dec = jnp.exp(b_last)
        inter = []
        for a_i in range(2):
            sl = slice(a_i * half, (a_i + 1) * half)
            st = state_ref[a_i]
            inter.append(_nt_dot(qe[:, sl], st.astype(BF16)))
            upd = _tn_dot(vv[:, sl].astype(BF16), kd[:, sl].astype(BF16))
            state_ref[a_i] = st * dec[:, sl] + jnp.where(same_head, upd, 0.0)
        o = o + jnp.concatenate(inter, axis=1)

        ms = jnp.dot((o * o).astype(BF16), head_ones, preferred_element_type=F32) * (1.0 / HEAD_DIM)
        r = o * lax.rsqrt(ms + RMS_EPS) * nw
        r = r * jax.nn.sigmoid(g_ref[pl.ds(r0, C), :].astype(F32))
        o_ref[pl.ds(r0, C), :] = r.astype(BF16)
        return 0

    lax.fori_loop(0, HGRN_ROWS // C, chunk, 0)


def _hgrn(proj3, f3, lb_logits, norm_w):
    blk = lambda col: pl.BlockSpec((None, HGRN_ROWS, WIDTH), lambda b, i, col=col: (b, i, col))
    return pl.pallas_call(
        _hgrn_kernel,
        grid=(BATCH, SEQ // HGRN_ROWS),
        in_specs=[
            blk(3),
            pl.BlockSpec((None, HGRN_ROWS, WIDTH), lambda b, i: (b, i, 0)),
            blk(5),
            blk(6),
            pl.BlockSpec((2, WIDTH), lambda b, i: (0, 0)),
            pl.BlockSpec((1, WIDTH), lambda b, i: (0, 0)),
        ],
        out_specs=pl.BlockSpec((None, HGRN_ROWS, WIDTH), lambda b, i: (b, i, 0)),
        out_shape=jax.ShapeDtypeStruct((BATCH, SEQ, WIDTH), BF16),
        scratch_shapes=[
            pltpu.VMEM((2, WIDTH // 2, WIDTH // 2), F32),
            pltpu.VMEM((8 + HGRN_CHUNK, WIDTH), F32),
            pltpu.VMEM((8 + HGRN_CHUNK, WIDTH), F32),
            pltpu.VMEM((8 + HGRN_CHUNK, WIDTH), F32),
        ],
        compiler_params=pltpu.CompilerParams(
            dimension_semantics=("arbitrary", "arbitrary"), vmem_limit_bytes=VMEM_LIMIT),
        name="hgrn2",
    )(proj3, f3, proj3, proj3, lb_logits, norm_w)


def _out_kernel(a_ref, r_ref, x_ref, wo_ref, g_ref, b_ref, wr_ref, br_ref,
                x1_ref, x1b_ref, comb_ref):
    mix = jnp.dot(a_ref[...], wo_ref[0:WIDTH, :], preferred_element_type=F32)
    mix = mix + jnp.dot(r_ref[...], wo_ref[WIDTH:, :], preferred_element_type=F32)
    x1 = _layer_norm(ALPHA * x_ref[...] + mix, g_ref[...], b_ref[...])
    x1_ref[...] = x1
    x1b_ref[...] = x1.astype(BF16)

    logits = jnp.dot(x1, wr_ref[...], preferred_element_type=F32,
                     precision=lax.Precision.HIGHEST) + br_ref[...]
    assert EXPERTS_PER_GROUP == 1 << 2
    lane_i = lax.broadcasted_iota(jnp.int32, logits.shape, 1)
    lane = lane_i.astype(F32)
    grp_of_lane = (lane_i >> 2).astype(F32)
    none = float(LANES)
    is_g = (lane_i >= N_EXPERTS) & (lane_i < N_EXPERTS + N_GROUPS)
    gl = jnp.where(is_g, logits, -jnp.inf)
    ge = jnp.exp(gl - jnp.max(gl, axis=-1, keepdims=True))
    gp = ge / jnp.sum(ge, axis=-1, keepdims=True)
    g_w = jnp.max(gp, axis=-1, keepdims=True)
    g_lane = jnp.min(jnp.where(is_g & (gp == g_w), lane, none), axis=-1, keepdims=True)
    g_idx = g_lane - float(N_EXPERTS)

    in_grp = (lane_i < N_EXPERTS) & (grp_of_lane == g_idx)
    el = jnp.where(in_grp, logits, -jnp.inf)
    ee = jnp.exp(el - jnp.max(el, axis=-1, keepdims=True))
    ep = ee / jnp.sum(ee, axis=-1, keepdims=True)
    p1 = jnp.max(ep, axis=-1, keepdims=True)
    i1 = jnp.min(jnp.where(in_grp & (ep == p1), lane, none), axis=-1, keepdims=True)
    rest = in_grp & (lane != i1)
    ep2 = jnp.where(rest, ep, -1.0)
    p2 = jnp.max(ep2, axis=-1, keepdims=True)
    i2 = jnp.min(jnp.where(rest & (ep2 == p2), lane, none), axis=-1, keepdims=True)
    den = p1 + p2
    comb = jnp.where(lane == i1, g_w * (p1 / den), 0.0)
    comb = jnp.where(lane == i2, g_w * (p2 / den), comb)
    comb_ref[...] = comb


def _out_proj(a2, r2, x2, wo_b, g1, b1, w_route, b_route):
    row = lambda w: pl.BlockSpec((OUT_TM, w), lambda i: (i, 0))
    full = lambda s: pl.BlockSpec(s, lambda i: (0, 0))
    return pl.pallas_call(
        _out_kernel,
        grid=(N_TOK // OUT_TM,),
        in_specs=[row(WIDTH), row(WIDTH), row(D_MODEL), full((D_MODEL, D_MODEL)),
                  full((1, D_MODEL)), full((1, D_MODEL)), full((D_MODEL, LANES)), full((1, LANES))],
        out_specs=[row(D_MODEL), row(D_MODEL), row(LANES)],
        out_shape=[jax.ShapeDtypeStruct((N_TOK, D_MODEL), F32),
                   jax.ShapeDtypeStruct((N_TOK, D_MODEL), BF16),
                   jax.ShapeDtypeStruct((N_TOK, LANES), F32)],
        compiler_params=pltpu.CompilerParams(
            dimension_semantics=("arbitrary",), vmem_limit_bytes=VMEM_LIMIT),
        name="out_proj_ln_route",
    )(a2, r2, x2, wo_b, g1, b1, w_route, b_route)


def _moe_kernel(xb_ref, x1_ref, comb_ref, wg_ref, wu_ref, wd_ref, g_ref, b_ref, o_ref, acc_ref):
    e = pl.program_id(1)
    xb = xb_ref[...]
    hg = jnp.dot(xb, wg_ref[...], preferred_element_type=F32)
    hu = jnp.dot(xb, wu_ref[...], preferred_element_type=F32)
    h = (hg * jax.nn.sigmoid(hg) * hu).astype(BF16)
    y = jnp.dot(h, wd_ref[...], preferred_element_type=F32)
    comb = comb_ref[...]
    lane = lax.broadcasted_iota(jnp.int32, comb.shape, 1)
    c = jnp.sum(jnp.where(lane == e, comb, 0.0), axis=-1, keepdims=True)

    @pl.when(e == 0)
    def _():
        acc_ref[...] = c * y

    @pl.when(e > 0)
    def _():
        acc_ref[...] += c * y

    @pl.when(e == N_EXPERTS - 1)
    def _():
        o_ref[...] = _layer_norm(ALPHA * x1_ref[...] + acc_ref[...], g_ref[...], b_ref[...])


def _moe(x1b, x1, comb, wg_b, wu_b, wd_b, g2, b2):
    return pl.pallas_call(
        _moe_kernel,
        grid=(N_TOK // MOE_TM, N_EXPERTS),
        in_specs=[
            pl.BlockSpec((MOE_TM, D_MODEL), lambda i, e: (i, 0)),
            pl.BlockSpec((MOE_TM, D_MODEL), lambda i, e: (i, 0)),
            pl.BlockSpec((MOE_TM, LANES), lambda i, e: (i, 0)),
            pl.BlockSpec((None, D_MODEL, D_EXPERT), lambda i, e: (e, 0, 0)),
            pl.BlockSpec((None, D_MODEL, D_EXPERT), lambda i, e: (e, 0, 0)),
            pl.BlockSpec((None, D_EXPERT, D_MODEL), lambda i, e: (e, 0, 0)),
            pl.BlockSpec((1, D_MODEL), lambda i, e: (0, 0)),
            pl.BlockSpec((1, D_MODEL), lambda i, e: (0, 0)),
        ],
        out_specs=pl.BlockSpec((MOE_TM, D_MODEL), lambda i, e: (i, 0)),
        out_shape=jax.ShapeDtypeStruct((N_TOK, D_MODEL), F32),
        scratch_shapes=[pltpu.VMEM((MOE_TM, D_MODEL), F32)],
        compiler_params=pltpu.CompilerParams(
            dimension_semantics=("arbitrary", "arbitrary"), vmem_limit_bytes=VMEM_LIMIT),
        name="moe_experts_ln",
    )(x1b, x1, comb, wg_b, wu_b, wd_b, g2, b2)


def kernel(x, w_in, t5_bias, hgrn_lb_logits, hgrn_norm_w, w_o, ln1_g, ln1_b, w_group, b_group,
           w_expert, b_expert, w_gate, w_up, w_down, ln2_g, ln2_b):
    x2 = x.reshape(N_TOK, D_MODEL)
    proj, f_logits = _proj(x2, w_in[0].astype(BF16))
    proj3 = proj.reshape(BATCH, SEQ, IN_WIDTH)

    a = _attention(t5_bias, proj3, _bias_tiles(t5_bias))
    r = _hgrn(proj3, f_logits.reshape(BATCH, SEQ, WIDTH), hgrn_lb_logits, hgrn_norm_w)

    pad = LANES - N_EXPERTS - N_GROUPS
    w_route = jnp.concatenate(
        [w_expert[0].transpose(1, 0, 2).reshape(D_MODEL, N_EXPERTS), w_group[0],
         jnp.zeros((D_MODEL, pad), F32)], axis=1)
    b_route = jnp.concatenate(
        [b_expert[0].reshape(N_EXPERTS), b_group[0], jnp.zeros((pad,), F32)]).reshape(1, LANES)
    x1, x1b, comb = _out_proj(
        a.reshape(N_TOK, WIDTH), r.reshape(N_TOK, WIDTH), x2, w_o[0].astype(BF16),
        ln1_g, ln1_b, w_route, b_route)

    out = _moe(x1b, x1, comb, w_gate[0].astype(BF16), w_up[0].astype(BF16),
               w_down[0].astype(BF16), ln2_g, ln2_b)
    return out.reshape(BATCH, SEQ, D_MODEL)
```

```python
import functools
import math

import jax
import jax.numpy as jnp
from jax import lax
from jax.experimental import pallas as pl
from jax.experimental.pallas import tpu as pltpu

F32 = jnp.float32
BF16 = jnp.bfloat16

D_MODEL = 1024
BATCH = 2
SEQ = 8192
N_TOK = BATCH * SEQ
HEADS = 8
HEAD_DIM = 64
WIDTH = HEADS * HEAD_DIM
IN_WIDTH = 7 * WIDTH
MOBA_BLOCK = 256
N_BLOCKS = SEQ // MOBA_BLOCK
MOBA_TOPK = 3
HGRN_CHUNK = 64
N_BUCKETS = 32
MAX_DISTANCE = 128
N_GROUPS = 4
EXPERTS_PER_GROUP = 4
N_EXPERTS = 16
D_EXPERT = 512
LN_EPS = 1e-5
RMS_EPS = 1e-6
ALPHA = 2.0 ** 0.25

LANES = 128
BF16_SUBLANES = 16
NEG = -1e30
VMEM_LIMIT = 48 * 1024 * 1024

PROJ_TM = 1024
OUT_TM = 512
MOE_TM = 512
HGRN_ROWS = 256
FAR_BLOCKS = 2
FAR_KEYS = FAR_BLOCKS * MOBA_BLOCK
N_FAR_STEPS = N_BLOCKS // FAR_BLOCKS

BIAS_HI_LANE = N_BLOCKS
BIAS_LO_LANE = N_BLOCKS + 1


def _nt_dot(a, b):
    return lax.dot_general(a, b, (((1,), (1,)), ((), ())), preferred_element_type=F32)


def _tn_dot(a, b):
    return lax.dot_general(a, b, (((0,), (0,)), ((), ())), preferred_element_type=F32)


def _layer_norm(y, g, b):
    mu = jnp.mean(y, axis=-1, keepdims=True)
    d = y - mu
    var = jnp.mean(d * d, axis=-1, keepdims=True)
    return d * lax.rsqrt(var + LN_EPS) * g + b


def _bias_tile_kernel(tab_ref, out_ref):
    h = pl.program_id(0)
    ki = lax.broadcasted_iota(jnp.int32, (MOBA_BLOCK, MOBA_BLOCK), 0)
    qi = lax.broadcasted_iota(jnp.int32, (MOBA_BLOCK, MOBA_BLOCK), 1)
    max_exact = N_BUCKETS // 2
    for w in range(2):
        n = jnp.maximum(qi - ki + MOBA_BLOCK * w, 0)
        nf = jnp.maximum(n, 1).astype(F32)
        large = max_exact + (jnp.log(nf / max_exact) / math.log(MAX_DISTANCE / max_exact)
                             * (N_BUCKETS - max_exact)).astype(jnp.int32)
        large = jnp.minimum(large, N_BUCKETS - 1)
        bucket = jnp.where(n < max_exact, n, large)
        acc = jnp.zeros((MOBA_BLOCK, MOBA_BLOCK), F32)
        for bk in range(N_BUCKETS):
            acc = jnp.where(bucket == bk, tab_ref[bk, h], acc)
        out_ref[0, w] = acc


def _bias_tiles(t5_bias):
    return pl.pallas_call(
        _bias_tile_kernel,
        grid=(HEADS,),
        in_specs=[pl.BlockSpec(memory_space=pltpu.SMEM)],
        out_specs=pl.BlockSpec((1, 2, MOBA_BLOCK, MOBA_BLOCK), lambda h: (h, 0, 0, 0)),
        out_shape=jax.ShapeDtypeStruct((HEADS, 2, MOBA_BLOCK, MOBA_BLOCK), F32),
        name="t5_bias_tiles",
    )(t5_bias)


def _proj_kernel(x_ref, w_ref, wvt_ref, o_ref, f_ref, vt_ref, xb_ref):
    n = pl.program_id(1)

    @pl.when(n == 0)
    def _():
        xb_ref[...] = x_ref[...].astype(BF16)

    acc = jnp.dot(xb_ref[...], w_ref[...], preferred_element_type=F32)
    o_ref[...] = acc.astype(BF16)

    @pl.when(n == 2)
    def _():
        vt = _nt_dot(wvt_ref[...], xb_ref[...]).astype(BF16)
        for c in range(PROJ_TM // MOBA_BLOCK):
            vt_ref[c] = vt[:, c * MOBA_BLOCK:(c + 1) * MOBA_BLOCK]

    @pl.when(n == 4)
    def _():
        f_ref[...] = acc


def _proj(x2, w_in_b, w_vt_b):
    tiles_per_seq = SEQ // PROJ_TM
    blocks_per_tile = PROJ_TM // MOBA_BLOCK
    return pl.pallas_call(
        _proj_kernel,
        grid=(N_TOK // PROJ_TM, IN_WIDTH // WIDTH),
        in_specs=[
            pl.BlockSpec((PROJ_TM, D_MODEL), lambda i, n: (i, 0)),
            pl.BlockSpec((D_MODEL, WIDTH), lambda i, n: (0, n)),
            pl.BlockSpec((WIDTH, D_MODEL), lambda i, n: (0, 0)),
        ],
        out_specs=[
            pl.BlockSpec((PROJ_TM, WIDTH), lambda i, n: (i, n)),
            pl.BlockSpec((PROJ_TM, WIDTH), lambda i, n: (i, 0)),
            pl.BlockSpec((None, blocks_per_tile, WIDTH, MOBA_BLOCK),
                         lambda i, n: (i // tiles_per_seq, i % tiles_per_seq, 0, 0)),
        ],
        out_shape=[
            jax.ShapeDtypeStruct((N_TOK, IN_WIDTH), BF16),
            jax.ShapeDtypeStruct((N_TOK, WIDTH), F32),
            jax.ShapeDtypeStruct((BATCH, N_BLOCKS, WIDTH, MOBA_BLOCK), BF16),
        ],
        scratch_shapes=[pltpu.VMEM((PROJ_TM, D_MODEL), BF16)],
        compiler_params=pltpu.CompilerParams(
            dimension_semantics=("arbitrary", "arbitrary"), vmem_limit_bytes=VMEM_LIMIT),
        name="in_proj",
    )(x2, w_in_b, w_vt_b)


def _attn_kernel(tab_ref, q_ref, k_ref, vt_ref, bias_ref, o_ref,
                 kmh_ref, kml_ref, ext_ref, s_ref):
    assert MOBA_BLOCK == 1 << 8
    pair = pl.program_id(1)
    own = pl.program_id(2)
    lane = lax.broadcasted_iota(jnp.int32, (MOBA_BLOCK, LANES), 1)
    lane_f = lane.astype(F32)

    @pl.when(own == 0)
    def _():
        r = lax.broadcasted_iota(jnp.int32, (LANES, SEQ), 1)
        nb = lax.broadcasted_iota(jnp.int32, (LANES, SEQ), 0)
        avg = jnp.where((r >> 8) == nb, 1.0 / MOBA_BLOCK, 0.0).astype(BF16)
        km = jnp.dot(avg, k_ref[...], preferred_element_type=F32)
        hi = km.astype(BF16)
        kmh_ref[...] = hi
        kml_ref[...] = (km - hi.astype(F32)).astype(BF16)
        kr = lax.broadcasted_iota(jnp.int32, (SEQ, LANES), 0) >> 8
        kl = lax.broadcasted_iota(jnp.int32, (SEQ, LANES), 1)
        ext_ref[...] = jnp.where((kl == kr) | (kl == BIAS_HI_LANE) | (kl == BIAS_LO_LANE),
                                 1.0, 0.0).astype(BF16)

    q2 = q_ref[...]
    krow = lax.broadcasted_iota(jnp.int32, (MOBA_BLOCK, MOBA_BLOCK), 0)
    qcol = lax.broadcasted_iota(jnp.int32, (MOBA_BLOCK, MOBA_BLOCK), 1)
    own_start = pl.multiple_of(own * MOBA_BLOCK, MOBA_BLOCK)
    prev_start = pl.multiple_of(jnp.maximum(own - 1, 0) * MOBA_BLOCK, MOBA_BLOCK)
    k_own = k_ref[pl.ds(own_start, MOBA_BLOCK), :]
    k_prev = k_ref[pl.ds(prev_start, MOBA_BLOCK), :]
    ext_prev = jnp.where(lane == own - 1, 1.0, 0.0).astype(BF16)
    k_prev_aug = jnp.concatenate([k_prev, ext_prev], axis=1)
    ones_rows = jnp.ones((BF16_SUBLANES, MOBA_BLOCK), BF16)

    def v_aug(hh, block):
        vt = vt_ref[block, hh * HEAD_DIM:(hh + 1) * HEAD_DIM, :]
        return jnp.concatenate([vt, ones_rows], axis=0)

    q_far = []
    carry = []
    for hh in range(2):
        head = pair * 2 + hh
        in_head = (lane >= hh * HEAD_DIM) & (lane < (hh + 1) * HEAD_DIM)
        qs = jnp.where(in_head, q2, jnp.zeros_like(q2)) * jnp.asarray(HEAD_DIM ** -0.5, BF16)

        gate = _nt_dot(qs, kmh_ref[...]) + _nt_dot(qs, kml_ref[...])
        eligible = lane < own
        g = jnp.where(eligible, gate, -jnp.inf)
        sel = jnp.zeros((MOBA_BLOCK, LANES), jnp.bool_)
        for _ in range(MOBA_TOPK):
            mx = jnp.max(g, axis=-1, keepdims=True)
            idx = jnp.min(jnp.where(g == mx, lane_f, float(LANES)), axis=-1, keepdims=True)
            hit = lane_f == idx
            sel = sel | hit
            g = jnp.where(hit, -jnp.inf, g)
        sel = sel & eligible

        b31 = jnp.full((MOBA_BLOCK, LANES), tab_ref[N_BUCKETS - 1, head], F32)
        b31_hi = b31.astype(BF16).astype(F32)
        aug = jnp.where(sel, 0.0, NEG)
        aug = jnp.where(lane == BIAS_HI_LANE, b31_hi, aug)
        aug = jnp.where(lane == BIAS_LO_LANE, b31 - b31_hi, aug)
        aug = jnp.where(lane < BIAS_LO_LANE + 1, aug, 0.0)
        q_aug = jnp.concatenate([qs, aug.astype(BF16)], axis=1)
        aug_far = jnp.where(lane == own - 1, NEG, aug)
        q_far.append(jnp.concatenate([qs, aug_far.astype(BF16)], axis=1))

        s_prev = _nt_dot(k_prev_aug, q_aug) + bias_ref[hh, 1]
        s_prev = jnp.where(own >= 1, s_prev, NEG)
        s_own = _nt_dot(k_own, qs) + bias_ref[hh, 0]
        s_own = jnp.where(qcol >= krow, s_own, NEG)
        m = jnp.maximum(jnp.max(s_prev, axis=0, keepdims=True),
                        jnp.max(s_own, axis=0, keepdims=True))
        p = jnp.concatenate([jnp.exp(s_prev - m), jnp.exp(s_own - m)], axis=0).astype(BF16)
        va = jnp.concatenate([v_aug(hh, jnp.maximum(own - 1, 0)), v_aug(hh, own)], axis=1)
        acc = jnp.dot(va, p, preferred_element_type=F32)
        carry += [m, acc]

    q_cat = jnp.concatenate(q_far, axis=0)

    def scores(step, slot):
        start = pl.multiple_of(jnp.minimum(step, N_FAR_STEPS - 1) * FAR_KEYS, FAR_KEYS)
        k_aug = jnp.concatenate([k_ref[pl.ds(start, FAR_KEYS), :],
                                 ext_ref[pl.ds(start, FAR_KEYS), :]], axis=1)
        s_ref[slot] = _nt_dot(k_aug, q_cat)

    def absorb(step, slot, carry):
        block0 = jnp.minimum(step, N_FAR_STEPS - 1) * FAR_BLOCKS
        out = []
        for hh in range(2):
            m, acc = carry[2 * hh], carry[2 * hh + 1]
            s = s_ref[slot, :, hh * MOBA_BLOCK:(hh + 1) * MOBA_BLOCK]
            m_new = jnp.maximum(m, jnp.max(s, axis=0, keepdims=True))
            alpha = jnp.exp(m - m_new)
            p = jnp.exp(s - m_new).astype(BF16)
            va = jnp.concatenate([v_aug(hh, block0 + c) for c in range(FAR_BLOCKS)], axis=1)
            acc = alpha * acc + jnp.dot(va, p, preferred_element_type=F32)
            out += [m_new, acc]
        return tuple(out)

    n_steps = own // FAR_BLOCKS
    scores(0, 0)

    def far_body(i, carry):
        scores(2 * i + 1, 1)
        carry = absorb(2 * i, 0, carry)
        scores(2 * i + 2, 0)
        return absorb(2 * i + 1, 1, carry)

    carry = lax.fori_loop(0, (n_steps + 1) // 2, far_body, tuple(carry))

    outs = []
    for hh in range(2):
        acc = carry[2 * hh + 1]
        outs.append(acc[0:HEAD_DIM] / acc[HEAD_DIM:HEAD_DIM + 1])
    o_ref[...] = jnp.concatenate(outs, axis=0).T.astype(BF16)


def _attention(t5_bias, proj3, vt3, bias_tiles):
    n_pairs = HEADS // 2
    return pl.pallas_call(
        _attn_kernel,
        grid=(BATCH, n_pairs, N_BLOCKS),
        in_specs=[
            pl.BlockSpec(memory_space=pltpu.SMEM),
            pl.BlockSpec((None, MOBA_BLOCK, LANES), lambda b, p, i: (b, i, p)),
            pl.BlockSpec((None, SEQ, LANES), lambda b, p, i: (b, 0, n_pairs + p)),
            pl.BlockSpec((None, N_BLOCKS, LANES, MOBA_BLOCK), lambda b, p, i: (b, 0, p, 0)),
            pl.BlockSpec((2, 2, MOBA_BLOCK, MOBA_BLOCK), lambda b, p, i: (p, 0, 0, 0)),
        ],
        out_specs=pl.BlockSpec((None, MOBA_BLOCK, LANES), lambda b, p, i: (b, i, p)),
        out_shape=jax.ShapeDtypeStruct((BATCH, SEQ, WIDTH), BF16),
        scratch_shapes=[
            pltpu.VMEM((LANES, LANES), BF16),
            pltpu.VMEM((LANES, LANES), BF16),
            pltpu.VMEM((SEQ, LANES), BF16),
            pltpu.VMEM((2, FAR_KEYS, 2 * MOBA_BLOCK), F32),
        ],
        compiler_params=pltpu.CompilerParams(
            dimension_semantics=("arbitrary", "arbitrary", "arbitrary"),
            vmem_limit_bytes=VMEM_LIMIT),
        name="moba_attention",
    )(t5_bias, proj3, proj3, vt3, bias_tiles)


def _hgrn_kernel(q_ref, f_ref, i_ref, g_ref, lbl_ref, nw_ref, o_ref,
                 state_ref, bpad_ref, kpad_ref, vpad_ref):
    C = HGRN_CHUNK
    half = WIDTH // 2

    @pl.when(pl.program_id(1) == 0)
    def _():
        state_ref[...] = jnp.zeros_like(state_ref)

    zpad = jnp.zeros((8, WIDTH), F32)
    bpad_ref[0:8, :] = zpad
    kpad_ref[0:8, :] = zpad
    vpad_ref[0:8, :] = zpad

    l0 = lbl_ref[0:1, :]
    l1 = lbl_ref[1:2, :]
    lmx = jnp.maximum(l0, l1)
    e0 = jnp.exp(l0 - lmx)
    lb = e0 / (e0 + jnp.exp(l1 - lmx))
    nw = nw_ref[...]

    ri = lax.broadcasted_iota(jnp.int32, (C, C), 0)
    ci = lax.broadcasted_iota(jnp.int32, (C, C), 1)
    tril = jnp.where(ri >= ci, 1.0, 0.0).astype(F32)
    assert HEAD_DIM == 1 << 6
    hr = lax.broadcasted_iota(jnp.int32, (WIDTH, WIDTH), 0) >> 6
    hc = lax.broadcasted_iota(jnp.int32, (WIDTH, WIDTH), 1) >> 6
    head_ones = jnp.where(hr == hc, 1.0, 0.0).astype(BF16)
    sr = lax.broadcasted_iota(jnp.int32, (half, half), 0) >> 6
    sc = lax.broadcasted_iota(jnp.int32, (half, half), 1) >> 6
    same_head = sr == sc

    def chunk(c, _):
        r0 = pl.multiple_of(c * C, C)
        qq = q_ref[pl.ds(r0, C), :].astype(F32)
        qq = qq * jax.nn.sigmoid(qq)
        f = lb + (1.0 - lb) * jax.nn.sigmoid(f_ref[pl.ds(r0, C), :])
        gl = jnp.log(f)
        kk = 1.0 - f
        vv = i_ref[pl.ds(r0, C), :].astype(F32)
        b = jnp.dot(tril, gl, preferred_element_type=F32, precision=lax.Precision.HIGHEST)

        bpad_ref[8:8 + C, :] = b
        kpad_ref[8:8 + C, :] = kk
        vpad_ref[8:8 + C, :] = vv

        o_parts = []
        for j in range(C // 8):
            rows = C - 8 * j
            bt = b[8 * j:, :]
            qt = qq[8 * j:, :]
            ii = lax.broadcasted_iota(jnp.int32, (rows, WIDTH), 0)
            terms = []
            vs_all = []
            for dd in range(8):
                bs = bpad_ref[8 - dd:8 - dd + rows, :]
                ks = kpad_ref[8 - dd:8 - dd + rows, :]
                vs_all.append(vpad_ref[8 - dd:8 - dd + rows, :])
                t = qt * ks * jnp.exp(bt - bs)
                terms.append(jnp.where(ii >= dd, t, 0.0))
            a = jnp.dot(jnp.concatenate(terms, axis=0).astype(BF16), head_ones,
                        preferred_element_type=F32)
            av = a * jnp.concatenate(vs_all, axis=0)
            part = av[0:rows]
            for dd in range(1, 8):
                part = part + av[dd * rows:(dd + 1) * rows]
            o_parts.append(part)
        o = o_parts[0]
        for j in range(1, C // 8):
            o = o + jnp.concatenate([jnp.zeros((8 * j, WIDTH), F32), o_parts[j]], axis=0)

        qe = (qq * jnp.exp(b)).astype(BF16)
        b_last = b[C - 1:C, :]
        kd = kk * jnp.exp(b_last - b)
        dec = jnp.exp(b_last)
        inter = []
        for a_i in range(2):
            sl = slice(a_i * half, (a_i + 1) * half)
            st = state_ref[a_i]
            inter.append(_nt_dot(qe[:, sl], st.astype(BF16)))
            upd = _tn_dot(vv[:, sl].astype(BF16), kd[:, sl].astype(BF16))
            state_ref[a_i] = st * dec[:, sl] + jnp.where(same_head, upd, 0.0)
        o = o + jnp.concatenate(inter, axis=1)

        ms = jnp.dot((o * o).astype(BF16), head_ones, preferred_element_type=F32) * (1.0 / HEAD_DIM)
        r = o * lax.rsqrt(ms + RMS_EPS) * nw
        r = r * jax.nn.sigmoid(g_ref[pl.ds(r0, C), :].astype(F32))
        o_ref[pl.ds(r0, C), :] = r.astype(BF16)
        return 0

    lax.fori_loop(0, HGRN_ROWS // C, chunk, 0)


def _hgrn(proj3, f3, lb_logits, norm_w):
    blk = lambda col: pl.BlockSpec((None, HGRN_ROWS, WIDTH), lambda b, i, col=col: (b, i, col))
    return pl.pallas_call(
        _hgrn_kernel,
        grid=(BATCH, SEQ // HGRN_ROWS),
        in_specs=[
            blk(3),
            pl.BlockSpec((None, HGRN_ROWS, WIDTH), lambda b, i: (b, i, 0)),
            blk(5),
            blk(6),
            pl.BlockSpec((2, WIDTH), lambda b, i: (0, 0)),
            pl.BlockSpec((1, WIDTH), lambda b, i: (0, 0)),
        ],
        out_specs=pl.BlockSpec((None, HGRN_ROWS, WIDTH), lambda b, i: (b, i, 0)),
        out_shape=jax.ShapeDtypeStruct((BATCH, SEQ, WIDTH), BF16),
        scratch_shapes=[
            pltpu.VMEM((2, WIDTH // 2, WIDTH // 2), F32),
            pltpu.VMEM((8 + HGRN_CHUNK, WIDTH), F32),
            pltpu.VMEM((8 + HGRN_CHUNK, WIDTH), F32),
            pltpu.VMEM((8 + HGRN_CHUNK, WIDTH), F32),
        ],
        compiler_params=pltpu.CompilerParams(
            dimension_semantics=("arbitrary", "arbitrary"), vmem_limit_bytes=VMEM_LIMIT),
        name="hgrn2",
    )(proj3, f3, proj3, proj3, lb_logits, norm_w)


def _out_kernel(a_ref, r_ref, x_ref, wo_ref, g_ref, b_ref, wr_ref, br_ref,
                x1_ref, x1b_ref, comb_ref):
    mix = jnp.dot(a_ref[...], wo_ref[0:WIDTH, :], preferred_element_type=F32)
    mix = mix + jnp.dot(r_ref[...], wo_ref[WIDTH:, :], preferred_element_type=F32)
    x1 = _layer_norm(ALPHA * x_ref[...] + mix, g_ref[...], b_ref[...])
    x1_ref[...] = x1
    x1b_ref[...] = x1.astype(BF16)

    logits = jnp.dot(x1, wr_ref[...], preferred_element_type=F32,
                     precision=lax.Precision.HIGHEST) + br_ref[...]
    assert EXPERTS_PER_GROUP == 1 << 2
    lane_i = lax.broadcasted_iota(jnp.int32, logits.shape, 1)
    lane = lane_i.astype(F32)
    grp_of_lane = (lane_i >> 2).astype(F32)
    none = float(LANES)
    is_g = (lane_i >= N_EXPERTS) & (lane_i < N_EXPERTS + N_GROUPS)
    gl = jnp.where(is_g, logits, -jnp.inf)
    ge = jnp.exp(gl - jnp.max(gl, axis=-1, keepdims=True))
    gp = ge / jnp.sum(ge, axis=-1, keepdims=True)
    g_w = jnp.max(gp, axis=-1, keepdims=True)
    g_lane = jnp.min(jnp.where(is_g & (gp == g_w), lane, none), axis=-1, keepdims=True)
    g_idx = g_lane - float(N_EXPERTS)

    in_grp = (lane_i < N_EXPERTS) & (grp_of_lane == g_idx)
    el = jnp.where(in_grp, logits, -jnp.inf)
    ee = jnp.exp(el - jnp.max(el, axis=-1, keepdims=True))
    ep = ee / jnp.sum(ee, axis=-1, keepdims=True)
    p1 = jnp.max(ep, axis=-1, keepdims=True)
    i1 = jnp.min(jnp.where(in_grp & (ep == p1), lane, none), axis=-1, keepdims=True)
    rest = in_grp & (lane != i1)
    ep2 = jnp.where(rest, ep, -1.0)
    p2 = jnp.max(ep2, axis=-1, keepdims=True)
    i2 = jnp.min(jnp.where(rest & (ep2 == p2), lane, none), axis=-1, keepdims=True)
    den = p1 + p2
    comb = jnp.where(lane == i1, g_w * (p1 / den), 0.0)
    comb = jnp.where(lane == i2, g_w * (p2 / den), comb)
    comb_ref[...] = comb


def _out_proj(a2, r2, x2, wo_b, g1, b1, w_route, b_route):
    row = lambda w: pl.BlockSpec((OUT_TM, w), lambda i: (i, 0))
    full = lambda s: pl.BlockSpec(s, lambda i: (0, 0))
    return pl.pallas_call(
        _out_kernel,
        grid=(N_TOK // OUT_TM,),
        in_specs=[row(WIDTH), row(WIDTH), row(D_MODEL), full((D_MODEL, D_MODEL)),
                  full((1, D_MODEL)), full((1, D_MODEL)), full((D_MODEL, LANES)), full((1, LANES))],
        out_specs=[row(D_MODEL), row(D_MODEL), row(LANES)],
        out_shape=[jax.ShapeDtypeStruct((N_TOK, D_MODEL), F32),
                   jax.ShapeDtypeStruct((N_TOK, D_MODEL), BF16),
                   jax.ShapeDtypeStruct((N_TOK, LANES), F32)],
        compiler_params=pltpu.CompilerParams(
            dimension_semantics=("arbitrary",), vmem_limit_bytes=VMEM_LIMIT),
        name="out_proj_ln_route",
    )(a2, r2, x2, wo_b, g1, b1, w_route, b_route)


def _moe_kernel(xb_ref, x1_ref, comb_ref, wg_ref, wu_ref, wd_ref, g_ref, b_ref, o_ref, acc_ref):
    e = pl.program_id(1)
    xb = xb_ref[...]
    hg = jnp.dot(xb, wg_ref[...], preferred_element_type=F32)
    hu = jnp.dot(xb, wu_ref[...], preferred_element_type=F32)
    h = (hg * jax.nn.sigmoid(hg) * hu).astype(BF16)
    y = jnp.dot(h, wd_ref[...], preferred_element_type=F32)
    comb = comb_ref[...]
    lane = lax.broadcasted_iota(jnp.int32, comb.shape, 1)
    c = jnp.sum(jnp.where(lane == e, comb, 0.0), axis=-1, keepdims=True)

    @pl.when(e == 0)
    def _():
        acc_ref[...] = c * y

    @pl.when(e > 0)
    def _():
        acc_ref[...] += c * y

    @pl.when(e == N_EXPERTS - 1)
    def _():
        o_ref[...] = _layer_norm(ALPHA * x1_ref[...] + acc_ref[...], g_ref[...], b_ref[...])


def _moe(x1b, x1, comb, wg_b, wu_b, wd_b, g2, b2):
    return pl.pallas_call(
        _moe_kernel,
        grid=(N_TOK // MOE_TM, N_EXPERTS),
        in_specs=[
            pl.BlockSpec((MOE_TM, D_MODEL), lambda i, e: (i, 0)),
            pl.BlockSpec((MOE_TM, D_MODEL), lambda i, e: (i, 0)),
            pl.BlockSpec((MOE_TM, LANES), lambda i, e: (i, 0)),
            pl.BlockSpec((None, D_MODEL, D_EXPERT), lambda i, e: (e, 0, 0)),
            pl.BlockSpec((None, D_MODEL, D_EXPERT), lambda i, e: (e, 0, 0)),
            pl.BlockSpec((None, D_EXPERT, D_MODEL), lambda i, e: (e, 0, 0)),
            pl.BlockSpec((1, D_MODEL), lambda i, e: (0, 0)),
            pl.BlockSpec((1, D_MODEL), lambda i, e: (0, 0)),
        ],
        out_specs=pl.BlockSpec((MOE_TM, D_MODEL), lambda i, e: (i, 0)),
        out_shape=jax.ShapeDtypeStruct((N_TOK, D_MODEL), F32),
        scratch_shapes=[pltpu.VMEM((MOE_TM, D_MODEL), F32)],
        compiler_params=pltpu.CompilerParams(
            dimension_semantics=("arbitrary", "arbitrary"), vmem_limit_bytes=VMEM_LIMIT),
        name="moe_experts_ln",
    )(x1b, x1, comb, wg_b, wu_b, wd_b, g2, b2)


def kernel(x, w_in, t5_bias, hgrn_lb_logits, hgrn_norm_w, w_o, ln1_g, ln1_b, w_group, b_group,
           w_expert, b_expert, w_gate, w_up, w_down, ln2_g, ln2_b):
    x2 = x.reshape(N_TOK, D_MODEL)
    w_in_b = w_in[0].astype(BF16)
    w_vt_b = w_in_b[:, 2 * WIDTH:3 * WIDTH].T
    proj, f_logits, vt3 = _proj(x2, w_in_b, w_vt_b)
    proj3 = proj.reshape(BATCH, SEQ, IN_WIDTH)

    a = _attention(t5_bias, proj3, vt3, _bias_tiles(t5_bias))
    r = _hgrn(proj3, f_logits.reshape(BATCH, SEQ, WIDTH), hgrn_lb_logits, hgrn_norm_w)

    pad = LANES - N_EXPERTS - N_GROUPS
    w_route = jnp.concatenate(
        [w_expert[0].transpose(1, 0, 2).reshape(D_MODEL, N_EXPERTS), w_group[0],
         jnp.zeros((D_MODEL, pad), F32)], axis=1)
    b_route = jnp.concatenate(
        [b_expert[0].reshape(N_EXPERTS), b_group[0], jnp.zeros((pad,), F32)]).reshape(1, LANES)
    x1, x1b, comb = _out_proj(
        a.reshape(N_TOK, WIDTH), r.reshape(N_TOK, WIDTH), x2, w_o[0].astype(BF16),
        ln1_g, ln1_b, w_route, b_route)

    out = _moe(x1b, x1, comb, w_gate[0].astype(BF16), w_up[0].astype(BF16),
               w_down[0].astype(BF16), ln2_g, ln2_b)
    return out.reshape(BATCH, SEQ, D_MODEL)
```

```python
import functools
import math

import jax
import jax.numpy as jnp
from jax import lax
from jax.experimental import pallas as pl
from jax.experimental.pallas import tpu as pltpu

F32 = jnp.float32
BF16 = jnp.bfloat16

D_MODEL = 1024
BATCH = 2
SEQ = 8192
N_TOK = BATCH * SEQ
HEADS = 8
HEAD_DIM = 64
WIDTH = HEADS * HEAD_DIM
IN_WIDTH = 7 * WIDTH
MOBA_BLOCK = 256
N_BLOCKS = SEQ // MOBA_BLOCK
MOBA_TOPK = 3
HGRN_CHUNK = 32
N_BUCKETS = 32
MAX_DISTANCE = 128
N_GROUPS = 4
EXPERTS_PER_GROUP = 4
N_EXPERTS = 16
D_EXPERT = 512
LN_EPS = 1e-5
RMS_EPS = 1e-6
ALPHA = 2.0 ** 0.25
LOG2E = math.log2(math.e)

LANES = 128
BF16_SUBLANES = 16
NEG = -1e30
VMEM_LIMIT = 48 * 1024 * 1024

PROJ_TM = 1024
OUT_TM = 512
MOE_TM = 512
HGRN_ROWS = 256
FAR_BLOCKS = 2
FAR_KEYS = FAR_BLOCKS * MOBA_BLOCK
N_FAR_STEPS = N_BLOCKS // FAR_BLOCKS

BIAS_HI_LANE = N_BLOCKS
BIAS_LO_LANE = N_BLOCKS + 1


def _nt_dot(a, b):
    return lax.dot_general(a, b, (((1,), (1,)), ((), ())), preferred_element_type=F32)


def _tn_dot(a, b):
    return lax.dot_general(a, b, (((0,), (0,)), ((), ())), preferred_element_type=F32)


def _sigmoid(x):
    return 1.0 / (1.0 + jnp.exp2(x * (-LOG2E)))


def _layer_norm(y, g, b):
    mu = jnp.mean(y, axis=-1, keepdims=True)
    d = y - mu
    var = jnp.mean(d * d, axis=-1, keepdims=True)
    return d * lax.rsqrt(var + LN_EPS) * g + b


def _bias_tile_kernel(tab_ref, out_ref):
    h = pl.program_id(0)
    ki = lax.broadcasted_iota(jnp.int32, (MOBA_BLOCK, MOBA_BLOCK), 0)
    qi = lax.broadcasted_iota(jnp.int32, (MOBA_BLOCK, MOBA_BLOCK), 1)
    max_exact = N_BUCKETS // 2
    for w in range(2):
        n = jnp.maximum(qi - ki + MOBA_BLOCK * w, 0)
        nf = jnp.maximum(n, 1).astype(F32)
        large = max_exact + (jnp.log(nf / max_exact) / math.log(MAX_DISTANCE / max_exact)
                             * (N_BUCKETS - max_exact)).astype(jnp.int32)
        large = jnp.minimum(large, N_BUCKETS - 1)
        bucket = jnp.where(n < max_exact, n, large)
        acc = jnp.zeros((MOBA_BLOCK, MOBA_BLOCK), F32)
        for bk in range(N_BUCKETS):
            acc = jnp.where(bucket == bk, tab_ref[bk, h], acc)
        out_ref[0, w] = acc


def _bias_tiles(t5_bias):
    return pl.pallas_call(
        _bias_tile_kernel,
        grid=(HEADS,),
        in_specs=[pl.BlockSpec(memory_space=pltpu.SMEM)],
        out_specs=pl.BlockSpec((1, 2, MOBA_BLOCK, MOBA_BLOCK), lambda h: (h, 0, 0, 0)),
        out_shape=jax.ShapeDtypeStruct((HEADS, 2, MOBA_BLOCK, MOBA_BLOCK), F32),
        name="t5_bias_tiles",
    )(t5_bias)


def _proj_kernel(x_ref, w_ref, wvt_ref, o_ref, f_ref, vt_ref, xb_ref):
    n = pl.program_id(1)

    @pl.when(n == 0)
    def _():
        xb_ref[...] = x_ref[...].astype(BF16)

    acc = jnp.dot(xb_ref[...], w_ref[...], preferred_element_type=F32)
    o_ref[...] = acc.astype(BF16)

    @pl.when(n == 2)
    def _():
        vt = _nt_dot(wvt_ref[...], xb_ref[...]).astype(BF16)
        for c in range(PROJ_TM // MOBA_BLOCK):
            vt_ref[c] = vt[:, c * MOBA_BLOCK:(c + 1) * MOBA_BLOCK]

    @pl.when(n == 4)
    def _():
        f_ref[...] = acc


def _proj(x2, w_in_b, w_vt_b):
    tiles_per_seq = SEQ // PROJ_TM
    blocks_per_tile = PROJ_TM // MOBA_BLOCK
    return pl.pallas_call(
        _proj_kernel,
        grid=(N_TOK // PROJ_TM, IN_WIDTH // WIDTH),
        in_specs=[
            pl.BlockSpec((PROJ_TM, D_MODEL), lambda i, n: (i, 0)),
            pl.BlockSpec((D_MODEL, WIDTH), lambda i, n: (0, n)),
            pl.BlockSpec((WIDTH, D_MODEL), lambda i, n: (0, 0)),
        ],
        out_specs=[
            pl.BlockSpec((PROJ_TM, WIDTH), lambda i, n: (i, n)),
            pl.BlockSpec((PROJ_TM, WIDTH), lambda i, n: (i, 0)),
            pl.BlockSpec((None, blocks_per_tile, WIDTH, MOBA_BLOCK),
                         lambda i, n: (i // tiles_per_seq, i % tiles_per_seq, 0, 0)),
        ],
        out_shape=[
            jax.ShapeDtypeStruct((N_TOK, IN_WIDTH), BF16),
            jax.ShapeDtypeStruct((N_TOK, WIDTH), F32),
            jax.ShapeDtypeStruct((BATCH, N_BLOCKS, WIDTH, MOBA_BLOCK), BF16),
        ],
        scratch_shapes=[pltpu.VMEM((PROJ_TM, D_MODEL), BF16)],
        compiler_params=pltpu.CompilerParams(
            dimension_semantics=("arbitrary", "arbitrary"), vmem_limit_bytes=VMEM_LIMIT),
        name="in_proj",
    )(x2, w_in_b, w_vt_b)


def _attn_kernel(tab_ref, q_ref, k_ref, vt_ref, bias_ref, o_ref,
                 kmh_ref, kml_ref, ext_ref, s_ref):
    assert MOBA_BLOCK == 1 << 8
    pair = pl.program_id(1)
    own = pl.program_id(2)
    lane = lax.broadcasted_iota(jnp.int32, (MOBA_BLOCK, LANES), 1)
    lane_f = lane.astype(F32)

    @pl.when(own == 0)
    def _():
        r = lax.broadcasted_iota(jnp.int32, (LANES, SEQ), 1)
        nb = lax.broadcasted_iota(jnp.int32, (LANES, SEQ), 0)
        avg = jnp.where((r >> 8) == nb, 1.0 / MOBA_BLOCK, 0.0).astype(BF16)
        km = jnp.dot(avg, k_ref[...], preferred_element_type=F32)
        hi = km.astype(BF16)
        kmh_ref[...] = hi
        kml_ref[...] = (km - hi.astype(F32)).astype(BF16)
        kr = lax.broadcasted_iota(jnp.int32, (SEQ, LANES), 0) >> 8
        kl = lax.broadcasted_iota(jnp.int32, (SEQ, LANES), 1)
        ext_ref[...] = jnp.where((kl == kr) | (kl == BIAS_HI_LANE) | (kl == BIAS_LO_LANE),
                                 1.0, 0.0).astype(BF16)

    q2 = q_ref[...]
    krow = lax.broadcasted_iota(jnp.int32, (MOBA_BLOCK, MOBA_BLOCK), 0)
    qcol = lax.broadcasted_iota(jnp.int32, (MOBA_BLOCK, MOBA_BLOCK), 1)
    own_start = pl.multiple_of(own * MOBA_BLOCK, MOBA_BLOCK)
    prev_start = pl.multiple_of(jnp.maximum(own - 1, 0) * MOBA_BLOCK, MOBA_BLOCK)
    k_own = k_ref[pl.ds(own_start, MOBA_BLOCK), :]
    k_prev = k_ref[pl.ds(prev_start, MOBA_BLOCK), :]
    ext_prev = jnp.where(lane == own - 1, 1.0, 0.0).astype(BF16)
    k_prev_aug = jnp.concatenate([k_prev, ext_prev], axis=1)
    ones_rows = jnp.ones((BF16_SUBLANES, MOBA_BLOCK), BF16)

    def v_aug(hh, block):
        vt = vt_ref[block, hh * HEAD_DIM:(hh + 1) * HEAD_DIM, :]
        return jnp.concatenate([vt, ones_rows], axis=0)

    q_far = []
    carry = []
    for hh in range(2):
        head = pair * 2 + hh
        in_head = (lane >= hh * HEAD_DIM) & (lane < (hh + 1) * HEAD_DIM)
        qs = jnp.where(in_head, q2, jnp.zeros_like(q2)) * jnp.asarray(HEAD_DIM ** -0.5, BF16)

        gate = _nt_dot(qs, kmh_ref[...]) + _nt_dot(qs, kml_ref[...])
        eligible = lane < own
        g = jnp.where(eligible, gate, -jnp.inf)
        sel = jnp.zeros((MOBA_BLOCK, LANES), jnp.bool_)
        for _ in range(MOBA_TOPK):
            mx = jnp.max(g, axis=-1, keepdims=True)
            idx = jnp.min(jnp.where(g == mx, lane_f, float(LANES)), axis=-1, keepdims=True)
            hit = lane_f == idx
            sel = sel | hit
            g = jnp.where(hit, -jnp.inf, g)
        sel = sel & eligible

        b31 = jnp.full((MOBA_BLOCK, LANES), tab_ref[N_BUCKETS - 1, head], F32)
        b31_hi = b31.astype(BF16).astype(F32)
        aug = jnp.where(sel, 0.0, NEG)
        aug = jnp.where(lane == BIAS_HI_LANE, b31_hi, aug)
        aug = jnp.where(lane == BIAS_LO_LANE, b31 - b31_hi, aug)
        aug = jnp.where(lane < BIAS_LO_LANE + 1, aug, 0.0)
        q_aug = jnp.concatenate([qs, aug.astype(BF16)], axis=1)
        aug_far = jnp.where(lane == own - 1, NEG, aug)
        q_far.append(jnp.concatenate([qs, aug_far.astype(BF16)], axis=1))

        s_prev = _nt_dot(k_prev_aug, q_aug) + bias_ref[hh, 1]
        s_prev = jnp.where(own >= 1, s_prev, NEG)
        s_own = _nt_dot(k_own, qs) + bias_ref[hh, 0]
        s_own = jnp.where(qcol >= krow, s_own, NEG)
        m = jnp.maximum(jnp.max(s_prev, axis=0, keepdims=True),
                        jnp.max(s_own, axis=0, keepdims=True))
        p = jnp.concatenate([jnp.exp(s_prev - m), jnp.exp(s_own - m)], axis=0).astype(BF16)
        va = jnp.concatenate([v_aug(hh, jnp.maximum(own - 1, 0)), v_aug(hh, own)], axis=1)
        acc = jnp.dot(va, p, preferred_element_type=F32)
        carry += [m, acc]

    q_cat = jnp.concatenate(q_far, axis=0)

    def scores(step, slot):
        start = pl.multiple_of(jnp.minimum(step, N_FAR_STEPS - 1) * FAR_KEYS, FAR_KEYS)
        k_aug = jnp.concatenate([k_ref[pl.ds(start, FAR_KEYS), :],
                                 ext_ref[pl.ds(start, FAR_KEYS), :]], axis=1)
        s_ref[slot] = _nt_dot(k_aug, q_cat)

    def absorb(step, slot, carry):
        block0 = jnp.minimum(step, N_FAR_STEPS - 1) * FAR_BLOCKS
        out = []
        for hh in range(2):
            m, acc = carry[2 * hh], carry[2 * hh + 1]
            s = s_ref[slot, :, hh * MOBA_BLOCK:(hh + 1) * MOBA_BLOCK]
            m_new = jnp.maximum(m, jnp.max(s, axis=0, keepdims=True))
            alpha = jnp.exp(m - m_new)
            p = jnp.exp(s - m_new).astype(BF16)
            va = jnp.concatenate([v_aug(hh, block0 + c) for c in range(FAR_BLOCKS)], axis=1)
            acc = alpha * acc + jnp.dot(va, p, preferred_element_type=F32)
            out += [m_new, acc]
        return tuple(out)

    n_steps = own // FAR_BLOCKS
    scores(0, 0)

    def far_body(i, carry):
        scores(2 * i + 1, 1)
        carry = absorb(2 * i, 0, carry)
        scores(2 * i + 2, 0)
        return absorb(2 * i + 1, 1, carry)

    carry = lax.fori_loop(0, (n_steps + 1) // 2, far_body, tuple(carry))

    outs = []
    for hh in range(2):
        acc = carry[2 * hh + 1]
        outs.append(acc[0:HEAD_DIM] / acc[HEAD_DIM:HEAD_DIM + 1])
    o_ref[...] = jnp.concatenate(outs, axis=0).T.astype(BF16)


def _attention(t5_bias, proj3, vt3, bias_tiles):
    n_pairs = HEADS // 2
    return pl.pallas_call(
        _attn_kernel,
        grid=(BATCH, n_pairs, N_BLOCKS),
        in_specs=[
            pl.BlockSpec(memory_space=pltpu.SMEM),
            pl.BlockSpec((None, MOBA_BLOCK, LANES), lambda b, p, i: (b, i, p)),
            pl.BlockSpec((None, SEQ, LANES), lambda b, p, i: (b, 0, n_pairs + p)),
            pl.BlockSpec((None, N_BLOCKS, LANES, MOBA_BLOCK), lambda b, p, i: (b, 0, p, 0)),
            pl.BlockSpec((2, 2, MOBA_BLOCK, MOBA_BLOCK), lambda b, p, i: (p, 0, 0, 0)),
        ],
        out_specs=pl.BlockSpec((None, MOBA_BLOCK, LANES), lambda b, p, i: (b, i, p)),
        out_shape=jax.ShapeDtypeStruct((BATCH, SEQ, WIDTH), BF16),
        scratch_shapes=[
            pltpu.VMEM((LANES, LANES), BF16),
            pltpu.VMEM((LANES, LANES), BF16),
            pltpu.VMEM((SEQ, LANES), BF16),
            pltpu.VMEM((2, FAR_KEYS, 2 * MOBA_BLOCK), F32),
        ],
        compiler_params=pltpu.CompilerParams(
            dimension_semantics=("arbitrary", "arbitrary", "arbitrary"),
            vmem_limit_bytes=VMEM_LIMIT),
        name="moba_attention",
    )(t5_bias, proj3, proj3, vt3, bias_tiles)


def _hgrn_kernel(q_ref, f_ref, i_ref, g_ref, lbl_ref, nw_ref, o_ref,
                 state_ref, bpad_ref, kpad_ref, vpad_ref, shb_ref, shk_ref, shv_ref):
    C = HGRN_CHUNK
    half = WIDTH // 2

    @pl.when(pl.program_id(1) == 0)
    def _():
        state_ref[...] = jnp.zeros_like(state_ref)

    zpad = jnp.zeros((8, WIDTH), F32)
    bpad_ref[0:8, :] = zpad
    kpad_ref[0:8, :] = zpad
    vpad_ref[0:8, :] = zpad

    l0 = lbl_ref[0:1, :]
    l1 = lbl_ref[1:2, :]
    lmx = jnp.maximum(l0, l1)
    e0 = jnp.exp(l0 - lmx)
    lb = e0 / (e0 + jnp.exp(l1 - lmx))
    nw = nw_ref[...]

    ri = lax.broadcasted_iota(jnp.int32, (C, C), 0)
    ci = lax.broadcasted_iota(jnp.int32, (C, C), 1)
    tril = jnp.where(ri >= ci, 1.0, 0.0).astype(F32)
    assert HEAD_DIM == 1 << 6
    sr = lax.broadcasted_iota(jnp.int32, (half, half), 0) >> 6
    sc = lax.broadcasted_iota(jnp.int32, (half, half), 1) >> 6
    same_head = sr == sc
    head_ones = jnp.where(same_head, 1.0, 0.0).astype(BF16)

    def chunk(c, _):
        r0 = pl.multiple_of(c * C, C)
        qq = q_ref[pl.ds(r0, C), :].astype(F32)
        qq = qq * _sigmoid(qq)
        f = lb + (1.0 - lb) * _sigmoid(f_ref[pl.ds(r0, C), :])
        gl = jnp.log(f)
        kk = 1.0 - f
        vv = i_ref[pl.ds(r0, C), :].astype(F32)
        b = jnp.dot(tril, gl, preferred_element_type=F32,
                    precision=lax.Precision.HIGHEST) * LOG2E

        bpad_ref[8:8 + C, :] = b
        kpad_ref[8:8 + C, :] = kk
        vpad_ref[8:8 + C, :] = vv
        for dd in range(8):
            shb_ref[dd] = bpad_ref[8 - dd:8 - dd + C, :]
            shk_ref[dd] = kpad_ref[8 - dd:8 - dd + C, :]
            shv_ref[dd] = vpad_ref[8 - dd:8 - dd + C, :]

        o_parts = []
        for j in range(C // 8):
            rows = C - 8 * j
            bt = b[8 * j:, :]
            qt = qq[8 * j:, :]
            terms = []
            vs_all = []
            for dd in range(8):
                terms.append(qt * shk_ref[dd, 0:rows, :] * jnp.exp2(bt - shb_ref[dd, 0:rows, :]))
                vs_all.append(shv_ref[dd, 0:rows, :])
            t_all = jnp.concatenate(terms, axis=0).astype(BF16)
            a = jnp.concatenate(
                [jnp.dot(t_all[:, 0:half], head_ones, preferred_element_type=F32),
                 jnp.dot(t_all[:, half:], head_ones, preferred_element_type=F32)], axis=1)
            av = a * jnp.concatenate(vs_all, axis=0)
            part = av[0:rows]
            for dd in range(1, 8):
                part = part + av[dd * rows:(dd + 1) * rows]
            o_parts.append(part)
        o = o_parts[0]
        for j in range(1, C // 8):
            o = o + jnp.concatenate([jnp.zeros((8 * j, WIDTH), F32), o_parts[j]], axis=0)

        qe = (qq * jnp.exp2(b)).astype(BF16)
        b_last = b[C - 1:C, :]
        kd = kk * jnp.exp2(b_last - b)
        dec = jnp.exp2(b_last)
        inter = []
        for a_i in range(2):
            sl = slice(a_i * half, (a_i + 1) * half)
            st = state_ref[a_i]
            inter.append(_nt_dot(qe[:, sl], st.astype(BF16)))
            upd = _tn_dot(vv[:, sl].astype(BF16), kd[:, sl].astype(BF16))
            state_ref[a_i] = st * dec[:, sl] + jnp.where(same_head, upd, 0.0)
        o = o + jnp.concatenate(inter, axis=1)

        oo = (o * o).astype(BF16)
        ms = jnp.concatenate(
            [jnp.dot(oo[:, 0:half], head_ones, preferred_element_type=F32),
             jnp.dot(oo[:, half:], head_ones, preferred_element_type=F32)],
            axis=1) * (1.0 / HEAD_DIM)
        r = o * lax.rsqrt(ms + RMS_EPS) * nw
        r = r * _sigmoid(g_ref[pl.ds(r0, C), :].astype(F32))
        o_ref[pl.ds(r0, C), :] = r.astype(BF16)
        return 0

    lax.fori_loop(0, HGRN_ROWS // C, chunk, 0)


def _hgrn(proj3, f3, lb_logits, norm_w):
    blk = lambda col: pl.BlockSpec((None, HGRN_ROWS, WIDTH), lambda b, i, col=col: (b, i, col))
    return pl.pallas_call(
        _hgrn_kernel,
        grid=(BATCH, SEQ // HGRN_ROWS),
        in_specs=[
            blk(3),
            pl.BlockSpec((None, HGRN_ROWS, WIDTH), lambda b, i: (b, i, 0)),
            blk(5),
            blk(6),
            pl.BlockSpec((2, WIDTH), lambda b, i: (0, 0)),
            pl.BlockSpec((1, WIDTH), lambda b, i: (0, 0)),
        ],
        out_specs=pl.BlockSpec((None, HGRN_ROWS, WIDTH), lambda b, i: (b, i, 0)),
        out_shape=jax.ShapeDtypeStruct((BATCH, SEQ, WIDTH), BF16),
        scratch_shapes=[
            pltpu.VMEM((2, WIDTH // 2, WIDTH // 2), F32),
            pltpu.VMEM((8 + HGRN_CHUNK, WIDTH), F32),
            pltpu.VMEM((8 + HGRN_CHUNK, WIDTH), F32),
            pltpu.VMEM((8 + HGRN_CHUNK, WIDTH), F32),
            pltpu.VMEM((8, HGRN_CHUNK, WIDTH), F32),
            pltpu.VMEM((8, HGRN_CHUNK, WIDTH), F32),
            pltpu.VMEM((8, HGRN_CHUNK, WIDTH), F32),
        ],
        compiler_params=pltpu.CompilerParams(
            dimension_semantics=("arbitrary", "arbitrary"), vmem_limit_bytes=VMEM_LIMIT),
        name="hgrn2",
    )(proj3, f3, proj3, proj3, lb_logits, norm_w)


def _out_kernel(a_ref, r_ref, x_ref, wo_ref, g_ref, b_ref, wr_ref, br_ref,
                x1_ref, x1b_ref, comb_ref):
    mix = jnp.dot(a_ref[...], wo_ref[0:WIDTH, :], preferred_element_type=F32)
    mix = mix + jnp.dot(r_ref[...], wo_ref[WIDTH:, :], preferred_element_type=F32)
    x1 = _layer_norm(ALPHA * x_ref[...] + mix, g_ref[...], b_ref[...])
    x1_ref[...] = x1
    x1b_ref[...] = x1.astype(BF16)

    logits = jnp.dot(x1, wr_ref[...], preferred_element_type=F32,
                     precision=lax.Precision.HIGHEST) + br_ref[...]
    assert EXPERTS_PER_GROUP == 1 << 2
    lane_i = lax.broadcasted_iota(jnp.int32, logits.shape, 1)
    lane = lane_i.astype(F32)
    grp_of_lane = (lane_i >> 2).astype(F32)
    none = float(LANES)
    is_g = (lane_i >= N_EXPERTS) & (lane_i < N_EXPERTS + N_GROUPS)
    gl = jnp.where(is_g, logits, -jnp.inf)
    ge = jnp.exp(gl - jnp.max(gl, axis=-1, keepdims=True))
    gp = ge / jnp.sum(ge, axis=-1, keepdims=True)
    g_w = jnp.max(gp, axis=-1, keepdims=True)
    g_lane = jnp.min(jnp.where(is_g & (gp == g_w), lane, none), axis=-1, keepdims=True)
    g_idx = g_lane - float(N_EXPERTS)

    in_grp = (lane_i < N_EXPERTS) & (grp_of_lane == g_idx)
    el = jnp.where(in_grp, logits, -jnp.inf)
    ee = jnp.exp(el - jnp.max(el, axis=-1, keepdims=True))
    ep = ee / jnp.sum(ee, axis=-1, keepdims=True)
    p1 = jnp.max(ep, axis=-1, keepdims=True)
    i1 = jnp.min(jnp.where(in_grp & (ep == p1), lane, none), axis=-1, keepdims=True)
    rest = in_grp & (lane != i1)
    ep2 = jnp.where(rest, ep, -1.0)
    p2 = jnp.max(ep2, axis=-1, keepdims=True)
    i2 = jnp.min(jnp.where(rest & (ep2 == p2), lane, none), axis=-1, keepdims=True)
    den = p1 + p2
    comb = jnp.where(lane == i1, g_w * (p1 / den), 0.0)
    comb = jnp.where(lane == i2, g_w * (p2 / den), comb)
    comb_ref[...] = comb


def _out_proj(a2, r2, x2, wo_b, g1, b1, w_route, b_route):
    row = lambda w: pl.BlockSpec((OUT_TM, w), lambda i: (i, 0))
    full = lambda s: pl.BlockSpec(s, lambda i: (0, 0))
    return pl.pallas_call(
        _out_kernel,
        grid=(N_TOK // OUT_TM,),
        in_specs=[row(WIDTH), row(WIDTH), row(D_MODEL), full((D_MODEL, D_MODEL)),
                  full((1, D_MODEL)), full((1, D_MODEL)), full((D_MODEL, LANES)), full((1, LANES))],
        out_specs=[row(D_MODEL), row(D_MODEL), row(LANES)],
        out_shape=[jax.ShapeDtypeStruct((N_TOK, D_MODEL), F32),
                   jax.ShapeDtypeStruct((N_TOK, D_MODEL), BF16),
                   jax.ShapeDtypeStruct((N_TOK, LANES), F32)],
        compiler_params=pltpu.CompilerParams(
            dimension_semantics=("arbitrary",), vmem_limit_bytes=VMEM_LIMIT),
        name="out_proj_ln_route",
    )(a2, r2, x2, wo_b, g1, b1, w_route, b_route)


def _moe_kernel(xb_ref, x1_ref, comb_ref, wg_ref, wu_ref, wd_ref, g_ref, b_ref, o_ref, acc_ref):
    e = pl.program_id(1)
    xb = xb_ref[...]
    hg = jnp.dot(xb, wg_ref[...], preferred_element_type=F32)
    hu = jnp.dot(xb, wu_ref[...], preferred_element_type=F32)
    h = (hg * _sigmoid(hg) * hu).astype(BF16)
    y = jnp.dot(h, wd_ref[...], preferred_element_type=F32)
    comb = comb_ref[...]
    lane = lax.broadcasted_iota(jnp.int32, comb.shape, 1)
    c = jnp.sum(jnp.where(lane == e, comb, 0.0), axis=-1, keepdims=True)

    @pl.when(e == 0)
    def _():
        acc_ref[...] = c * y

    @pl.when(e > 0)
    def _():
        acc_ref[...] += c * y

    @pl.when(e == N_EXPERTS - 1)
    def _():
        o_ref[...] = _layer_norm(ALPHA * x1_ref[...] + acc_ref[...], g_ref[...], b_ref[...])


def _moe(x1b, x1, comb, wg_b, wu_b, wd_b, g2, b2):
    return pl.pallas_call(
        _moe_kernel,
        grid=(N_TOK // MOE_TM, N_EXPERTS),
        in_specs=[
            pl.BlockSpec((MOE_TM, D_MODEL), lambda i, e: (i, 0)),
            pl.BlockSpec((MOE_TM, D_MODEL), lambda i, e: (i, 0)),
            pl.BlockSpec((MOE_TM, LANES), lambda i, e: (i, 0)),
            pl.BlockSpec((None, D_MODEL, D_EXPERT), lambda i, e: (e, 0, 0)),
            pl.BlockSpec((None, D_MODEL, D_EXPERT), lambda i, e: (e, 0, 0)),
            pl.BlockSpec((None, D_EXPERT, D_MODEL), lambda i, e: (e, 0, 0)),
            pl.BlockSpec((1, D_MODEL), lambda i, e: (0, 0)),
            pl.BlockSpec((1, D_MODEL), lambda i, e: (0, 0)),
        ],
        out_specs=pl.BlockSpec((MOE_TM, D_MODEL), lambda i, e: (i, 0)),
        out_shape=jax.ShapeDtypeStruct((N_TOK, D_MODEL), F32),
        scratch_shapes=[pltpu.VMEM((MOE_TM, D_MODEL), F32)],
        compiler_params=pltpu.CompilerParams(
            dimension_semantics=("arbitrary", "arbitrary"), vmem_limit_bytes=VMEM_LIMIT),
        name="moe_experts_ln",
    )(x1b, x1, comb, wg_b, wu_b, wd_b, g2, b2)


def kernel(x, w_in, t5_bias, hgrn_lb_logits, hgrn_norm_w, w_o, ln1_g, ln1_b, w_group, b_group,
           w_expert, b_expert, w_gate, w_up, w_down, ln2_g, ln2_b):
    x2 = x.reshape(N_TOK, D_MODEL)
    w_in_b = w_in[0].astype(BF16)
    w_vt_b = w_in_b[:, 2 * WIDTH:3 * WIDTH].T
    proj, f_logits, vt3 = _proj(x2, w_in_b, w_vt_b)
    proj3 = proj.reshape(BATCH, SEQ, IN_WIDTH)

    a = _attention(t5_bias, proj3, vt3, _bias_tiles(t5_bias))
    r = _hgrn(proj3, f_logits.reshape(BATCH, SEQ, WIDTH), hgrn_lb_logits, hgrn_norm_w)

    pad = LANES - N_EXPERTS - N_GROUPS
    w_route = jnp.concatenate(
        [w_expert[0].transpose(1, 0, 2).reshape(D_MODEL, N_EXPERTS), w_group[0],
         jnp.zeros((D_MODEL, pad), F32)], axis=1)
    b_route = jnp.concatenate(
        [b_expert[0].reshape(N_EXPERTS), b_group[0], jnp.zeros((pad,), F32)]).reshape(1, LANES)
    x1, x1b, comb = _out_proj(
        a.reshape(N_TOK, WIDTH), r.reshape(N_TOK, WIDTH), x2, w_o[0].astype(BF16),
        ln1_g, ln1_b, w_route, b_route)

    out = _moe(x1b, x1, comb, w_gate[0].astype(BF16), w_up[0].astype(BF16),
               w_down[0].astype(BF16), ln2_g, ln2_b)
    return out.reshape(BATCH, SEQ, D_MODEL)
```

```python
import functools
import math

import jax
import jax.numpy as jnp
from jax import lax
from jax.experimental import pallas as pl
from jax.experimental.pallas import tpu as pltpu

F32 = jnp.float32
BF16 = jnp.bfloat16

D_MODEL = 1024
BATCH = 2
SEQ = 8192
N_TOK = BATCH * SEQ
HEADS = 8
HEAD_DIM = 64
WIDTH = HEADS * HEAD_DIM
IN_WIDTH = 7 * WIDTH
MOBA_BLOCK = 256
N_BLOCKS = SEQ // MOBA_BLOCK
MOBA_TOPK = 3
HGRN_CHUNK = 32
N_BUCKETS = 32
MAX_DISTANCE = 128
N_GROUPS = 4
EXPERTS_PER_GROUP = 4
N_EXPERTS = 16
D_EXPERT = 512
LN_EPS = 1e-5
RMS_EPS = 1e-6
ALPHA = 2.0 ** 0.25
LOG2E = math.log2(math.e)

LANES = 128
BF16_SUBLANES = 16
NEG = -1e30
VMEM_LIMIT = 48 * 1024 * 1024

PROJ_TM = 1024
OUT_TM = 512
MOE_TM = 1024
MOE_ALIGN = 128
MOE_WIN = 256
MOE_ROWS = MOE_TM + N_GROUPS * MOE_ALIGN + (MOE_WIN - MOE_ALIGN)
GROUP_LANE = N_EXPERTS
HGRN_ROWS = 256
FAR_BLOCKS = 2
FAR_KEYS = FAR_BLOCKS * MOBA_BLOCK
N_FAR_STEPS = N_BLOCKS // FAR_BLOCKS

BIAS_HI_LANE = N_BLOCKS
BIAS_LO_LANE = N_BLOCKS + 1


def _nt_dot(a, b):
    return lax.dot_general(a, b, (((1,), (1,)), ((), ())), preferred_element_type=F32)


def _tn_dot(a, b):
    return lax.dot_general(a, b, (((0,), (0,)), ((), ())), preferred_element_type=F32)


def _sigmoid(x):
    return 1.0 / (1.0 + jnp.exp2(x * (-LOG2E)))


def _layer_norm(y, g, b):
    mu = jnp.mean(y, axis=-1, keepdims=True)
    d = y - mu
    var = jnp.mean(d * d, axis=-1, keepdims=True)
    return d * lax.rsqrt(var + LN_EPS) * g + b


def _bias_tile_kernel(tab_ref, out_ref):
    h = pl.program_id(0)
    ki = lax.broadcasted_iota(jnp.int32, (MOBA_BLOCK, MOBA_BLOCK), 0)
    qi = lax.broadcasted_iota(jnp.int32, (MOBA_BLOCK, MOBA_BLOCK), 1)
    max_exact = N_BUCKETS // 2
    for w in range(2):
        n = jnp.maximum(qi - ki + MOBA_BLOCK * w, 0)
        nf = jnp.maximum(n, 1).astype(F32)
        large = max_exact + (jnp.log(nf / max_exact) / math.log(MAX_DISTANCE / max_exact)
                             * (N_BUCKETS - max_exact)).astype(jnp.int32)
        large = jnp.minimum(large, N_BUCKETS - 1)
        bucket = jnp.where(n < max_exact, n, large)
        acc = jnp.zeros((MOBA_BLOCK, MOBA_BLOCK), F32)
        for bk in range(N_BUCKETS):
            acc = jnp.where(bucket == bk, tab_ref[bk, h], acc)
        out_ref[0, w] = acc


def _bias_tiles(t5_bias):
    return pl.pallas_call(
        _bias_tile_kernel,
        grid=(HEADS,),
        in_specs=[pl.BlockSpec(memory_space=pltpu.SMEM)],
        out_specs=pl.BlockSpec((1, 2, MOBA_BLOCK, MOBA_BLOCK), lambda h: (h, 0, 0, 0)),
        out_shape=jax.ShapeDtypeStruct((HEADS, 2, MOBA_BLOCK, MOBA_BLOCK), F32),
        name="t5_bias_tiles",
    )(t5_bias)


def _proj_kernel(x_ref, w_ref, wvt_ref, o_ref, f_ref, vt_ref, xb_ref):
    n = pl.program_id(1)

    @pl.when(n == 0)
    def _():
        xb_ref[...] = x_ref[...].astype(BF16)

    acc = jnp.dot(xb_ref[...], w_ref[...], preferred_element_type=F32)
    o_ref[...] = acc.astype(BF16)

    @pl.when(n == 2)
    def _():
        vt = _nt_dot(wvt_ref[...], xb_ref[...]).astype(BF16)
        for c in range(PROJ_TM // MOBA_BLOCK):
            vt_ref[c] = vt[:, c * MOBA_BLOCK:(c + 1) * MOBA_BLOCK]

    @pl.when(n == 4)
    def _():
        f_ref[...] = acc


def _proj(x2, w_in_b, w_vt_b):
    tiles_per_seq = SEQ // PROJ_TM
    blocks_per_tile = PROJ_TM // MOBA_BLOCK
    return pl.pallas_call(
        _proj_kernel,
        grid=(N_TOK // PROJ_TM, IN_WIDTH // WIDTH),
        in_specs=[
            pl.BlockSpec((PROJ_TM, D_MODEL), lambda i, n: (i, 0)),
            pl.BlockSpec((D_MODEL, WIDTH), lambda i, n: (0, n)),
            pl.BlockSpec((WIDTH, D_MODEL), lambda i, n: (0, 0)),
        ],
        out_specs=[
            pl.BlockSpec((PROJ_TM, WIDTH), lambda i, n: (i, n)),
            pl.BlockSpec((PROJ_TM, WIDTH), lambda i, n: (i, 0)),
            pl.BlockSpec((None, blocks_per_tile, WIDTH, MOBA_BLOCK),
                         lambda i, n: (i // tiles_per_seq, i % tiles_per_seq, 0, 0)),
        ],
        out_shape=[
            jax.ShapeDtypeStruct((N_TOK, IN_WIDTH), BF16),
            jax.ShapeDtypeStruct((N_TOK, WIDTH), F32),
            jax.ShapeDtypeStruct((BATCH, N_BLOCKS, WIDTH, MOBA_BLOCK), BF16),
        ],
        scratch_shapes=[pltpu.VMEM((PROJ_TM, D_MODEL), BF16)],
        compiler_params=pltpu.CompilerParams(
            dimension_semantics=("arbitrary", "arbitrary"), vmem_limit_bytes=VMEM_LIMIT),
        name="in_proj",
    )(x2, w_in_b, w_vt_b)


def _attn_kernel(tab_ref, q_ref, k_ref, vt_ref, bias_ref, o_ref,
                 kmh_ref, kml_ref, ext_ref, s_ref):
    assert MOBA_BLOCK == 1 << 8
    pair = pl.program_id(1)
    own = pl.program_id(2)
    lane = lax.broadcasted_iota(jnp.int32, (MOBA_BLOCK, LANES), 1)
    lane_f = lane.astype(F32)

    @pl.when(own == 0)
    def _():
        r = lax.broadcasted_iota(jnp.int32, (LANES, SEQ), 1)
        nb = lax.broadcasted_iota(jnp.int32, (LANES, SEQ), 0)
        avg = jnp.where((r >> 8) == nb, 1.0 / MOBA_BLOCK, 0.0).astype(BF16)
        km = jnp.dot(avg, k_ref[...], preferred_element_type=F32)
        hi = km.astype(BF16)
        kmh_ref[...] = hi
        kml_ref[...] = (km - hi.astype(F32)).astype(BF16)
        kr = lax.broadcasted_iota(jnp.int32, (SEQ, LANES), 0) >> 8
        kl = lax.broadcasted_iota(jnp.int32, (SEQ, LANES), 1)
        ext_ref[...] = jnp.where((kl == kr) | (kl == BIAS_HI_LANE) | (kl == BIAS_LO_LANE),
                                 1.0, 0.0).astype(BF16)

    q2 = q_ref[...]
    krow = lax.broadcasted_iota(jnp.int32, (MOBA_BLOCK, MOBA_BLOCK), 0)
    qcol = lax.broadcasted_iota(jnp.int32, (MOBA_BLOCK, MOBA_BLOCK), 1)
    own_start = pl.multiple_of(own * MOBA_BLOCK, MOBA_BLOCK)
    prev_start = pl.multiple_of(jnp.maximum(own - 1, 0) * MOBA_BLOCK, MOBA_BLOCK)
    k_own = k_ref[pl.ds(own_start, MOBA_BLOCK), :]
    k_prev = k_ref[pl.ds(prev_start, MOBA_BLOCK), :]
    ext_prev = jnp.where(lane == own - 1, 1.0, 0.0).astype(BF16)
    k_prev_aug = jnp.concatenate([k_prev, ext_prev], axis=1)
    ones_rows = jnp.ones((BF16_SUBLANES, MOBA_BLOCK), BF16)

    def v_aug(hh, block):
        vt = vt_ref[block, hh * HEAD_DIM:(hh + 1) * HEAD_DIM, :]
        return jnp.concatenate([vt, ones_rows], axis=0)

    q_far = []
    carry = []
    for hh in range(2):
        head = pair * 2 + hh
        in_head = (lane >= hh * HEAD_DIM) & (lane < (hh + 1) * HEAD_DIM)
        qs = jnp.where(in_head, q2, jnp.zeros_like(q2)) * jnp.asarray(HEAD_DIM ** -0.5, BF16)

        gate = _nt_dot(qs, kmh_ref[...]) + _nt_dot(qs, kml_ref[...])
        eligible = lane < own
        g = jnp.where(eligible, gate, -jnp.inf)
        sel = jnp.zeros((MOBA_BLOCK, LANES), jnp.bool_)
        for _ in range(MOBA_TOPK):
            mx = jnp.max(g, axis=-1, keepdims=True)
            idx = jnp.min(jnp.where(g == mx, lane_f, float(LANES)), axis=-1, keepdims=True)
            hit = lane_f == idx
            sel = sel | hit
            g = jnp.where(hit, -jnp.inf, g)
        sel = sel & eligible

        b31 = jnp.full((MOBA_BLOCK, LANES), tab_ref[N_BUCKETS - 1, head], F32)
        b31_hi = b31.astype(BF16).astype(F32)
        aug = jnp.where(sel, 0.0, NEG)
        aug = jnp.where(lane == BIAS_HI_LANE, b31_hi, aug)
        aug = jnp.where(lane == BIAS_LO_LANE, b31 - b31_hi, aug)
        aug = jnp.where(lane < BIAS_LO_LANE + 1, aug, 0.0)
        q_aug = jnp.concatenate([qs, aug.astype(BF16)], axis=1)
        aug_far = jnp.where(lane == own - 1, NEG, aug)
        q_far.append(jnp.concatenate([qs, aug_far.astype(BF16)], axis=1))

        s_prev = _nt_dot(k_prev_aug, q_aug) + bias_ref[hh, 1]
        s_prev = jnp.where(own >= 1, s_prev, NEG)
        s_own = _nt_dot(k_own, qs) + bias_ref[hh, 0]
        s_own = jnp.where(qcol >= krow, s_own, NEG)
        m = jnp.maximum(jnp.max(s_prev, axis=0, keepdims=True),
                        jnp.max(s_own, axis=0, keepdims=True))
        p = jnp.concatenate([jnp.exp(s_prev - m), jnp.exp(s_own - m)], axis=0).astype(BF16)
        va = jnp.concatenate([v_aug(hh, jnp.maximum(own - 1, 0)), v_aug(hh, own)], axis=1)
        acc = jnp.dot(va, p, preferred_element_type=F32)
        carry += [m, acc]

    q_cat = jnp.concatenate(q_far, axis=0)

    def scores(step, slot):
        start = pl.multiple_of(jnp.minimum(step, N_FAR_STEPS - 1) * FAR_KEYS, FAR_KEYS)
        k_aug = jnp.concatenate([k_ref[pl.ds(start, FAR_KEYS), :],
                                 ext_ref[pl.ds(start, FAR_KEYS), :]], axis=1)
        s_ref[slot] = _nt_dot(k_aug, q_cat)

    def absorb(step, slot, carry):
        block0 = jnp.minimum(step, N_FAR_STEPS - 1) * FAR_BLOCKS
        out = []
        for hh in range(2):
            m, acc = carry[2 * hh], carry[2 * hh + 1]
            s = s_ref[slot, :, hh * MOBA_BLOCK:(hh + 1) * MOBA_BLOCK]
            m_new = jnp.maximum(m, jnp.max(s, axis=0, keepdims=True))
            alpha = jnp.exp(m - m_new)
            p = jnp.exp(s - m_new).astype(BF16)
            va = jnp.concatenate([v_aug(hh, block0 + c) for c in range(FAR_BLOCKS)], axis=1)
            acc = alpha * acc + jnp.dot(va, p, preferred_element_type=F32)
            out += [m_new, acc]
        return tuple(out)

    n_steps = own // FAR_BLOCKS
    scores(0, 0)

    def far_body(i, carry):
        scores(2 * i + 1, 1)
        carry = absorb(2 * i, 0, carry)
        scores(2 * i + 2, 0)
        return absorb(2 * i + 1, 1, carry)

    carry = lax.fori_loop(0, (n_steps + 1) // 2, far_body, tuple(carry))

    outs = []
    for hh in range(2):
        acc = carry[2 * hh + 1]
        outs.append(acc[0:HEAD_DIM] / acc[HEAD_DIM:HEAD_DIM + 1])
    o_ref[...] = jnp.concatenate(outs, axis=0).T.astype(BF16)


def _attention(t5_bias, proj3, vt3, bias_tiles):
    n_pairs = HEADS // 2
    return pl.pallas_call(
        _attn_kernel,
        grid=(BATCH, n_pairs, N_BLOCKS),
        in_specs=[
            pl.BlockSpec(memory_space=pltpu.SMEM),
            pl.BlockSpec((None, MOBA_BLOCK, LANES), lambda b, p, i: (b, i, p)),
            pl.BlockSpec((None, SEQ, LANES), lambda b, p, i: (b, 0, n_pairs + p)),
            pl.BlockSpec((None, N_BLOCKS, LANES, MOBA_BLOCK), lambda b, p, i: (b, 0, p, 0)),
            pl.BlockSpec((2, 2, MOBA_BLOCK, MOBA_BLOCK), lambda b, p, i: (p, 0, 0, 0)),
        ],
        out_specs=pl.BlockSpec((None, MOBA_BLOCK, LANES), lambda b, p, i: (b, i, p)),
        out_shape=jax.ShapeDtypeStruct((BATCH, SEQ, WIDTH), BF16),
        scratch_shapes=[
            pltpu.VMEM((LANES, LANES), BF16),
            pltpu.VMEM((LANES, LANES), BF16),
            pltpu.VMEM((SEQ, LANES), BF16),
            pltpu.VMEM((2, FAR_KEYS, 2 * MOBA_BLOCK), F32),
        ],
        compiler_params=pltpu.CompilerParams(
            dimension_semantics=("arbitrary", "arbitrary", "arbitrary"),
            vmem_limit_bytes=VMEM_LIMIT),
        name="moba_attention",
    )(t5_bias, proj3, proj3, vt3, bias_tiles)


def _hgrn_kernel(q_ref, f_ref, i_ref, g_ref, lbl_ref, nw_ref, o_ref,
                 state_ref, bpad_ref, kpad_ref, vpad_ref, shb_ref, shk_ref, shv_ref):
    C = HGRN_CHUNK
    half = WIDTH // 2

    @pl.when(pl.program_id(1) == 0)
    def _():
        state_ref[...] = jnp.zeros_like(state_ref)

    zpad = jnp.zeros((8, WIDTH), F32)
    bpad_ref[0:8, :] = zpad
    kpad_ref[0:8, :] = zpad
    vpad_ref[0:8, :] = zpad

    l0 = lbl_ref[0:1, :]
    l1 = lbl_ref[1:2, :]
    lmx = jnp.maximum(l0, l1)
    e0 = jnp.exp(l0 - lmx)
    lb = e0 / (e0 + jnp.exp(l1 - lmx))
    nw = nw_ref[...]

    ri = lax.broadcasted_iota(jnp.int32, (C, C), 0)
    ci = lax.broadcasted_iota(jnp.int32, (C, C), 1)
    tril = jnp.where(ri >= ci, 1.0, 0.0).astype(F32)
    assert HEAD_DIM == 1 << 6
    sr = lax.broadcasted_iota(jnp.int32, (half, half), 0) >> 6
    sc = lax.broadcasted_iota(jnp.int32, (half, half), 1) >> 6
    same_head = sr == sc
    head_ones = jnp.where(same_head, 1.0, 0.0).astype(BF16)

    def chunk(c, _):
        r0 = pl.multiple_of(c * C, C)
        qq = q_ref[pl.ds(r0, C), :].astype(F32)
        qq = qq * _sigmoid(qq)
        f = lb + (1.0 - lb) * _sigmoid(f_ref[pl.ds(r0, C), :])
        gl = jnp.log(f)
        kk = 1.0 - f
        vv = i_ref[pl.ds(r0, C), :].astype(F32)
        b = jnp.dot(tril, gl, preferred_element_type=F32,
                    precision=lax.Precision.HIGHEST) * LOG2E

        bpad_ref[8:8 + C, :] = b
        kpad_ref[8:8 + C, :] = kk
        vpad_ref[8:8 + C, :] = vv
        for dd in range(8):
            shb_ref[dd] = bpad_ref[8 - dd:8 - dd + C, :]
            shk_ref[dd] = kpad_ref[8 - dd:8 - dd + C, :]
            shv_ref[dd] = vpad_ref[8 - dd:8 - dd + C, :]

        o_parts = []
        for j in range(C // 8):
            rows = C - 8 * j
            bt = b[8 * j:, :]
            qt = qq[8 * j:, :]
            terms = []
            vs_all = []
            for dd in range(8):
                terms.append(qt * shk_ref[dd, 0:rows, :] * jnp.exp2(bt - shb_ref[dd, 0:rows, :]))
                vs_all.append(shv_ref[dd, 0:rows, :])
            t_all = jnp.concatenate(terms, axis=0).astype(BF16)
            a = jnp.concatenate(
                [jnp.dot(t_all[:, 0:half], head_ones, preferred_element_type=F32),
                 jnp.dot(t_all[:, half:], head_ones, preferred_element_type=F32)], axis=1)
            av = a * jnp.concatenate(vs_all, axis=0)
            part = av[0:rows]
            for dd in range(1, 8):
                part = part + av[dd * rows:(dd + 1) * rows]
            o_parts.append(part)
        o = o_parts[0]
        for j in range(1, C // 8):
            o = o + jnp.concatenate([jnp.zeros((8 * j, WIDTH), F32), o_parts[j]], axis=0)

        qe = (qq * jnp.exp2(b)).astype(BF16)
        b_last = b[C - 1:C, :]
        kd = kk * jnp.exp2(b_last - b)
        dec = jnp.exp2(b_last)
        inter = []
        for a_i in range(2):
            sl = slice(a_i * half, (a_i + 1) * half)
            st = state_ref[a_i]
            inter.append(_nt_dot(qe[:, sl], st.astype(BF16)))
            upd = _tn_dot(vv[:, sl].astype(BF16), kd[:, sl].astype(BF16))
            state_ref[a_i] = st * dec[:, sl] + jnp.where(same_head, upd, 0.0)
        o = o + jnp.concatenate(inter, axis=1)

        oo = (o * o).astype(BF16)
        ms = jnp.concatenate(
            [jnp.dot(oo[:, 0:half], head_ones, preferred_element_type=F32),
             jnp.dot(oo[:, half:], head_ones, preferred_element_type=F32)],
            axis=1) * (1.0 / HEAD_DIM)
        r = o * lax.rsqrt(ms + RMS_EPS) * nw
        r = r * _sigmoid(g_ref[pl.ds(r0, C), :].astype(F32))
        o_ref[pl.ds(r0, C), :] = r.astype(BF16)
        return 0

    lax.fori_loop(0, HGRN_ROWS // C, chunk, 0)


def _hgrn(proj3, f3, lb_logits, norm_w):
    blk = lambda col: pl.BlockSpec((None, HGRN_ROWS, WIDTH), lambda b, i, col=col: (b, i, col))
    return pl.pallas_call(
        _hgrn_kernel,
        grid=(BATCH, SEQ // HGRN_ROWS),
        in_specs=[
            blk(3),
            pl.BlockSpec((None, HGRN_ROWS, WIDTH), lambda b, i: (b, i, 0)),
            blk(5),
            blk(6),
            pl.BlockSpec((2, WIDTH), lambda b, i: (0, 0)),
            pl.BlockSpec((1, WIDTH), lambda b, i: (0, 0)),
        ],
        out_specs=pl.BlockSpec((None, HGRN_ROWS, WIDTH), lambda b, i: (b, i, 0)),
        out_shape=jax.ShapeDtypeStruct((BATCH, SEQ, WIDTH), BF16),
        scratch_shapes=[
            pltpu.VMEM((2, WIDTH // 2, WIDTH // 2), F32),
            pltpu.VMEM((8 + HGRN_CHUNK, WIDTH), F32),
            pltpu.VMEM((8 + HGRN_CHUNK, WIDTH), F32),
            pltpu.VMEM((8 + HGRN_CHUNK, WIDTH), F32),
            pltpu.VMEM((8, HGRN_CHUNK, WIDTH), F32),
            pltpu.VMEM((8, HGRN_CHUNK, WIDTH), F32),
            pltpu.VMEM((8, HGRN_CHUNK, WIDTH), F32),
        ],
        compiler_params=pltpu.CompilerParams(
            dimension_semantics=("arbitrary", "arbitrary"), vmem_limit_bytes=VMEM_LIMIT),
        name="hgrn2",
    )(proj3, f3, proj3, proj3, lb_logits, norm_w)


def _out_kernel(a_ref, r_ref, x_ref, wo_ref, g_ref, b_ref, wr_ref, br_ref,
                x1_ref, comb_ref):
    mix = jnp.dot(a_ref[...], wo_ref[0:WIDTH, :], preferred_element_type=F32)
    mix = mix + jnp.dot(r_ref[...], wo_ref[WIDTH:, :], preferred_element_type=F32)
    x1 = _layer_norm(ALPHA * x_ref[...] + mix, g_ref[...], b_ref[...])
    x1_ref[...] = x1

    logits = jnp.dot(x1, wr_ref[...], preferred_element_type=F32,
                     precision=lax.Precision.HIGHEST) + br_ref[...]
    assert EXPERTS_PER_GROUP == 1 << 2
    lane_i = lax.broadcasted_iota(jnp.int32, logits.shape, 1)
    lane = lane_i.astype(F32)
    grp_of_lane = (lane_i >> 2).astype(F32)
    none = float(LANES)
    is_g = (lane_i >= N_EXPERTS) & (lane_i < N_EXPERTS + N_GROUPS)
    gl = jnp.where(is_g, logits, -jnp.inf)
    ge = jnp.exp(gl - jnp.max(gl, axis=-1, keepdims=True))
    gp = ge / jnp.sum(ge, axis=-1, keepdims=True)
    g_w = jnp.max(gp, axis=-1, keepdims=True)
    g_lane = jnp.min(jnp.where(is_g & (gp == g_w), lane, none), axis=-1, keepdims=True)
    g_idx = g_lane - float(N_EXPERTS)

    in_grp = (lane_i < N_EXPERTS) & (grp_of_lane == g_idx)
    el = jnp.where(in_grp, logits, -jnp.inf)
    ee = jnp.exp(el - jnp.max(el, axis=-1, keepdims=True))
    ep = ee / jnp.sum(ee, axis=-1, keepdims=True)
    p1 = jnp.max(ep, axis=-1, keepdims=True)
    i1 = jnp.min(jnp.where(in_grp & (ep == p1), lane, none), axis=-1, keepdims=True)
    rest = in_grp & (lane != i1)
    ep2 = jnp.where(rest, ep, -1.0)
    p2 = jnp.max(ep2, axis=-1, keepdims=True)
    i2 = jnp.min(jnp.where(rest & (ep2 == p2), lane, none), axis=-1, keepdims=True)
    den = p1 + p2
    comb = jnp.where(lane == i1, g_w * (p1 / den), 0.0)
    comb = jnp.where(lane == i2, g_w * (p2 / den), comb)
    comb_ref[...] = jnp.where(lane_i == GROUP_LANE, g_idx, comb)


def _out_proj(a2, r2, x2, wo_b, g1, b1, w_route, b_route):
    row = lambda w: pl.BlockSpec((OUT_TM, w), lambda i: (i, 0))
    full = lambda s: pl.BlockSpec(s, lambda i: (0, 0))
    return pl.pallas_call(
        _out_kernel,
        grid=(N_TOK // OUT_TM,),
        in_specs=[row(WIDTH), row(WIDTH), row(D_MODEL), full((D_MODEL, D_MODEL)),
                  full((1, D_MODEL)), full((1, D_MODEL)), full((D_MODEL, LANES)), full((1, LANES))],
        out_specs=[row(D_MODEL), row(LANES)],
        out_shape=[jax.ShapeDtypeStruct((N_TOK, D_MODEL), F32),
                   jax.ShapeDtypeStruct((N_TOK, LANES), F32)],
        compiler_params=pltpu.CompilerParams(
            dimension_semantics=("arbitrary",), vmem_limit_bytes=VMEM_LIMIT),
        name="out_proj_ln_route",
    )(a2, r2, x2, wo_b, g1, b1, w_route, b_route)


def _moe_kernel(x1_ref, comb_ref, wg_ref, wu_ref, wd_ref, g_ref, b_ref, o_ref,
                pt_ref, xs_ref, cs_ref, acc_ref, win_ref):
    e = pl.program_id(1)
    assert MOE_ALIGN == 1 << 7 and EXPERTS_PER_GROUP == 1 << 2

    @pl.when(e == 0)
    def _():
        comb = comb_ref[...]
        lane = lax.broadcasted_iota(jnp.int32, comb.shape, 1)
        lane_f = lane.astype(F32)
        gid = jnp.sum(jnp.where(lane == GROUP_LANE, comb, 0.0), axis=-1, keepdims=True)
        mine = lane_f == gid
        ti = lax.broadcasted_iota(jnp.int32, (MOE_TM, MOE_TM), 0)
        tj = lax.broadcasted_iota(jnp.int32, (MOE_TM, MOE_TM), 1)
        tril = jnp.where(ti >= tj, 1.0, 0.0).astype(BF16)
        incl = jnp.dot(tril, jnp.where(mine, 1.0, 0.0).astype(BF16),
                       preferred_element_type=F32)
        rank = jnp.sum(jnp.where(mine, incl, 0.0), axis=-1, keepdims=True) - 1.0
        counts = incl[MOE_TM - 1:MOE_TM, :]
        lane_row = lane[0:1, :]
        start = jnp.int32(0)
        base = jnp.zeros_like(gid)
        for g in range(N_GROUPS):
            cnt = jnp.sum(jnp.where(lane_row == g, counts, 0.0)).astype(jnp.int32)
            win_ref[0, g] = start
            win_ref[1, g] = (cnt + (MOE_WIN - 1)) // MOE_WIN
            base = jnp.where(gid == float(g), start.astype(F32), base)
            start = start + ((cnt + (MOE_ALIGN - 1)) >> 7) * MOE_ALIGN
        pos = base + rank
        r_iota = lax.broadcasted_iota(jnp.int32, (MOE_TM, MOE_ROWS), 1).astype(F32)
        pt = jnp.where(r_iota == pos, 1.0, 0.0).astype(BF16)
        pt_ref[...] = pt
        xs_ref[...] = _tn_dot(pt, x1_ref[...].astype(BF16)).astype(BF16)
        c_hi = comb.astype(BF16)
        c_lo = (comb - c_hi.astype(F32)).astype(BF16)
        cs_ref[...] = _tn_dot(pt, c_hi) + _tn_dot(pt, c_lo)
        acc_ref[...] = jnp.zeros_like(acc_ref)

    g = e >> 2
    row0 = win_ref[0, g]
    lane_w = lax.broadcasted_iota(jnp.int32, (MOE_WIN, LANES), 1)

    def window(k, _):
        r0 = pl.multiple_of(row0 + k * MOE_WIN, MOE_ALIGN)
        xw = xs_ref[pl.ds(r0, MOE_WIN), :]
        hg = jnp.dot(xw, wg_ref[...], preferred_element_type=F32)
        hu = jnp.dot(xw, wu_ref[...], preferred_element_type=F32)
        h = (hg * _sigmoid(hg) * hu).astype(BF16)
        y = jnp.dot(h, wd_ref[...], preferred_element_type=F32)
        c = jnp.sum(jnp.where(lane_w == e, cs_ref[pl.ds(r0, MOE_WIN), :], 0.0),
                    axis=-1, keepdims=True)
        acc_ref[pl.ds(r0, MOE_WIN), :] += c * y
        return 0

    lax.fori_loop(0, win_ref[1, g], window, 0)

    @pl.when(e == N_EXPERTS - 1)
    def _():
        acc = acc_ref[...]
        a_hi = acc.astype(BF16)
        a_lo = (acc - a_hi.astype(F32)).astype(BF16)
        pt = pt_ref[...]
        moe = (jnp.dot(pt, a_hi, preferred_element_type=F32)
               + jnp.dot(pt, a_lo, preferred_element_type=F32))
        o_ref[...] = _layer_norm(ALPHA * x1_ref[...] + moe, g_ref[...], b_ref[...])


def _moe(x1, comb, wg_b, wu_b, wd_b, g2, b2):
    return pl.pallas_call(
        _moe_kernel,
        grid=(N_TOK // MOE_TM, N_EXPERTS),
        in_specs=[
            pl.BlockSpec((MOE_TM, D_MODEL), lambda i, e: (i, 0)),
            pl.BlockSpec((MOE_TM, LANES), lambda i, e: (i, 0)),
            pl.BlockSpec((None, D_MODEL, D_EXPERT), lambda i, e: (e, 0, 0)),
            pl.BlockSpec((None, D_MODEL, D_EXPERT), lambda i, e: (e, 0, 0)),
            pl.BlockSpec((None, D_EXPERT, D_MODEL), lambda i, e: (e, 0, 0)),
            pl.BlockSpec((1, D_MODEL), lambda i, e: (0, 0)),
            pl.BlockSpec((1, D_MODEL), lambda i, e: (0, 0)),
        ],
        out_specs=pl.BlockSpec((MOE_TM, D_MODEL), lambda i, e: (i, 0)),
        out_shape=jax.ShapeDtypeStruct((N_TOK, D_MODEL), F32),
        scratch_shapes=[
            pltpu.VMEM((MOE_TM, MOE_ROWS), BF16),
            pltpu.VMEM((MOE_ROWS, D_MODEL), BF16),
            pltpu.VMEM((MOE_ROWS, LANES), F32),
            pltpu.VMEM((MOE_ROWS, D_MODEL), F32),
            pltpu.SMEM((2, N_GROUPS), jnp.int32),
        ],
        compiler_params=pltpu.CompilerParams(
            dimension_semantics=("arbitrary", "arbitrary"), vmem_limit_bytes=VMEM_LIMIT),
        name="moe_experts_ln",
    )(x1, comb, wg_b, wu_b, wd_b, g2, b2)


def kernel(x, w_in, t5_bias, hgrn_lb_logits, hgrn_norm_w, w_o, ln1_g, ln1_b, w_group, b_group,
           w_expert, b_expert, w_gate, w_up, w_down, ln2_g, ln2_b):
    x2 = x.reshape(N_TOK, D_MODEL)
    w_in_b = w_in[0].astype(BF16)
    w_vt_b = w_in_b[:, 2 * WIDTH:3 * WIDTH].T
    proj, f_logits, vt3 = _proj(x2, w_in_b, w_vt_b)
    proj3 = proj.reshape(BATCH, SEQ, IN_WIDTH)

    a = _attention(t5_bias, proj3, vt3, _bias_tiles(t5_bias))
    r = _hgrn(proj3, f_logits.reshape(BATCH, SEQ, WIDTH), hgrn_lb_logits, hgrn_norm_w)

    pad = LANES - N_EXPERTS - N_GROUPS
    w_route = jnp.concatenate(
        [w_expert[0].transpose(1, 0, 2).reshape(D_MODEL, N_EXPERTS), w_group[0],
         jnp.zeros((D_MODEL, pad), F32)], axis=1)
    b_route = jnp.concatenate(
        [b_expert[0].reshape(N_EXPERTS), b_group[0], jnp.zeros((pad,), F32)]).reshape(1, LANES)
    x1, comb = _out_proj(
        a.reshape(N_TOK, WIDTH), r.reshape(N_TOK, WIDTH), x2, w_o[0].astype(BF16),
        ln1_g, ln1_b, w_route, b_route)

    out = _moe(x1, comb, w_gate[0].astype(BF16), w_up[0].astype(BF16),
               w_down[0].astype(BF16), ln2_g, ln2_b)
    return out.reshape(BATCH, SEQ, D_MODEL)
```

```python
import functools
import math

import jax
import jax.numpy as jnp
from jax import lax
from jax.experimental import pallas as pl
from jax.experimental.pallas import tpu as pltpu

F32 = jnp.float32
BF16 = jnp.bfloat16

D_MODEL = 1024
BATCH = 2
SEQ = 8192
N_TOK = BATCH * SEQ
HEADS = 8
HEAD_DIM = 64
WIDTH = HEADS * HEAD_DIM
IN_WIDTH = 7 * WIDTH
MOBA_BLOCK = 256
N_BLOCKS = SEQ // MOBA_BLOCK
MOBA_TOPK = 3
HGRN_CHUNK = 32
N_BUCKETS = 32
MAX_DISTANCE = 128
N_GROUPS = 4
EXPERTS_PER_GROUP = 4
N_EXPERTS = 16
D_EXPERT = 512
LN_EPS = 1e-5
RMS_EPS = 1e-6
ALPHA = 2.0 ** 0.25
LOG2E = math.log2(math.e)

LANES = 128
BF16_SUBLANES = 16
NEG = -1e30
VMEM_LIMIT = 48 * 1024 * 1024

PROJ_TM = 1024
OUT_TM = 512
MOE_TM = 1024
MOE_ALIGN = 128
MOE_WIN = 256
MOE_ROWS = MOE_TM + N_GROUPS * MOE_ALIGN
GROUP_LANE = N_EXPERTS
HGRN_ROWS = 256
Q_TILES = 2
FAR_BLOCKS = 2
FAR_KEYS = FAR_BLOCKS * MOBA_BLOCK
N_FAR_STEPS = N_BLOCKS // FAR_BLOCKS

BIAS_HI_LANE = N_BLOCKS
BIAS_LO_LANE = N_BLOCKS + 1


def _nt_dot(a, b):
    return lax.dot_general(a, b, (((1,), (1,)), ((), ())), preferred_element_type=F32)


def _tn_dot(a, b):
    return lax.dot_general(a, b, (((0,), (0,)), ((), ())), preferred_element_type=F32)


def _sigmoid(x):
    return 1.0 / (1.0 + jnp.exp2(x * (-LOG2E)))


def _layer_norm(y, g, b):
    mu = jnp.mean(y, axis=-1, keepdims=True)
    d = y - mu
    var = jnp.mean(d * d, axis=-1, keepdims=True)
    return d * lax.rsqrt(var + LN_EPS) * g + b


def _bias_tile_kernel(tab_ref, out_ref):
    h = pl.program_id(0)
    ki = lax.broadcasted_iota(jnp.int32, (MOBA_BLOCK, MOBA_BLOCK), 0)
    qi = lax.broadcasted_iota(jnp.int32, (MOBA_BLOCK, MOBA_BLOCK), 1)
    max_exact = N_BUCKETS // 2
    for w in range(2):
        n = jnp.maximum(qi - ki + MOBA_BLOCK * w, 0)
        nf = jnp.maximum(n, 1).astype(F32)
        large = max_exact + (jnp.log(nf / max_exact) / math.log(MAX_DISTANCE / max_exact)
                             * (N_BUCKETS - max_exact)).astype(jnp.int32)
        large = jnp.minimum(large, N_BUCKETS - 1)
        bucket = jnp.where(n < max_exact, n, large)
        acc = jnp.zeros((MOBA_BLOCK, MOBA_BLOCK), F32)
        for bk in range(N_BUCKETS):
            acc = jnp.where(bucket == bk, tab_ref[bk, h], acc)
        out_ref[0, w] = acc


def _bias_tiles(t5_bias):
    return pl.pallas_call(
        _bias_tile_kernel,
        grid=(HEADS,),
        in_specs=[pl.BlockSpec(memory_space=pltpu.SMEM)],
        out_specs=pl.BlockSpec((1, 2, MOBA_BLOCK, MOBA_BLOCK), lambda h: (h, 0, 0, 0)),
        out_shape=jax.ShapeDtypeStruct((HEADS, 2, MOBA_BLOCK, MOBA_BLOCK), F32),
        name="t5_bias_tiles",
    )(t5_bias)


def _proj_kernel(x_ref, w_ref, wqt_ref, wvt_ref, o_ref, f_ref, qt_ref, vt_ref, xb_ref):
    n = pl.program_id(1)

    def feature_major(wt_ref, t_ref):
        t = _nt_dot(wt_ref[...], xb_ref[...]).astype(BF16)
        for c in range(PROJ_TM // MOBA_BLOCK):
            t_ref[c] = t[:, c * MOBA_BLOCK:(c + 1) * MOBA_BLOCK]

    @pl.when(n == 0)
    def _():
        xb_ref[...] = x_ref[...].astype(BF16)
        feature_major(wqt_ref, qt_ref)

    acc = jnp.dot(xb_ref[...], w_ref[...], preferred_element_type=F32)
    o_ref[...] = acc.astype(BF16)

    @pl.when(n == 2)
    def _():
        feature_major(wvt_ref, vt_ref)

    @pl.when(n == 4)
    def _():
        f_ref[...] = acc


def _proj(x2, w_in_b, w_qt_b, w_vt_b):
    tiles_per_seq = SEQ // PROJ_TM
    blocks_per_tile = PROJ_TM // MOBA_BLOCK
    t_spec = pl.BlockSpec((None, blocks_per_tile, WIDTH, MOBA_BLOCK),
                          lambda i, n: (i // tiles_per_seq, i % tiles_per_seq, 0, 0))
    t_shape = jax.ShapeDtypeStruct((BATCH, N_BLOCKS, WIDTH, MOBA_BLOCK), BF16)
    return pl.pallas_call(
        _proj_kernel,
        grid=(N_TOK // PROJ_TM, IN_WIDTH // WIDTH),
        in_specs=[
            pl.BlockSpec((PROJ_TM, D_MODEL), lambda i, n: (i, 0)),
            pl.BlockSpec((D_MODEL, WIDTH), lambda i, n: (0, n)),
            pl.BlockSpec((WIDTH, D_MODEL), lambda i, n: (0, 0)),
            pl.BlockSpec((WIDTH, D_MODEL), lambda i, n: (0, 0)),
        ],
        out_specs=[
            pl.BlockSpec((PROJ_TM, WIDTH), lambda i, n: (i, n)),
            pl.BlockSpec((PROJ_TM, WIDTH), lambda i, n: (i, 0)),
            t_spec,
            t_spec,
        ],
        out_shape=[
            jax.ShapeDtypeStruct((N_TOK, IN_WIDTH), BF16),
            jax.ShapeDtypeStruct((N_TOK, WIDTH), F32),
            t_shape,
            t_shape,
        ],
        scratch_shapes=[pltpu.VMEM((PROJ_TM, D_MODEL), BF16)],
        compiler_params=pltpu.CompilerParams(
            dimension_semantics=("arbitrary", "arbitrary"), vmem_limit_bytes=VMEM_LIMIT),
        name="in_proj",
    )(x2, w_in_b, w_qt_b, w_vt_b)


def _colreduce(x, op):
    r = x[0:8]
    for i in range(1, x.shape[0] // 8):
        r = op(r, x[8 * i:8 * i + 8])
    for shift in (4, 2, 1):
        r = op(r, pltpu.roll(r, shift, 0))
    return r


def _attn_kernel(tab_ref, qt_ref, k_ref, vt_ref, bias_ref, o_ref,
                 kmh_ref, kml_ref, ext_ref, s_ref):
    assert MOBA_BLOCK == 1 << 8
    pair = pl.program_id(1)
    step_i = pl.program_id(2)
    lane = lax.broadcasted_iota(jnp.int32, (MOBA_BLOCK, LANES), 1)

    @pl.when(step_i == 0)
    def _():
        r = lax.broadcasted_iota(jnp.int32, (LANES, SEQ), 1)
        nb = lax.broadcasted_iota(jnp.int32, (LANES, SEQ), 0)
        avg = jnp.where((r >> 8) == nb, 1.0 / MOBA_BLOCK, 0.0).astype(BF16)
        km = jnp.dot(avg, k_ref[...], preferred_element_type=F32)
        hi = km.astype(BF16)
        kmh_ref[...] = hi
        kml_ref[...] = (km - hi.astype(F32)).astype(BF16)
        kr = lax.broadcasted_iota(jnp.int32, (SEQ, LANES), 0) >> 8
        kl = lax.broadcasted_iota(jnp.int32, (SEQ, LANES), 1)
        ext_ref[...] = jnp.where((kl == kr) | (kl == BIAS_HI_LANE) | (kl == BIAS_LO_LANE),
                                 1.0, 0.0).astype(BF16)

    krow = lax.broadcasted_iota(jnp.int32, (MOBA_BLOCK, MOBA_BLOCK), 0)
    qcol = lax.broadcasted_iota(jnp.int32, (MOBA_BLOCK, MOBA_BLOCK), 1)
    frow = lax.broadcasted_iota(jnp.int32, (LANES, MOBA_BLOCK), 0)
    nrow = lax.broadcasted_iota(jnp.int32, (N_BLOCKS, MOBA_BLOCK), 0)
    nrow_f = nrow.astype(F32)
    brow = lax.broadcasted_iota(jnp.int32, (8, MOBA_BLOCK), 0)
    ones_rows = jnp.ones((BF16_SUBLANES, MOBA_BLOCK), BF16)
    zero_rows = jnp.zeros((LANES - N_BLOCKS - 8, MOBA_BLOCK), F32)
    km_hi = kmh_ref[0:N_BLOCKS, :]
    km_lo = kml_ref[0:N_BLOCKS, :]

    def v_aug(hh, block):
        vt = vt_ref[block, hh * HEAD_DIM:(hh + 1) * HEAD_DIM, :]
        return jnp.concatenate([vt, ones_rows], axis=0)

    def rows_to_tile(x8):
        return jnp.concatenate([x8] * (N_BLOCKS // 8), axis=0)

    q_far = []
    carry = []
    for t in range(Q_TILES):
        own = step_i * Q_TILES + t
        prev = jnp.maximum(own - 1, 0)
        qt2 = qt_ref[t]
        k_own = k_ref[pl.ds(pl.multiple_of(own * MOBA_BLOCK, MOBA_BLOCK), MOBA_BLOCK), :]
        k_prev = k_ref[pl.ds(pl.multiple_of(prev * MOBA_BLOCK, MOBA_BLOCK), MOBA_BLOCK), :]
        ext_prev = jnp.where(lane == own - 1, 1.0, 0.0).astype(BF16)
        k_prev_aug = jnp.concatenate([k_prev, ext_prev], axis=1)
        for hh in range(2):
            head = pair * 2 + hh
            in_head = (frow >= hh * HEAD_DIM) & (frow < (hh + 1) * HEAD_DIM)
            qs = jnp.where(in_head, qt2, jnp.zeros_like(qt2)) * jnp.asarray(HEAD_DIM ** -0.5, BF16)

            gate = (jnp.dot(km_hi, qs, preferred_element_type=F32)
                    + jnp.dot(km_lo, qs, preferred_element_type=F32))
            eligible = nrow < own
            g = jnp.where(eligible, gate, -jnp.inf)
            sel = jnp.zeros((N_BLOCKS, MOBA_BLOCK), jnp.bool_)
            for _ in range(MOBA_TOPK):
                mx = rows_to_tile(_colreduce(g, jnp.maximum))
                idx = rows_to_tile(_colreduce(jnp.where(g == mx, nrow_f, float(LANES)),
                                              jnp.minimum))
                hit = nrow_f == idx
                sel = sel | hit
                g = jnp.where(hit, -jnp.inf, g)
            sel = sel & eligible

            b31 = jnp.full((8, MOBA_BLOCK), tab_ref[N_BUCKETS - 1, head], F32)
            b31_hi = b31.astype(BF16).astype(F32)
            bias_rows = jnp.where(brow == 0, b31_hi, jnp.where(brow == 1, b31 - b31_hi, 0.0))

            def q_aug(selected):
                aug = jnp.concatenate([jnp.where(selected, 0.0, NEG), bias_rows, zero_rows], axis=0)
                return jnp.concatenate([qs, aug.astype(BF16)], axis=0)

            q_far.append(q_aug(sel & (nrow != own - 1)))

            s_prev = jnp.dot(k_prev_aug, q_aug(sel), preferred_element_type=F32) + bias_ref[hh, 1]
            s_prev = jnp.where(own >= 1, s_prev, NEG)
            s_own = jnp.dot(k_own, qs, preferred_element_type=F32) + bias_ref[hh, 0]
            s_own = jnp.where(qcol >= krow, s_own, NEG)
            m = jnp.maximum(jnp.max(s_prev, axis=0, keepdims=True),
                            jnp.max(s_own, axis=0, keepdims=True))
            p = jnp.concatenate([jnp.exp(s_prev - m), jnp.exp(s_own - m)], axis=0).astype(BF16)
            va = jnp.concatenate([v_aug(hh, prev), v_aug(hh, own)], axis=1)
            acc = jnp.dot(va, p, preferred_element_type=F32)
            carry += [m, acc]

    q_cat = jnp.concatenate(q_far, axis=1)
    n_tiles = 2 * Q_TILES

    def scores(step, slot):
        start = pl.multiple_of(jnp.minimum(step, N_FAR_STEPS - 1) * FAR_KEYS, FAR_KEYS)
        k_aug = jnp.concatenate([k_ref[pl.ds(start, FAR_KEYS), :],
                                 ext_ref[pl.ds(start, FAR_KEYS), :]], axis=1)
        s_ref[slot] = jnp.dot(k_aug, q_cat, preferred_element_type=F32)

    def absorb(step, slot, carry):
        block0 = step * FAR_BLOCKS
        out = []
        for j in range(n_tiles):
            hh = j % 2
            m, acc = carry[2 * j], carry[2 * j + 1]
            s = s_ref[slot, :, j * MOBA_BLOCK:(j + 1) * MOBA_BLOCK]
            m_new = jnp.maximum(m, jnp.max(s, axis=0, keepdims=True))
            alpha = jnp.exp(m - m_new)
            p = jnp.exp(s - m_new).astype(BF16)
            va = jnp.concatenate([v_aug(hh, block0 + c) for c in range(FAR_BLOCKS)], axis=1)
            acc = alpha * acc + jnp.dot(va, p, preferred_element_type=F32)
            out += [m_new, acc]
        return tuple(out)

    n_steps = step_i
    scores(0, 0)

    def far_body(i, carry):
        scores(2 * i + 1, 1)
        carry = absorb(2 * i, 0, carry)
        scores(2 * i + 2, 0)
        return absorb(2 * i + 1, 1, carry)

    carry = lax.fori_loop(0, n_steps // 2, far_body, tuple(carry))
    carry = lax.cond(n_steps % 2 == 1,
                     lambda c: absorb(n_steps - 1, 0, c), lambda c: c, carry)

    for t in range(Q_TILES):
        outs = []
        for hh in range(2):
            acc = carry[2 * (2 * t + hh) + 1]
            outs.append(acc[0:HEAD_DIM] / acc[HEAD_DIM:HEAD_DIM + 1])
        o_ref[t * MOBA_BLOCK:(t + 1) * MOBA_BLOCK, :] = (
            jnp.concatenate(outs, axis=0).T.astype(BF16))


def _attention(t5_bias, qt3, proj3, vt3, bias_tiles):
    n_pairs = HEADS // 2
    assert FAR_BLOCKS == Q_TILES
    return pl.pallas_call(
        _attn_kernel,
        grid=(BATCH, n_pairs, N_BLOCKS // Q_TILES),
        in_specs=[
            pl.BlockSpec(memory_space=pltpu.SMEM),
            pl.BlockSpec((None, Q_TILES, LANES, MOBA_BLOCK), lambda b, p, i: (b, i, p, 0)),
            pl.BlockSpec((None, SEQ, LANES), lambda b, p, i: (b, 0, n_pairs + p)),
            pl.BlockSpec((None, N_BLOCKS, LANES, MOBA_BLOCK), lambda b, p, i: (b, 0, p, 0)),
            pl.BlockSpec((2, 2, MOBA_BLOCK, MOBA_BLOCK), lambda b, p, i: (p, 0, 0, 0)),
        ],
        out_specs=pl.BlockSpec((None, Q_TILES * MOBA_BLOCK, LANES), lambda b, p, i: (b, i, p)),
        out_shape=jax.ShapeDtypeStruct((BATCH, SEQ, WIDTH), BF16),
        scratch_shapes=[
            pltpu.VMEM((LANES, LANES), BF16),
            pltpu.VMEM((LANES, LANES), BF16),
            pltpu.VMEM((SEQ, LANES), BF16),
            pltpu.VMEM((2, FAR_KEYS, 2 * Q_TILES * MOBA_BLOCK), F32),
        ],
        compiler_params=pltpu.CompilerParams(
            dimension_semantics=("arbitrary", "arbitrary", "arbitrary"),
            vmem_limit_bytes=VMEM_LIMIT),
        name="moba_attention",
    )(t5_bias, qt3, proj3, vt3, bias_tiles)


def _hgrn_kernel(q_ref, f_ref, i_ref, g_ref, lbl_ref, nw_ref, o_ref,
                 state_ref, bpad_ref, kpad_ref, vpad_ref, shb_ref, shk_ref, shv_ref):
    C = HGRN_CHUNK
    half = WIDTH // 2

    @pl.when(pl.program_id(1) == 0)
    def _():
        state_ref[...] = jnp.zeros_like(state_ref)

    zpad = jnp.zeros((8, WIDTH), F32)
    bpad_ref[0:8, :] = zpad
    kpad_ref[0:8, :] = zpad
    vpad_ref[0:8, :] = zpad

    l0 = lbl_ref[0:1, :]
    l1 = lbl_ref[1:2, :]
    lmx = jnp.maximum(l0, l1)
    e0 = jnp.exp(l0 - lmx)
    lb = e0 / (e0 + jnp.exp(l1 - lmx))
    nw = nw_ref[...]

    ri = lax.broadcasted_iota(jnp.int32, (C, C), 0)
    ci = lax.broadcasted_iota(jnp.int32, (C, C), 1)
    tril = jnp.where(ri >= ci, 1.0, 0.0).astype(F32)
    assert HEAD_DIM == 1 << 6
    sr = lax.broadcasted_iota(jnp.int32, (half, half), 0) >> 6
    sc = lax.broadcasted_iota(jnp.int32, (half, half), 1) >> 6
    same_head = sr == sc
    head_ones = jnp.where(same_head, 1.0, 0.0).astype(BF16)

    def chunk(c, _):
        r0 = pl.multiple_of(c * C, C)
        qq = q_ref[pl.ds(r0, C), :].astype(F32)
        qq = qq * _sigmoid(qq)
        f = lb + (1.0 - lb) * _sigmoid(f_ref[pl.ds(r0, C), :])
        gl = jnp.log(f)
        kk = 1.0 - f
        vv = i_ref[pl.ds(r0, C), :].astype(F32)
        b = jnp.dot(tril, gl, preferred_element_type=F32,
                    precision=lax.Precision.HIGHEST) * LOG2E

        bpad_ref[8:8 + C, :] = b
        kpad_ref[8:8 + C, :] = kk
        vpad_ref[8:8 + C, :] = vv
        for dd in range(8):
            shb_ref[dd] = bpad_ref[8 - dd:8 - dd + C, :]
            shk_ref[dd] = kpad_ref[8 - dd:8 - dd + C, :]
            shv_ref[dd] = vpad_ref[8 - dd:8 - dd + C, :]

        o_parts = []
        for j in range(C // 8):
            rows = C - 8 * j
            bt = b[8 * j:, :]
            qt = qq[8 * j:, :]
            terms = []
            vs_all = []
            for dd in range(8):
                terms.append(qt * shk_ref[dd, 0:rows, :] * jnp.exp2(bt - shb_ref[dd, 0:rows, :]))
                vs_all.append(shv_ref[dd, 0:rows, :])
            t_all = jnp.concatenate(terms, axis=0).astype(BF16)
            a = jnp.concatenate(
                [jnp.dot(t_all[:, 0:half], head_ones, preferred_element_type=F32),
                 jnp.dot(t_all[:, half:], head_ones, preferred_element_type=F32)], axis=1)
            av = a * jnp.concatenate(vs_all, axis=0)
            part = av[0:rows]
            for dd in range(1, 8):
                part = part + av[dd * rows:(dd + 1) * rows]
            o_parts.append(part)
        o = o_parts[0]
        for j in range(1, C // 8):
            o = o + jnp.concatenate([jnp.zeros((8 * j, WIDTH), F32), o_parts[j]], axis=0)

        qe = (qq * jnp.exp2(b)).astype(BF16)
        b_last = b[C - 1:C, :]
        kd = kk * jnp.exp2(b_last - b)
        dec = jnp.exp2(b_last)
        inter = []
        for a_i in range(2):
            sl = slice(a_i * half, (a_i + 1) * half)
            st = state_ref[a_i]
            inter.append(_nt_dot(qe[:, sl], st.astype(BF16)))
            upd = _tn_dot(vv[:, sl].astype(BF16), kd[:, sl].astype(BF16))
            state_ref[a_i] = st * dec[:, sl] + jnp.where(same_head, upd, 0.0)
        o = o + jnp.concatenate(inter, axis=1)

        oo = (o * o).astype(BF16)
        ms = jnp.concatenate(
            [jnp.dot(oo[:, 0:half], head_ones, preferred_element_type=F32),
             jnp.dot(oo[:, half:], head_ones, preferred_element_type=F32)],
            axis=1) * (1.0 / HEAD_DIM)
        r = o * lax.rsqrt(ms + RMS_EPS) * nw
        r = r * _sigmoid(g_ref[pl.ds(r0, C), :].astype(F32))
        o_ref[pl.ds(r0, C), :] = r.astype(BF16)
        return 0

    lax.fori_loop(0, HGRN_ROWS // C, chunk, 0)


def _hgrn(proj3, f3, lb_logits, norm_w):
    blk = lambda col: pl.BlockSpec((None, HGRN_ROWS, WIDTH), lambda b, i, col=col: (b, i, col))
    return pl.pallas_call(
        _hgrn_kernel,
        grid=(BATCH, SEQ // HGRN_ROWS),
        in_specs=[
            blk(3),
            pl.BlockSpec((None, HGRN_ROWS, WIDTH), lambda b, i: (b, i, 0)),
            blk(5),
            blk(6),
            pl.BlockSpec((2, WIDTH), lambda b, i: (0, 0)),
            pl.BlockSpec((1, WIDTH), lambda b, i: (0, 0)),
        ],
        out_specs=pl.BlockSpec((None, HGRN_ROWS, WIDTH), lambda b, i: (b, i, 0)),
        out_shape=jax.ShapeDtypeStruct((BATCH, SEQ, WIDTH), BF16),
        scratch_shapes=[
            pltpu.VMEM((2, WIDTH // 2, WIDTH // 2), F32),
            pltpu.VMEM((8 + HGRN_CHUNK, WIDTH), F32),
            pltpu.VMEM((8 + HGRN_CHUNK, WIDTH), F32),
            pltpu.VMEM((8 + HGRN_CHUNK, WIDTH), F32),
            pltpu.VMEM((8, HGRN_CHUNK, WIDTH), F32),
            pltpu.VMEM((8, HGRN_CHUNK, WIDTH), F32),
            pltpu.VMEM((8, HGRN_CHUNK, WIDTH), F32),
        ],
        compiler_params=pltpu.CompilerParams(
            dimension_semantics=("arbitrary", "arbitrary"), vmem_limit_bytes=VMEM_LIMIT),
        name="hgrn2",
    )(proj3, f3, proj3, proj3, lb_logits, norm_w)


def _out_kernel(a_ref, r_ref, x_ref, wo_ref, g_ref, b_ref, wr_ref, br_ref,
                x1_ref, comb_ref):
    mix = jnp.dot(a_ref[...], wo_ref[0:WIDTH, :], preferred_element_type=F32)
    mix = mix + jnp.dot(r_ref[...], wo_ref[WIDTH:, :], preferred_element_type=F32)
    x1 = _layer_norm(ALPHA * x_ref[...] + mix, g_ref[...], b_ref[...])
    x1_ref[...] = x1

    logits = jnp.dot(x1, wr_ref[...], preferred_element_type=F32,
                     precision=lax.Precision.HIGHEST) + br_ref[...]
    assert EXPERTS_PER_GROUP == 1 << 2
    lane_i = lax.broadcasted_iota(jnp.int32, logits.shape, 1)
    lane = lane_i.astype(F32)
    grp_of_lane = (lane_i >> 2).astype(F32)
    none = float(LANES)
    is_g = (lane_i >= N_EXPERTS) & (lane_i < N_EXPERTS + N_GROUPS)
    gl = jnp.where(is_g, logits, -jnp.inf)
    ge = jnp.exp(gl - jnp.max(gl, axis=-1, keepdims=True))
    gp = ge / jnp.sum(ge, axis=-1, keepdims=True)
    g_w = jnp.max(gp, axis=-1, keepdims=True)
    g_lane = jnp.min(jnp.where(is_g & (gp == g_w), lane, none), axis=-1, keepdims=True)
    g_idx = g_lane - float(N_EXPERTS)

    in_grp = (lane_i < N_EXPERTS) & (grp_of_lane == g_idx)
    el = jnp.where(in_grp, logits, -jnp.inf)
    ee = jnp.exp(el - jnp.max(el, axis=-1, keepdims=True))
    ep = ee / jnp.sum(ee, axis=-1, keepdims=True)
    p1 = jnp.max(ep, axis=-1, keepdims=True)
    i1 = jnp.min(jnp.where(in_grp & (ep == p1), lane, none), axis=-1, keepdims=True)
    rest = in_grp & (lane != i1)
    ep2 = jnp.where(rest, ep, -1.0)
    p2 = jnp.max(ep2, axis=-1, keepdims=True)
    i2 = jnp.min(jnp.where(rest & (ep2 == p2), lane, none), axis=-1, keepdims=True)
    den = p1 + p2
    comb = jnp.where(lane == i1, g_w * (p1 / den), 0.0)
    comb = jnp.where(lane == i2, g_w * (p2 / den), comb)
    comb_ref[...] = jnp.where(lane_i == GROUP_LANE, g_idx, comb)


def _out_proj(a2, r2, x2, wo_b, g1, b1, w_route, b_route):
    row = lambda w: pl.BlockSpec((OUT_TM, w), lambda i: (i, 0))
    full = lambda s: pl.BlockSpec(s, lambda i: (0, 0))
    return pl.pallas_call(
        _out_kernel,
        grid=(N_TOK // OUT_TM,),
        in_specs=[row(WIDTH), row(WIDTH), row(D_MODEL), full((D_MODEL, D_MODEL)),
                  full((1, D_MODEL)), full((1, D_MODEL)), full((D_MODEL, LANES)), full((1, LANES))],
        out_specs=[row(D_MODEL), row(LANES)],
        out_shape=[jax.ShapeDtypeStruct((N_TOK, D_MODEL), F32),
                   jax.ShapeDtypeStruct((N_TOK, LANES), F32)],
        compiler_params=pltpu.CompilerParams(
            dimension_semantics=("arbitrary",), vmem_limit_bytes=VMEM_LIMIT),
        name="out_proj_ln_route",
    )(a2, r2, x2, wo_b, g1, b1, w_route, b_route)


def _moe_kernel(x1_ref, comb_ref, wg_ref, wu_ref, wd_ref, g_ref, b_ref, o_ref,
                pt_ref, xs_ref, cs_ref, acc_ref, win_ref):
    e = pl.program_id(1)
    assert MOE_ALIGN == 1 << 7 and EXPERTS_PER_GROUP == 1 << 2

    @pl.when(e == 0)
    def _():
        comb = comb_ref[...]
        lane = lax.broadcasted_iota(jnp.int32, comb.shape, 1)
        lane_f = lane.astype(F32)
        gid = jnp.sum(jnp.where(lane == GROUP_LANE, comb, 0.0), axis=-1, keepdims=True)
        mine = lane_f == gid
        ti = lax.broadcasted_iota(jnp.int32, (MOE_TM, MOE_TM), 0)
        tj = lax.broadcasted_iota(jnp.int32, (MOE_TM, MOE_TM), 1)
        tril = jnp.where(ti >= tj, 1.0, 0.0).astype(BF16)
        incl = jnp.dot(tril, jnp.where(mine, 1.0, 0.0).astype(BF16),
                       preferred_element_type=F32)
        rank = jnp.sum(jnp.where(mine, incl, 0.0), axis=-1, keepdims=True) - 1.0
        counts = incl[MOE_TM - 1:MOE_TM, :]
        lane_row = lane[0:1, :]
        start = jnp.int32(0)
        base = jnp.zeros_like(gid)
        for g in range(N_GROUPS):
            cnt = jnp.sum(jnp.where(lane_row == g, counts, 0.0)).astype(jnp.int32)
            chunks = (cnt + (MOE_ALIGN - 1)) >> 7
            win_ref[0, g] = start
            win_ref[1, g] = chunks >> 1
            win_ref[2, g] = chunks & 1
            base = jnp.where(gid == float(g), start.astype(F32), base)
            start = start + chunks * MOE_ALIGN
        pos = base + rank
        r_iota = lax.broadcasted_iota(jnp.int32, (MOE_TM, MOE_ROWS), 1).astype(F32)
        pt = jnp.where(r_iota == pos, 1.0, 0.0).astype(BF16)
        pt_ref[...] = pt
        xs_ref[...] = _tn_dot(pt, x1_ref[...].astype(BF16)).astype(BF16)
        c_hi = comb.astype(BF16)
        c_lo = (comb - c_hi.astype(F32)).astype(BF16)
        cs_ref[...] = _tn_dot(pt, c_hi) + _tn_dot(pt, c_lo)
        acc_ref[...] = jnp.zeros_like(acc_ref)

    g = e >> 2
    row0 = win_ref[0, g]
    n_full = win_ref[1, g]

    def window(start, rows):
        r0 = pl.multiple_of(start, MOE_ALIGN)
        xw = xs_ref[pl.ds(r0, rows), :]
        hg = jnp.dot(xw, wg_ref[...], preferred_element_type=F32)
        hu = jnp.dot(xw, wu_ref[...], preferred_element_type=F32)
        h = (hg * _sigmoid(hg) * hu).astype(BF16)
        y = jnp.dot(h, wd_ref[...], preferred_element_type=F32)
        lane_w = lax.broadcasted_iota(jnp.int32, (rows, LANES), 1)
        c = jnp.sum(jnp.where(lane_w == e, cs_ref[pl.ds(r0, rows), :], 0.0),
                    axis=-1, keepdims=True)
        acc_ref[pl.ds(r0, rows), :] += c * y

    def full_window(k, _):
        window(row0 + k * MOE_WIN, MOE_WIN)
        return 0

    lax.fori_loop(0, n_full, full_window, 0)

    @pl.when(win_ref[2, g] == 1)
    def _():
        window(row0 + n_full * MOE_WIN, MOE_ALIGN)

    @pl.when(e == N_EXPERTS - 1)
    def _():
        acc = acc_ref[...]
        a_hi = acc.astype(BF16)
        a_lo = (acc - a_hi.astype(F32)).astype(BF16)
        pt = pt_ref[...]
        moe = (jnp.dot(pt, a_hi, preferred_element_type=F32)
               + jnp.dot(pt, a_lo, preferred_element_type=F32))
        o_ref[...] = _layer_norm(ALPHA * x1_ref[...] + moe, g_ref[...], b_ref[...])


def _moe(x1, comb, wg_b, wu_b, wd_b, g2, b2):
    return pl.pallas_call(
        _moe_kernel,
        grid=(N_TOK // MOE_TM, N_EXPERTS),
        in_specs=[
            pl.BlockSpec((MOE_TM, D_MODEL), lambda i, e: (i, 0)),
            pl.BlockSpec((MOE_TM, LANES), lambda i, e: (i, 0)),
            pl.BlockSpec((None, D_MODEL, D_EXPERT), lambda i, e: (e, 0, 0)),
            pl.BlockSpec((None, D_MODEL, D_EXPERT), lambda i, e: (e, 0, 0)),
            pl.BlockSpec((None, D_EXPERT, D_MODEL), lambda i, e: (e, 0, 0)),
            pl.BlockSpec((1, D_MODEL), lambda i, e: (0, 0)),
            pl.BlockSpec((1, D_MODEL), lambda i, e: (0, 0)),
        ],
        out_specs=pl.BlockSpec((MOE_TM, D_MODEL), lambda i, e: (i, 0)),
        out_shape=jax.ShapeDtypeStruct((N_TOK, D_MODEL), F32),
        scratch_shapes=[
            pltpu.VMEM((MOE_TM, MOE_ROWS), BF16),
            pltpu.VMEM((MOE_ROWS, D_MODEL), BF16),
            pltpu.VMEM((MOE_ROWS, LANES), F32),
            pltpu.VMEM((MOE_ROWS, D_MODEL), F32),
            pltpu.SMEM((3, N_GROUPS), jnp.int32),
        ],
        compiler_params=pltpu.CompilerParams(
            dimension_semantics=("arbitrary", "arbitrary"), vmem_limit_bytes=VMEM_LIMIT),
        name="moe_experts_ln",
    )(x1, comb, wg_b, wu_b, wd_b, g2, b2)


def kernel(x, w_in, t5_bias, hgrn_lb_logits, hgrn_norm_w, w_o, ln1_g, ln1_b, w_group, b_group,
           w_expert, b_expert, w_gate, w_up, w_down, ln2_g, ln2_b):
    x2 = x.reshape(N_TOK, D_MODEL)
    w_in_b = w_in[0].astype(BF16)
    w_qt_b = w_in_b[:, 0:WIDTH].T
    w_vt_b = w_in_b[:, 2 * WIDTH:3 * WIDTH].T
    proj, f_logits, qt3, vt3 = _proj(x2, w_in_b, w_qt_b, w_vt_b)
    proj3 = proj.reshape(BATCH, SEQ, IN_WIDTH)

    a = _attention(t5_bias, qt3, proj3, vt3, _bias_tiles(t5_bias))
    r = _hgrn(proj3, f_logits.reshape(BATCH, SEQ, WIDTH), hgrn_lb_logits, hgrn_norm_w)

    pad = LANES - N_EXPERTS - N_GROUPS
    w_route = jnp.concatenate(
        [w_expert[0].transpose(1, 0, 2).reshape(D_MODEL, N_EXPERTS), w_group[0],
         jnp.zeros((D_MODEL, pad), F32)], axis=1)
    b_route = jnp.concatenate(
        [b_expert[0].reshape(N_EXPERTS), b_group[0], jnp.zeros((pad,), F32)]).reshape(1, LANES)
    x1, comb = _out_proj(
        a.reshape(N_TOK, WIDTH), r.reshape(N_TOK, WIDTH), x2, w_o[0].astype(BF16),
        ln1_g, ln1_b, w_route, b_route)

    out = _moe(x1, comb, w_gate[0].astype(BF16), w_up[0].astype(BF16),
               w_down[0].astype(BF16), ln2_g, ln2_b)
    return out.reshape(BATCH, SEQ, D_MODEL)
```

```python
import functools
import math

import jax
import jax.numpy as jnp
from jax import lax
from jax.experimental import pallas as pl
from jax.experimental.pallas import tpu as pltpu

F32 = jnp.float32
BF16 = jnp.bfloat16

D_MODEL = 1024
BATCH = 2
SEQ = 8192
N_TOK = BATCH * SEQ
HEADS = 8
HEAD_DIM = 64
WIDTH = HEADS * HEAD_DIM
IN_WIDTH = 7 * WIDTH
ROW_WIDTH = 5 * WIDTH
COL_HQ, COL_HF, COL_HI, COL_HG = 1, 2, 3, 4
MOBA_BLOCK = 256
N_BLOCKS = SEQ // MOBA_BLOCK
MOBA_TOPK = 3
HGRN_CHUNK = 32
N_BUCKETS = 32
MAX_DISTANCE = 128
N_GROUPS = 4
EXPERTS_PER_GROUP = 4
N_EXPERTS = 16
D_EXPERT = 512
LN_EPS = 1e-5
RMS_EPS = 1e-6
ALPHA = 2.0 ** 0.25
LOG2E = math.log2(math.e)

LANES = 128
BF16_SUBLANES = 16
NEG = -1e30
VMEM_LIMIT = 48 * 1024 * 1024

PROJ_TM = 1024
OUT_TM = 512
MOE_TM = 1024
MOE_ALIGN = 128
MOE_WIN = 256
MOE_ROWS = MOE_TM + N_GROUPS * MOE_ALIGN
GROUP_LANE = N_EXPERTS
HGRN_ROWS = 256
Q_TILES = 2
FAR_BLOCKS = 2
FAR_KEYS = FAR_BLOCKS * MOBA_BLOCK
N_FAR_STEPS = N_BLOCKS // FAR_BLOCKS

BIAS_HI_LANE = N_BLOCKS
BIAS_LO_LANE = N_BLOCKS + 1


def _nt_dot(a, b):
    return lax.dot_general(a, b, (((1,), (1,)), ((), ())), preferred_element_type=F32)


def _tn_dot(a, b):
    return lax.dot_general(a, b, (((0,), (0,)), ((), ())), preferred_element_type=F32)


def _sigmoid(x):
    return 1.0 / (1.0 + jnp.exp2(x * (-LOG2E)))


def _layer_norm(y, g, b):
    mu = jnp.mean(y, axis=-1, keepdims=True)
    d = y - mu
    var = jnp.mean(d * d, axis=-1, keepdims=True)
    return d * lax.rsqrt(var + LN_EPS) * g + b


def _bias_tile_kernel(tab_ref, out_ref):
    h = pl.program_id(0)
    ki = lax.broadcasted_iota(jnp.int32, (MOBA_BLOCK, MOBA_BLOCK), 0)
    qi = lax.broadcasted_iota(jnp.int32, (MOBA_BLOCK, MOBA_BLOCK), 1)
    max_exact = N_BUCKETS // 2
    for w in range(2):
        n = jnp.maximum(qi - ki + MOBA_BLOCK * w, 0)
        nf = jnp.maximum(n, 1).astype(F32)
        scaled = (jnp.log(nf / max_exact) / math.log(MAX_DISTANCE / max_exact)
                  * (N_BUCKETS - max_exact))
        large = max_exact + jnp.floor(jnp.maximum(scaled, 0.0)).astype(jnp.int32)
        large = jnp.minimum(large, N_BUCKETS - 1)
        bucket = jnp.where(n < max_exact, n, large)
        acc = jnp.zeros((MOBA_BLOCK, MOBA_BLOCK), F32)
        for bk in range(N_BUCKETS):
            acc = jnp.where(bucket == bk, tab_ref[bk, h], acc)
        out_ref[0, w] = acc


def _bias_tiles(t5_bias):
    return pl.pallas_call(
        _bias_tile_kernel,
        grid=(HEADS,),
        in_specs=[pl.BlockSpec(memory_space=pltpu.SMEM)],
        out_specs=pl.BlockSpec((1, 2, MOBA_BLOCK, MOBA_BLOCK), lambda h: (h, 0, 0, 0)),
        out_shape=jax.ShapeDtypeStruct((HEADS, 2, MOBA_BLOCK, MOBA_BLOCK), F32),
        name="t5_bias_tiles",
    )(t5_bias)


def _proj_kernel(x_ref, w_ref, wqt_ref, wvt_ref, o_ref, f_ref, qt_ref, vt_ref, xb_ref):
    n = pl.program_id(1)

    def feature_major(wt_ref, t_ref):
        t = _nt_dot(wt_ref[...], xb_ref[...]).astype(BF16)
        for c in range(PROJ_TM // MOBA_BLOCK):
            t_ref[c] = t[:, c * MOBA_BLOCK:(c + 1) * MOBA_BLOCK]

    @pl.when(n == 0)
    def _():
        xb_ref[...] = x_ref[...].astype(BF16)
        feature_major(wqt_ref, qt_ref)

    acc = jnp.dot(xb_ref[...], w_ref[...], preferred_element_type=F32)
    o_ref[...] = acc.astype(BF16)

    @pl.when(n == 1)
    def _():
        feature_major(wvt_ref, vt_ref)

    @pl.when(n == COL_HF)
    def _():
        f_ref[...] = acc


def _proj(x2, w_in_b, w_qt_b, w_vt_b):
    tiles_per_seq = SEQ // PROJ_TM
    blocks_per_tile = PROJ_TM // MOBA_BLOCK
    t_spec = pl.BlockSpec((None, blocks_per_tile, WIDTH, MOBA_BLOCK),
                          lambda i, n: (i // tiles_per_seq, i % tiles_per_seq, 0, 0))
    t_shape = jax.ShapeDtypeStruct((BATCH, N_BLOCKS, WIDTH, MOBA_BLOCK), BF16)
    return pl.pallas_call(
        _proj_kernel,
        grid=(N_TOK // PROJ_TM, ROW_WIDTH // WIDTH),
        in_specs=[
            pl.BlockSpec((PROJ_TM, D_MODEL), lambda i, n: (i, 0)),
            pl.BlockSpec((D_MODEL, WIDTH), lambda i, n: (0, jnp.where(n == 0, 1, n + 2))),
            pl.BlockSpec((WIDTH, D_MODEL), lambda i, n: (0, 0)),
            pl.BlockSpec((WIDTH, D_MODEL), lambda i, n: (0, 0)),
        ],
        out_specs=[
            pl.BlockSpec((PROJ_TM, WIDTH), lambda i, n: (i, n)),
            pl.BlockSpec((PROJ_TM, WIDTH), lambda i, n: (i, 0)),
            t_spec,
            t_spec,
        ],
        out_shape=[
            jax.ShapeDtypeStruct((N_TOK, ROW_WIDTH), BF16),
            jax.ShapeDtypeStruct((N_TOK, WIDTH), F32),
            t_shape,
            t_shape,
        ],
        scratch_shapes=[pltpu.VMEM((PROJ_TM, D_MODEL), BF16)],
        compiler_params=pltpu.CompilerParams(
            dimension_semantics=("arbitrary", "arbitrary"), vmem_limit_bytes=VMEM_LIMIT),
        name="in_proj",
    )(x2, w_in_b, w_qt_b, w_vt_b)


def _colreduce(x, op):
    r = x[0:8]
    for i in range(1, x.shape[0] // 8):
        r = op(r, x[8 * i:8 * i + 8])
    for shift in (4, 2, 1):
        r = op(r, pltpu.roll(r, shift, 0))
    return r


def _attn_kernel(tab_ref, qt_ref, k_ref, vt_ref, bias_ref, o_ref,
                 kmh_ref, kml_ref, ext_ref, s_ref, near_ref):
    assert MOBA_BLOCK == 1 << 8
    pair = pl.program_id(1)
    step_i = pl.program_id(2)
    lane = lax.broadcasted_iota(jnp.int32, (MOBA_BLOCK, LANES), 1)

    @pl.when(step_i == 0)
    def _():
        r = lax.broadcasted_iota(jnp.int32, (LANES, SEQ), 1)
        nb = lax.broadcasted_iota(jnp.int32, (LANES, SEQ), 0)
        avg = jnp.where((r >> 8) == nb, 1.0 / MOBA_BLOCK, 0.0).astype(BF16)
        km = jnp.dot(avg, k_ref[...], preferred_element_type=F32)
        hi = km.astype(BF16)
        kmh_ref[...] = hi
        kml_ref[...] = (km - hi.astype(F32)).astype(BF16)
        kr = lax.broadcasted_iota(jnp.int32, (SEQ, LANES), 0) >> 8
        kl = lax.broadcasted_iota(jnp.int32, (SEQ, LANES), 1)
        ext_ref[...] = jnp.where((kl == kr) | (kl == BIAS_HI_LANE) | (kl == BIAS_LO_LANE),
                                 1.0, 0.0).astype(BF16)

    krow = lax.broadcasted_iota(jnp.int32, (MOBA_BLOCK, MOBA_BLOCK), 0)
    qcol = lax.broadcasted_iota(jnp.int32, (MOBA_BLOCK, MOBA_BLOCK), 1)
    frow = lax.broadcasted_iota(jnp.int32, (LANES, MOBA_BLOCK), 0)
    nrow = lax.broadcasted_iota(jnp.int32, (N_BLOCKS, MOBA_BLOCK), 0)
    nrow_f = nrow.astype(F32)
    brow = lax.broadcasted_iota(jnp.int32, (8, MOBA_BLOCK), 0)
    ones_rows = jnp.ones((BF16_SUBLANES, MOBA_BLOCK), BF16)
    zero_rows = jnp.zeros((LANES - N_BLOCKS - 8, MOBA_BLOCK), F32)
    km_hi = kmh_ref[0:N_BLOCKS, :]
    km_lo = kml_ref[0:N_BLOCKS, :]

    def v_aug(hh, block):
        vt = vt_ref[block, hh * HEAD_DIM:(hh + 1) * HEAD_DIM, :]
        return jnp.concatenate([vt, ones_rows], axis=0)

    def rows_to_tile(x8):
        return jnp.concatenate([x8] * (N_BLOCKS // 8), axis=0)

    q_far = []
    carry = []
    for t in range(Q_TILES):
        own = step_i * Q_TILES + t
        prev = jnp.maximum(own - 1, 0)
        qt2 = qt_ref[t]
        k_own = k_ref[pl.ds(pl.multiple_of(own * MOBA_BLOCK, MOBA_BLOCK), MOBA_BLOCK), :]
        k_prev = k_ref[pl.ds(pl.multiple_of(prev * MOBA_BLOCK, MOBA_BLOCK), MOBA_BLOCK), :]
        ext_prev = jnp.where(lane == own - 1, 1.0, 0.0).astype(BF16)
        k_prev_aug = jnp.concatenate([k_prev, ext_prev], axis=1)
        for hh in range(2):
            head = pair * 2 + hh
            in_head = (frow >= hh * HEAD_DIM) & (frow < (hh + 1) * HEAD_DIM)
            qs = jnp.where(in_head, qt2, jnp.zeros_like(qt2)) * jnp.asarray(HEAD_DIM ** -0.5, BF16)

            gate = (jnp.dot(km_hi, qs, preferred_element_type=F32)
                    + jnp.dot(km_lo, qs, preferred_element_type=F32))
            eligible = nrow < own
            g = jnp.where(eligible, gate, -jnp.inf)
            sel = jnp.zeros((N_BLOCKS, MOBA_BLOCK), jnp.bool_)
            for _ in range(MOBA_TOPK):
                mx = rows_to_tile(_colreduce(g, jnp.maximum))
                idx = rows_to_tile(_colreduce(jnp.where(g == mx, nrow_f, float(LANES)),
                                              jnp.minimum))
                hit = nrow_f == idx
                sel = sel | hit
                g = jnp.where(hit, -jnp.inf, g)
            sel = sel & eligible

            b31 = jnp.full((8, MOBA_BLOCK), tab_ref[N_BUCKETS - 1, head], F32)
            b31_hi = b31.astype(BF16).astype(F32)
            bias_rows = jnp.where(brow == 0, b31_hi, jnp.where(brow == 1, b31 - b31_hi, 0.0))

            def q_aug(selected):
                aug = jnp.concatenate([jnp.where(selected, 0.0, NEG), bias_rows, zero_rows], axis=0)
                return jnp.concatenate([qs, aug.astype(BF16)], axis=0)

            q_far.append(q_aug(sel & (nrow != own - 1)))

            s_prev = jnp.dot(k_prev_aug, q_aug(sel), preferred_element_type=F32) + bias_ref[hh, 1]
            near_ref[2 * t + hh, 0:MOBA_BLOCK, :] = jnp.where(own >= 1, s_prev, NEG)
            s_own = jnp.dot(k_own, qs, preferred_element_type=F32) + bias_ref[hh, 0]
            near_ref[2 * t + hh, MOBA_BLOCK:, :] = jnp.where(qcol >= krow, s_own, NEG)
            carry += [jnp.full((1, MOBA_BLOCK), NEG, F32),
                      jnp.zeros((HEAD_DIM + BF16_SUBLANES, MOBA_BLOCK), F32)]

    q_cat = jnp.concatenate(q_far, axis=1)
    n_tiles = 2 * Q_TILES

    def absorb_scores(s, va, m, acc):
        m_new = jnp.maximum(m, jnp.max(s, axis=0, keepdims=True))
        alpha = jnp.exp(m - m_new)
        p = jnp.exp(s - m_new).astype(BF16)
        return m_new, alpha * acc + jnp.dot(va, p, preferred_element_type=F32)

    def scores(step, slot):
        start = pl.multiple_of(jnp.minimum(step, N_FAR_STEPS - 1) * FAR_KEYS, FAR_KEYS)
        k_aug = jnp.concatenate([k_ref[pl.ds(start, FAR_KEYS), :],
                                 ext_ref[pl.ds(start, FAR_KEYS), :]], axis=1)
        s_ref[slot] = jnp.dot(k_aug, q_cat, preferred_element_type=F32)

    def absorb(step, slot, carry):
        block0 = step * FAR_BLOCKS
        out = []
        for j in range(n_tiles):
            hh = j % 2
            s = s_ref[slot, :, j * MOBA_BLOCK:(j + 1) * MOBA_BLOCK]
            va = jnp.concatenate([v_aug(hh, block0 + c) for c in range(FAR_BLOCKS)], axis=1)
            out += absorb_scores(s, va, carry[2 * j], carry[2 * j + 1])
        return tuple(out)

    n_steps = step_i
    scores(0, 0)

    def far_body(i, carry):
        scores(2 * i + 1, 1)
        carry = absorb(2 * i, 0, carry)
        scores(2 * i + 2, 0)
        return absorb(2 * i + 1, 1, carry)

    carry = lax.fori_loop(0, n_steps // 2, far_body, tuple(carry))
    carry = lax.cond(n_steps % 2 == 1,
                     lambda c: absorb(n_steps - 1, 0, c), lambda c: c, carry)

    for t in range(Q_TILES):
        own = step_i * Q_TILES + t
        outs = []
        for hh in range(2):
            j = 2 * t + hh
            va = jnp.concatenate([v_aug(hh, jnp.maximum(own - 1, 0)), v_aug(hh, own)], axis=1)
            _, acc = absorb_scores(near_ref[j], va, carry[2 * j], carry[2 * j + 1])
            outs.append(acc[0:HEAD_DIM] / acc[HEAD_DIM:HEAD_DIM + 1])
        o_ref[t * MOBA_BLOCK:(t + 1) * MOBA_BLOCK, :] = (
            jnp.concatenate(outs, axis=0).T.astype(BF16))


def _attention(t5_bias, qt3, proj3, vt3, bias_tiles):
    n_pairs = HEADS // 2
    assert FAR_BLOCKS == Q_TILES
    return pl.pallas_call(
        _attn_kernel,
        grid=(BATCH, n_pairs, N_BLOCKS // Q_TILES),
        in_specs=[
            pl.BlockSpec(memory_space=pltpu.SMEM),
            pl.BlockSpec((None, Q_TILES, LANES, MOBA_BLOCK), lambda b, p, i: (b, i, p, 0)),
            pl.BlockSpec((None, SEQ, LANES), lambda b, p, i: (b, 0, p)),
            pl.BlockSpec((None, N_BLOCKS, LANES, MOBA_BLOCK), lambda b, p, i: (b, 0, p, 0)),
            pl.BlockSpec((2, 2, MOBA_BLOCK, MOBA_BLOCK), lambda b, p, i: (p, 0, 0, 0)),
        ],
        out_specs=pl.BlockSpec((None, Q_TILES * MOBA_BLOCK, LANES), lambda b, p, i: (b, i, p)),
        out_shape=jax.ShapeDtypeStruct((BATCH, SEQ, WIDTH), BF16),
        scratch_shapes=[
            pltpu.VMEM((LANES, LANES), BF16),
            pltpu.VMEM((LANES, LANES), BF16),
            pltpu.VMEM((SEQ, LANES), BF16),
            pltpu.VMEM((2, FAR_KEYS, 2 * Q_TILES * MOBA_BLOCK), F32),
            pltpu.VMEM((2 * Q_TILES, 2 * MOBA_BLOCK, MOBA_BLOCK), F32),
        ],
        compiler_params=pltpu.CompilerParams(
            dimension_semantics=("arbitrary", "arbitrary", "arbitrary"),
            vmem_limit_bytes=VMEM_LIMIT),
        name="moba_attention",
    )(t5_bias, qt3, proj3, vt3, bias_tiles)


def _hgrn_kernel(q_ref, f_ref, i_ref, g_ref, lbl_ref, nw_ref, o_ref,
                 state_ref, bpad_ref, kpad_ref, vpad_ref, shb_ref, shk_ref, shv_ref):
    C = HGRN_CHUNK
    half = WIDTH // 2

    @pl.when(pl.program_id(1) == 0)
    def _():
        state_ref[...] = jnp.zeros_like(state_ref)

    zpad = jnp.zeros((8, WIDTH), F32)
    bpad_ref[0:8, :] = zpad
    kpad_ref[0:8, :] = zpad
    vpad_ref[0:8, :] = zpad

    l0 = lbl_ref[0:1, :]
    l1 = lbl_ref[1:2, :]
    lmx = jnp.maximum(l0, l1)
    e0 = jnp.exp(l0 - lmx)
    lb = e0 / (e0 + jnp.exp(l1 - lmx))
    nw = nw_ref[...]

    ri = lax.broadcasted_iota(jnp.int32, (C, C), 0)
    ci = lax.broadcasted_iota(jnp.int32, (C, C), 1)
    tril = jnp.where(ri >= ci, 1.0, 0.0).astype(F32)
    assert HEAD_DIM == 1 << 6
    sr = lax.broadcasted_iota(jnp.int32, (half, half), 0) >> 6
    sc = lax.broadcasted_iota(jnp.int32, (half, half), 1) >> 6
    same_head = sr == sc
    head_ones = jnp.where(same_head, 1.0, 0.0).astype(BF16)

    def chunk(c, _):
        r0 = pl.multiple_of(c * C, C)
        qq = q_ref[pl.ds(r0, C), :].astype(F32)
        qq = qq * _sigmoid(qq)
        f = lb + (1.0 - lb) * _sigmoid(f_ref[pl.ds(r0, C), :])
        gl = jnp.log(f)
        kk = 1.0 - f
        vv = i_ref[pl.ds(r0, C), :].astype(F32)
        b = jnp.dot(tril, gl, preferred_element_type=F32,
                    precision=lax.Precision.HIGHEST) * LOG2E

        bpad_ref[8:8 + C, :] = b
        kpad_ref[8:8 + C, :] = kk
        vpad_ref[8:8 + C, :] = vv
        for dd in range(8):
            shb_ref[dd] = bpad_ref[8 - dd:8 - dd + C, :]
            shk_ref[dd] = kpad_ref[8 - dd:8 - dd + C, :]
            shv_ref[dd] = vpad_ref[8 - dd:8 - dd + C, :]

        o_parts = []
        for j in range(C // 8):
            rows = C - 8 * j
            bt = b[8 * j:, :]
            qt = qq[8 * j:, :]
            terms = []
            vs_all = []
            for dd in range(8):
                terms.append(qt * shk_ref[dd, 0:rows, :] * jnp.exp2(bt - shb_ref[dd, 0:rows, :]))
                vs_all.append(shv_ref[dd, 0:rows, :])
            t_all = jnp.concatenate(terms, axis=0).astype(BF16)
            a = jnp.concatenate(
                [jnp.dot(t_all[:, 0:half], head_ones, preferred_element_type=F32),
                 jnp.dot(t_all[:, half:], head_ones, preferred_element_type=F32)], axis=1)
            av = a * jnp.concatenate(vs_all, axis=0)
            part = av[0:rows]
            for dd in range(1, 8):
                part = part + av[dd * rows:(dd + 1) * rows]
            o_parts.append(part)
        o = o_parts[0]
        for j in range(1, C // 8):
            o = o + jnp.concatenate([jnp.zeros((8 * j, WIDTH), F32), o_parts[j]], axis=0)

        qe = (qq * jnp.exp2(b)).astype(BF16)
        b_last = b[C - 1:C, :]
        kd = kk * jnp.exp2(b_last - b)
        dec = jnp.exp2(b_last)
        inter = []
        for a_i in range(2):
            sl = slice(a_i * half, (a_i + 1) * half)
            st = state_ref[a_i]
            inter.append(_nt_dot(qe[:, sl], st.astype(BF16)))
            upd = _tn_dot(vv[:, sl].astype(BF16), kd[:, sl].astype(BF16))
            state_ref[a_i] = st * dec[:, sl] + jnp.where(same_head, upd, 0.0)
        o = o + jnp.concatenate(inter, axis=1)

        oo = (o * o).astype(BF16)
        ms = jnp.concatenate(
            [jnp.dot(oo[:, 0:half], head_ones, preferred_element_type=F32),
             jnp.dot(oo[:, half:], head_ones, preferred_element_type=F32)],
            axis=1) * (1.0 / HEAD_DIM)
        r = o * lax.rsqrt(ms + RMS_EPS) * nw
        r = r * _sigmoid(g_ref[pl.ds(r0, C), :].astype(F32))
        o_ref[pl.ds(r0, C), :] = r.astype(BF16)
        return 0

    lax.fori_loop(0, HGRN_ROWS // C, chunk, 0)


def _hgrn(proj3, f3, lb_logits, norm_w):
    blk = lambda col: pl.BlockSpec((None, HGRN_ROWS, WIDTH), lambda b, i, col=col: (b, i, col))
    return pl.pallas_call(
        _hgrn_kernel,
        grid=(BATCH, SEQ // HGRN_ROWS),
        in_specs=[
            blk(COL_HQ),
            pl.BlockSpec((None, HGRN_ROWS, WIDTH), lambda b, i: (b, i, 0)),
            blk(COL_HI),
            blk(COL_HG),
            pl.BlockSpec((2, WIDTH), lambda b, i: (0, 0)),
            pl.BlockSpec((1, WIDTH), lambda b, i: (0, 0)),
        ],
        out_specs=pl.BlockSpec((None, HGRN_ROWS, WIDTH), lambda b, i: (b, i, 0)),
        out_shape=jax.ShapeDtypeStruct((BATCH, SEQ, WIDTH), BF16),
        scratch_shapes=[
            pltpu.VMEM((2, WIDTH // 2, WIDTH // 2), F32),
            pltpu.VMEM((8 + HGRN_CHUNK, WIDTH), F32),
            pltpu.VMEM((8 + HGRN_CHUNK, WIDTH), F32),
            pltpu.VMEM((8 + HGRN_CHUNK, WIDTH), F32),
            pltpu.VMEM((8, HGRN_CHUNK, WIDTH), F32),
            pltpu.VMEM((8, HGRN_CHUNK, WIDTH), F32),
            pltpu.VMEM((8, HGRN_CHUNK, WIDTH), F32),
        ],
        compiler_params=pltpu.CompilerParams(
            dimension_semantics=("arbitrary", "arbitrary"), vmem_limit_bytes=VMEM_LIMIT),
        name="hgrn2",
    )(proj3, f3, proj3, proj3, lb_logits, norm_w)


def _out_kernel(a_ref, r_ref, x_ref, wo_ref, g_ref, b_ref, wr_ref, br_ref,
                x1_ref, comb_ref):
    mix = jnp.dot(a_ref[...], wo_ref[0:WIDTH, :], preferred_element_type=F32)
    mix = mix + jnp.dot(r_ref[...], wo_ref[WIDTH:, :], preferred_element_type=F32)
    x1 = _layer_norm(ALPHA * x_ref[...] + mix, g_ref[...], b_ref[...])
    x1_ref[...] = x1

    x_hi = x1.astype(BF16)
    x_lo = (x1 - x_hi.astype(F32)).astype(BF16)
    wr = wr_ref[...]
    w_hi = wr.astype(BF16)
    w_lo = (wr - w_hi.astype(F32)).astype(BF16)
    logits = (jnp.dot(x_hi, w_hi, preferred_element_type=F32)
              + jnp.dot(x_lo, w_hi, preferred_element_type=F32)
              + jnp.dot(x_hi, w_lo, preferred_element_type=F32)) + br_ref[...]
    assert EXPERTS_PER_GROUP == 1 << 2
    lane_i = lax.broadcasted_iota(jnp.int32, logits.shape, 1)
    lane = lane_i.astype(F32)
    grp_of_lane = (lane_i >> 2).astype(F32)
    none = float(LANES)
    is_g = (lane_i >= N_EXPERTS) & (lane_i < N_EXPERTS + N_GROUPS)
    gl = jnp.where(is_g, logits, -jnp.inf)
    ge = jnp.exp(gl - jnp.max(gl, axis=-1, keepdims=True))
    gp = ge / jnp.sum(ge, axis=-1, keepdims=True)
    g_w = jnp.max(gp, axis=-1, keepdims=True)
    g_lane = jnp.min(jnp.where(is_g & (gp == g_w), lane, none), axis=-1, keepdims=True)
    g_idx = g_lane - float(N_EXPERTS)

    in_grp = (lane_i < N_EXPERTS) & (grp_of_lane == g_idx)
    el = jnp.where(in_grp, logits, -jnp.inf)
    ee = jnp.exp(el - jnp.max(el, axis=-1, keepdims=True))
    ep = ee / jnp.sum(ee, axis=-1, keepdims=True)
    p1 = jnp.max(ep, axis=-1, keepdims=True)
    i1 = jnp.min(jnp.where(in_grp & (ep == p1), lane, none), axis=-1, keepdims=True)
    rest = in_grp & (lane != i1)
    ep2 = jnp.where(rest, ep, -1.0)
    p2 = jnp.max(ep2, axis=-1, keepdims=True)
    i2 = jnp.min(jnp.where(rest & (ep2 == p2), lane, none), axis=-1, keepdims=True)
    den = p1 + p2
    comb = jnp.where(lane == i1, g_w * (p1 / den), 0.0)
    comb = jnp.where(lane == i2, g_w * (p2 / den), comb)
    comb_ref[...] = jnp.where(lane_i == GROUP_LANE, g_idx, comb)


def _out_proj(a2, r2, x2, wo_b, g1, b1, w_route, b_route):
    row = lambda w: pl.BlockSpec((OUT_TM, w), lambda i: (i, 0))
    full = lambda s: pl.BlockSpec(s, lambda i: (0, 0))
    return pl.pallas_call(
        _out_kernel,
        grid=(N_TOK // OUT_TM,),
        in_specs=[row(WIDTH), row(WIDTH), row(D_MODEL), full((D_MODEL, D_MODEL)),
                  full((1, D_MODEL)), full((1, D_MODEL)), full((D_MODEL, LANES)), full((1, LANES))],
        out_specs=[row(D_MODEL), row(LANES)],
        out_shape=[jax.ShapeDtypeStruct((N_TOK, D_MODEL), F32),
                   jax.ShapeDtypeStruct((N_TOK, LANES), F32)],
        compiler_params=pltpu.CompilerParams(
            dimension_semantics=("arbitrary",), vmem_limit_bytes=VMEM_LIMIT),
        name="out_proj_ln_route",
    )(a2, r2, x2, wo_b, g1, b1, w_route, b_route)


def _moe_kernel(x1_ref, comb_ref, wg_ref, wu_ref, wd_ref, g_ref, b_ref, o_ref,
                pt_ref, xs_ref, cs_ref, acc_ref, win_ref):
    e = pl.program_id(1)
    assert MOE_ALIGN == 1 << 7 and EXPERTS_PER_GROUP == 1 << 2

    @pl.when(e == 0)
    def _():
        comb = comb_ref[...]
        lane = lax.broadcasted_iota(jnp.int32, comb.shape, 1)
        lane_f = lane.astype(F32)
        gid = jnp.sum(jnp.where(lane == GROUP_LANE, comb, 0.0), axis=-1, keepdims=True)
        mine = lane_f == gid
        ti = lax.broadcasted_iota(jnp.int32, (MOE_TM, MOE_TM), 0)
        tj = lax.broadcasted_iota(jnp.int32, (MOE_TM, MOE_TM), 1)
        tril = jnp.where(ti >= tj, 1.0, 0.0).astype(BF16)
        incl = jnp.dot(tril, jnp.where(mine, 1.0, 0.0).astype(BF16),
                       preferred_element_type=F32)
        rank = jnp.sum(jnp.where(mine, incl, 0.0), axis=-1, keepdims=True) - 1.0
        counts = incl[MOE_TM - 1:MOE_TM, :]
        lane_row = lane[0:1, :]
        start = jnp.int32(0)
        base = jnp.zeros_like(gid)
        for g in range(N_GROUPS):
            cnt = jnp.sum(jnp.where(lane_row == g, counts, 0.0)).astype(jnp.int32)
            chunks = (cnt + (MOE_ALIGN - 1)) >> 7
            win_ref[0, g] = start
            win_ref[1, g] = chunks >> 1
            win_ref[2, g] = chunks & 1
            base = jnp.where(gid == float(g), start.astype(F32), base)
            start = start + chunks * MOE_ALIGN
        pos = base + rank
        r_iota = lax.broadcasted_iota(jnp.int32, (MOE_TM, MOE_ROWS), 1).astype(F32)
        pt = jnp.where(r_iota == pos, 1.0, 0.0).astype(BF16)
        pt_ref[...] = pt
        xs_ref[...] = _tn_dot(pt, x1_ref[...].astype(BF16)).astype(BF16)
        c_hi = comb.astype(BF16)
        c_lo = (comb - c_hi.astype(F32)).astype(BF16)
        cs_ref[...] = _tn_dot(pt, c_hi) + _tn_dot(pt, c_lo)
        acc_ref[...] = jnp.zeros_like(acc_ref)

    g = e >> 2
    row0 = win_ref[0, g]
    n_full = win_ref[1, g]

    def window(start, rows):
        r0 = pl.multiple_of(start, MOE_ALIGN)
        xw = xs_ref[pl.ds(r0, rows), :]
        hg = jnp.dot(xw, wg_ref[...], preferred_element_type=F32)
        hu = jnp.dot(xw, wu_ref[...], preferred_element_type=F32)
        h = (hg * _sigmoid(hg) * hu).astype(BF16)
        y = jnp.dot(h, wd_ref[...], preferred_element_type=F32)
        lane_w = lax.broadcasted_iota(jnp.int32, (rows, LANES), 1)
        c = jnp.sum(jnp.where(lane_w == e, cs_ref[pl.ds(r0, rows), :], 0.0),
                    axis=-1, keepdims=True)
        acc_ref[pl.ds(r0, rows), :] += c * y

    def full_window(k, _):
        window(row0 + k * MOE_WIN, MOE_WIN)
        return 0

    lax.fori_loop(0, n_full, full_window, 0)

    @pl.when(win_ref[2, g] == 1)
    def _():
        window(row0 + n_full * MOE_WIN, MOE_ALIGN)

    @pl.when(e == N_EXPERTS - 1)
    def _():
        moe = jnp.dot(pt_ref[...], acc_ref[...].astype(BF16), preferred_element_type=F32)
        o_ref[...] = _layer_norm(ALPHA * x1_ref[...] + moe, g_ref[...], b_ref[...])


def _moe(x1, comb, wg_b, wu_b, wd_b, g2, b2):
    return pl.pallas_call(
        _moe_kernel,
        grid=(N_TOK // MOE_TM, N_EXPERTS),
        in_specs=[
            pl.BlockSpec((MOE_TM, D_MODEL), lambda i, e: (i, 0)),
            pl.BlockSpec((MOE_TM, LANES), lambda i, e: (i, 0)),
            pl.BlockSpec((None, D_MODEL, D_EXPERT), lambda i, e: (e, 0, 0)),
            pl.BlockSpec((None, D_MODEL, D_EXPERT), lambda i, e: (e, 0, 0)),
            pl.BlockSpec((None, D_EXPERT, D_MODEL), lambda i, e: (e, 0, 0)),
            pl.BlockSpec((1, D_MODEL), lambda i, e: (0, 0)),
            pl.BlockSpec((1, D_MODEL), lambda i, e: (0, 0)),
        ],
        out_specs=pl.BlockSpec((MOE_TM, D_MODEL), lambda i, e: (i, 0)),
        out_shape=jax.ShapeDtypeStruct((N_TOK, D_MODEL), F32),
        scratch_shapes=[
            pltpu.VMEM((MOE_TM, MOE_ROWS), BF16),
            pltpu.VMEM((MOE_ROWS, D_MODEL), BF16),
            pltpu.VMEM((MOE_ROWS, LANES), F32),
            pltpu.VMEM((MOE_ROWS, D_MODEL), F32),
            pltpu.SMEM((3, N_GROUPS), jnp.int32),
        ],
        compiler_params=pltpu.CompilerParams(
            dimension_semantics=("arbitrary", "arbitrary"), vmem_limit_bytes=VMEM_LIMIT),
        name="moe_experts_ln",
    )(x1, comb, wg_b, wu_b, wd_b, g2, b2)


def kernel(x, w_in, t5_bias, hgrn_lb_logits, hgrn_norm_w, w_o, ln1_g, ln1_b, w_group, b_group,
           w_expert, b_expert, w_gate, w_up, w_down, ln2_g, ln2_b):
    x2 = x.reshape(N_TOK, D_MODEL)
    w_in_b = w_in[0].astype(BF16)
    w_qt_b = w_in_b[:, 0:WIDTH].T
    w_vt_b = w_in_b[:, 2 * WIDTH:3 * WIDTH].T
    proj, f_logits, qt3, vt3 = _proj(x2, w_in_b, w_qt_b, w_vt_b)
    proj3 = proj.reshape(BATCH, SEQ, ROW_WIDTH)

    a = _attention(t5_bias, qt3, proj3, vt3, _bias_tiles(t5_bias))
    r = _hgrn(proj3, f_logits.reshape(BATCH, SEQ, WIDTH), hgrn_lb_logits, hgrn_norm_w)

    pad = LANES - N_EXPERTS - N_GROUPS
    w_route = jnp.concatenate(
        [w_expert[0].transpose(1, 0, 2).reshape(D_MODEL, N_EXPERTS), w_group[0],
         jnp.zeros((D_MODEL, pad), F32)], axis=1)
    b_route = jnp.concatenate(
        [b_expert[0].reshape(N_EXPERTS), b_group[0], jnp.zeros((pad,), F32)]).reshape(1, LANES)
    x1, comb = _out_proj(
        a.reshape(N_TOK, WIDTH), r.reshape(N_TOK, WIDTH), x2, w_o[0].astype(BF16),
        ln1_g, ln1_b, w_route, b_route)

    out = _moe(x1, comb, w_gate[0].astype(BF16), w_up[0].astype(BF16),
               w_down[0].astype(BF16), ln2_g, ln2_b)
    return out.reshape(BATCH, SEQ, D_MODEL)
```

```python
import functools
import math

import jax
import jax.numpy as jnp
from jax import lax
from jax.experimental import pallas as pl
from jax.experimental.pallas import tpu as pltpu

F32 = jnp.float32
BF16 = jnp.bfloat16

D_MODEL = 1024
BATCH = 2
SEQ = 8192
N_TOK = BATCH * SEQ
HEADS = 8
HEAD_DIM = 64
WIDTH = HEADS * HEAD_DIM
IN_WIDTH = 7 * WIDTH
ROW_WIDTH = 5 * WIDTH
COL_HQ, COL_HF, COL_HI, COL_HG = 1, 2, 3, 4
MOBA_BLOCK = 256
N_BLOCKS = SEQ // MOBA_BLOCK
MOBA_TOPK = 3
HGRN_CHUNK = 32
N_BUCKETS = 32
MAX_DISTANCE = 128
N_GROUPS = 4
EXPERTS_PER_GROUP = 4
N_EXPERTS = 16
D_EXPERT = 512
LN_EPS = 1e-5
RMS_EPS = 1e-6
ALPHA = 2.0 ** 0.25
LOG2E = math.log2(math.e)

LANES = 128
BF16_SUBLANES = 16
NEG = -1e30
VMEM_LIMIT = 48 * 1024 * 1024

PROJ_TM = 1024
OUT_TM = 512
MOE_TM = 1024
MOE_ALIGN = 128
MOE_WIN = 256
MOE_ROWS = MOE_TM + N_GROUPS * MOE_ALIGN
GROUP_LANE = N_EXPERTS
HGRN_ROWS = 256
Q_TILES = 2
FAR_BLOCKS = 2
FAR_KEYS = FAR_BLOCKS * MOBA_BLOCK
N_FAR_STEPS = N_BLOCKS // FAR_BLOCKS

BIAS_HI_LANE = N_BLOCKS
BIAS_LO_LANE = N_BLOCKS + 1


def _nt_dot(a, b):
    return lax.dot_general(a, b, (((1,), (1,)), ((), ())), preferred_element_type=F32)


def _tn_dot(a, b):
    return lax.dot_general(a, b, (((0,), (0,)), ((), ())), preferred_element_type=F32)


def _sigmoid(x):
    return 1.0 / (1.0 + jnp.exp2(x * (-LOG2E)))


def _layer_norm(y, g, b):
    mu = jnp.mean(y, axis=-1, keepdims=True)
    d = y - mu
    var = jnp.mean(d * d, axis=-1, keepdims=True)
    return d * lax.rsqrt(var + LN_EPS) * g + b


def _bias_tile_kernel(tab_ref, out_ref):
    h = pl.program_id(0)
    ki = lax.broadcasted_iota(jnp.int32, (MOBA_BLOCK, MOBA_BLOCK), 0)
    qi = lax.broadcasted_iota(jnp.int32, (MOBA_BLOCK, MOBA_BLOCK), 1)
    max_exact = N_BUCKETS // 2
    for w in range(2):
        n = jnp.maximum(qi - ki + MOBA_BLOCK * w, 0)
        nf = jnp.maximum(n, 1).astype(F32)
        scaled = (jnp.log(nf / max_exact) / math.log(MAX_DISTANCE / max_exact)
                  * (N_BUCKETS - max_exact))
        large = max_exact + jnp.floor(jnp.maximum(scaled, 0.0)).astype(jnp.int32)
        large = jnp.minimum(large, N_BUCKETS - 1)
        bucket = jnp.where(n < max_exact, n, large)
        acc = jnp.zeros((MOBA_BLOCK, MOBA_BLOCK), F32)
        for bk in range(N_BUCKETS):
            acc = jnp.where(bucket == bk, tab_ref[bk, h], acc)
        out_ref[0, w] = acc


def _bias_tiles(t5_bias):
    return pl.pallas_call(
        _bias_tile_kernel,
        grid=(HEADS,),
        in_specs=[pl.BlockSpec(memory_space=pltpu.SMEM)],
        out_specs=pl.BlockSpec((1, 2, MOBA_BLOCK, MOBA_BLOCK), lambda h: (h, 0, 0, 0)),
        out_shape=jax.ShapeDtypeStruct((HEADS, 2, MOBA_BLOCK, MOBA_BLOCK), F32),
        name="t5_bias_tiles",
    )(t5_bias)


def _proj_kernel(x_ref, w_ref, wqt_ref, wvt_ref, o_ref, f_ref, qt_ref, vt_ref, xb_ref):
    n = pl.program_id(1)

    def feature_major(wt_ref, t_ref):
        t = _nt_dot(wt_ref[...], xb_ref[...]).astype(BF16)
        for c in range(PROJ_TM // MOBA_BLOCK):
            t_ref[c] = t[:, c * MOBA_BLOCK:(c + 1) * MOBA_BLOCK]

    @pl.when(n == 0)
    def _():
        xb_ref[...] = x_ref[...].astype(BF16)
        feature_major(wqt_ref, qt_ref)

    acc = jnp.dot(xb_ref[...], w_ref[...], preferred_element_type=F32)
    o_ref[...] = acc.astype(BF16)

    @pl.when(n == 1)
    def _():
        feature_major(wvt_ref, vt_ref)

    @pl.when(n == COL_HF)
    def _():
        f_ref[...] = acc


def _proj(x2, w_in_b, w_qt_b, w_vt_b):
    tiles_per_seq = SEQ // PROJ_TM
    blocks_per_tile = PROJ_TM // MOBA_BLOCK
    t_spec = pl.BlockSpec((None, blocks_per_tile, WIDTH, MOBA_BLOCK),
                          lambda i, n: (i // tiles_per_seq, i % tiles_per_seq, 0, 0))
    t_shape = jax.ShapeDtypeStruct((BATCH, N_BLOCKS, WIDTH, MOBA_BLOCK), BF16)
    return pl.pallas_call(
        _proj_kernel,
        grid=(N_TOK // PROJ_TM, ROW_WIDTH // WIDTH),
        in_specs=[
            pl.BlockSpec((PROJ_TM, D_MODEL), lambda i, n: (i, 0)),
            pl.BlockSpec((D_MODEL, WIDTH), lambda i, n: (0, jnp.where(n == 0, 1, n + 2))),
            pl.BlockSpec((WIDTH, D_MODEL), lambda i, n: (0, 0)),
            pl.BlockSpec((WIDTH, D_MODEL), lambda i, n: (0, 0)),
        ],
        out_specs=[
            pl.BlockSpec((PROJ_TM, WIDTH), lambda i, n: (i, n)),
            pl.BlockSpec((PROJ_TM, WIDTH), lambda i, n: (i, 0)),
            t_spec,
            t_spec,
        ],
        out_shape=[
            jax.ShapeDtypeStruct((N_TOK, ROW_WIDTH), BF16),
            jax.ShapeDtypeStruct((N_TOK, WIDTH), F32),
            t_shape,
            t_shape,
        ],
        scratch_shapes=[pltpu.VMEM((PROJ_TM, D_MODEL), BF16)],
        compiler_params=pltpu.CompilerParams(
            dimension_semantics=("arbitrary", "arbitrary"), vmem_limit_bytes=VMEM_LIMIT),
        name="in_proj",
    )(x2, w_in_b, w_qt_b, w_vt_b)


def _colreduce(x, op):
    r = x[0:8]
    for i in range(1, x.shape[0] // 8):
        r = op(r, x[8 * i:8 * i + 8])
    for shift in (4, 2, 1):
        r = op(r, pltpu.roll(r, shift, 0))
    return r


def _attn_kernel(tab_ref, qt_ref, k_ref, vt_ref, bias_ref, o_ref,
                 kmh_ref, kml_ref, ext_ref, s_ref, near_ref):
    assert MOBA_BLOCK == 1 << 8
    pair = pl.program_id(1)
    step_i = pl.program_id(2)
    lane = lax.broadcasted_iota(jnp.int32, (MOBA_BLOCK, LANES), 1)

    @pl.when(step_i == 0)
    def _():
        r = lax.broadcasted_iota(jnp.int32, (LANES, SEQ), 1)
        nb = lax.broadcasted_iota(jnp.int32, (LANES, SEQ), 0)
        avg = jnp.where((r >> 8) == nb, 1.0 / MOBA_BLOCK, 0.0).astype(BF16)
        km = jnp.dot(avg, k_ref[...], preferred_element_type=F32)
        hi = km.astype(BF16)
        kmh_ref[...] = hi
        kml_ref[...] = (km - hi.astype(F32)).astype(BF16)
        kr = lax.broadcasted_iota(jnp.int32, (SEQ, LANES), 0) >> 8
        kl = lax.broadcasted_iota(jnp.int32, (SEQ, LANES), 1)
        ext_ref[...] = jnp.where((kl == kr) | (kl == BIAS_HI_LANE) | (kl == BIAS_LO_LANE),
                                 1.0, 0.0).astype(BF16)

    krow = lax.broadcasted_iota(jnp.int32, (MOBA_BLOCK, MOBA_BLOCK), 0)
    qcol = lax.broadcasted_iota(jnp.int32, (MOBA_BLOCK, MOBA_BLOCK), 1)
    frow = lax.broadcasted_iota(jnp.int32, (LANES, MOBA_BLOCK), 0)
    nrow = lax.broadcasted_iota(jnp.int32, (N_BLOCKS, MOBA_BLOCK), 0)
    nrow_f = nrow.astype(F32)
    brow = lax.broadcasted_iota(jnp.int32, (8, MOBA_BLOCK), 0)
    ones_rows = jnp.ones((BF16_SUBLANES, MOBA_BLOCK), BF16)
    zero_rows = jnp.zeros((LANES - N_BLOCKS - 8, MOBA_BLOCK), F32)
    km_hi = kmh_ref[0:N_BLOCKS, :]
    km_lo = kml_ref[0:N_BLOCKS, :]

    def v_aug(hh, block):
        vt = vt_ref[block, hh * HEAD_DIM:(hh + 1) * HEAD_DIM, :]
        return jnp.concatenate([vt, ones_rows], axis=0)

    def rows_to_tile(x8):
        return jnp.concatenate([x8] * (N_BLOCKS // 8), axis=0)

    q_far = []
    carry = []
    for t in range(Q_TILES):
        own = step_i * Q_TILES + t
        prev = jnp.maximum(own - 1, 0)
        qt2 = qt_ref[t]
        k_own = k_ref[pl.ds(pl.multiple_of(own * MOBA_BLOCK, MOBA_BLOCK), MOBA_BLOCK), :]
        k_prev = k_ref[pl.ds(pl.multiple_of(prev * MOBA_BLOCK, MOBA_BLOCK), MOBA_BLOCK), :]
        ext_prev = jnp.where(lane == own - 1, 1.0, 0.0).astype(BF16)
        k_prev_aug = jnp.concatenate([k_prev, ext_prev], axis=1)
        for hh in range(2):
            head = pair * 2 + hh
            in_head = (frow >= hh * HEAD_DIM) & (frow < (hh + 1) * HEAD_DIM)
            qs = jnp.where(in_head, qt2, jnp.zeros_like(qt2)) * jnp.asarray(HEAD_DIM ** -0.5, BF16)

            gate = (jnp.dot(km_hi, qs, preferred_element_type=F32)
                    + jnp.dot(km_lo, qs, preferred_element_type=F32))
            eligible = nrow < own
            g = jnp.where(eligible, gate, -jnp.inf)
            sel = jnp.zeros((N_BLOCKS, MOBA_BLOCK), jnp.bool_)
            for _ in range(MOBA_TOPK):
                mx = rows_to_tile(_colreduce(g, jnp.maximum))
                idx = rows_to_tile(_colreduce(jnp.where(g == mx, nrow_f, float(LANES)),
                                              jnp.minimum))
                hit = nrow_f == idx
                sel = sel | hit
                g = jnp.where(hit, -jnp.inf, g)
            sel = sel & eligible

            b31 = jnp.full((8, MOBA_BLOCK), tab_ref[N_BUCKETS - 1, head], F32)
            b31_hi = b31.astype(BF16).astype(F32)
            bias_rows = jnp.where(brow == 0, b31_hi, jnp.where(brow == 1, b31 - b31_hi, 0.0))

            def q_aug(selected):
                aug = jnp.concatenate([jnp.where(selected, 0.0, NEG), bias_rows, zero_rows], axis=0)
                return jnp.concatenate([qs, aug.astype(BF16)], axis=0)

            q_far.append(q_aug(sel & (nrow != own - 1)))

            s_prev = jnp.dot(k_prev_aug, q_aug(sel), preferred_element_type=F32) + bias_ref[hh, 1]
            near_ref[2 * t + hh, 0:MOBA_BLOCK, :] = jnp.where(own >= 1, s_prev, NEG)
            s_own = jnp.dot(k_own, qs, preferred_element_type=F32) + bias_ref[hh, 0]
            near_ref[2 * t + hh, MOBA_BLOCK:, :] = jnp.where(qcol >= krow, s_own, NEG)
            carry += [jnp.full((1, MOBA_BLOCK), NEG, F32),
                      jnp.zeros((HEAD_DIM + BF16_SUBLANES, MOBA_BLOCK), F32)]

    q_cat = jnp.concatenate(q_far, axis=1)
    n_tiles = 2 * Q_TILES

    def absorb_scores(s, va, m, acc):
        m_new = jnp.maximum(m, jnp.max(s, axis=0, keepdims=True))
        alpha = jnp.exp(m - m_new)
        p = jnp.exp(s - m_new).astype(BF16)
        return m_new, alpha * acc + jnp.dot(va, p, preferred_element_type=F32)

    def scores(step, slot):
        start = pl.multiple_of(jnp.minimum(step, N_FAR_STEPS - 1) * FAR_KEYS, FAR_KEYS)
        k_aug = jnp.concatenate([k_ref[pl.ds(start, FAR_KEYS), :],
                                 ext_ref[pl.ds(start, FAR_KEYS), :]], axis=1)
        s_ref[slot] = jnp.dot(k_aug, q_cat, preferred_element_type=F32)

    def absorb(step, slot, carry):
        block0 = step * FAR_BLOCKS
        out = []
        for j in range(n_tiles):
            hh = j % 2
            s = s_ref[slot, :, j * MOBA_BLOCK:(j + 1) * MOBA_BLOCK]
            va = jnp.concatenate([v_aug(hh, block0 + c) for c in range(FAR_BLOCKS)], axis=1)
            out += absorb_scores(s, va, carry[2 * j], carry[2 * j + 1])
        return tuple(out)

    n_steps = step_i
    scores(0, 0)

    def two_steps(base, carry):
        scores(base + 1, 1)
        carry = absorb(base, 0, carry)
        scores(base + 2, 0)
        return absorb(base + 1, 1, carry)

    def far_body(i, carry):
        return two_steps(4 * i + 2, two_steps(4 * i, carry))

    n_quads = n_steps // 4
    carry = lax.fori_loop(0, n_quads, far_body, tuple(carry))
    carry = lax.cond(n_steps % 4 >= 2,
                     lambda c: two_steps(4 * n_quads, c), lambda c: c, carry)
    carry = lax.cond(n_steps % 2 == 1,
                     lambda c: absorb(n_steps - 1, 0, c), lambda c: c, carry)

    for t in range(Q_TILES):
        own = step_i * Q_TILES + t
        outs = []
        for hh in range(2):
            j = 2 * t + hh
            va = jnp.concatenate([v_aug(hh, jnp.maximum(own - 1, 0)), v_aug(hh, own)], axis=1)
            _, acc = absorb_scores(near_ref[j], va, carry[2 * j], carry[2 * j + 1])
            outs.append(acc[0:HEAD_DIM] / acc[HEAD_DIM:HEAD_DIM + 1])
        o_ref[t * MOBA_BLOCK:(t + 1) * MOBA_BLOCK, :] = (
            jnp.concatenate(outs, axis=0).T.astype(BF16))


def _attention(t5_bias, qt3, proj3, vt3, bias_tiles):
    n_pairs = HEADS // 2
    assert FAR_BLOCKS == Q_TILES
    return pl.pallas_call(
        _attn_kernel,
        grid=(BATCH, n_pairs, N_BLOCKS // Q_TILES),
        in_specs=[
            pl.BlockSpec(memory_space=pltpu.SMEM),
            pl.BlockSpec((None, Q_TILES, LANES, MOBA_BLOCK), lambda b, p, i: (b, i, p, 0)),
            pl.BlockSpec((None, SEQ, LANES), lambda b, p, i: (b, 0, p)),
            pl.BlockSpec((None, N_BLOCKS, LANES, MOBA_BLOCK), lambda b, p, i: (b, 0, p, 0)),
            pl.BlockSpec((2, 2, MOBA_BLOCK, MOBA_BLOCK), lambda b, p, i: (p, 0, 0, 0)),
        ],
        out_specs=pl.BlockSpec((None, Q_TILES * MOBA_BLOCK, LANES), lambda b, p, i: (b, i, p)),
        out_shape=jax.ShapeDtypeStruct((BATCH, SEQ, WIDTH), BF16),
        scratch_shapes=[
            pltpu.VMEM((LANES, LANES), BF16),
            pltpu.VMEM((LANES, LANES), BF16),
            pltpu.VMEM((SEQ, LANES), BF16),
            pltpu.VMEM((2, FAR_KEYS, 2 * Q_TILES * MOBA_BLOCK), F32),
            pltpu.VMEM((2 * Q_TILES, 2 * MOBA_BLOCK, MOBA_BLOCK), F32),
        ],
        compiler_params=pltpu.CompilerParams(
            dimension_semantics=("arbitrary", "arbitrary", "arbitrary"),
            vmem_limit_bytes=VMEM_LIMIT),
        name="moba_attention",
    )(t5_bias, qt3, proj3, vt3, bias_tiles)


def _hgrn_kernel(q_ref, f_ref, i_ref, g_ref, lbl_ref, nw_ref, o_ref,
                 state_ref, bpad_ref, kpad_ref, vpad_ref, shb_ref, shk_ref, shv_ref,
                 ones_ref, mask_ref):
    C = HGRN_CHUNK
    half = WIDTH // 2
    assert HEAD_DIM == 1 << 6

    @pl.when(pl.program_id(1) == 0)
    def _():
        state_ref[...] = jnp.zeros_like(state_ref)
        sr = lax.broadcasted_iota(jnp.int32, (half, half), 0) >> 6
        sc = lax.broadcasted_iota(jnp.int32, (half, half), 1) >> 6
        same = jnp.where(sr == sc, 1.0, 0.0)
        ones_ref[...] = same.astype(BF16)
        mask_ref[...] = same

    zpad = jnp.zeros((8, WIDTH), F32)
    bpad_ref[0:8, :] = zpad
    kpad_ref[0:8, :] = zpad
    vpad_ref[0:8, :] = zpad

    l0 = lbl_ref[0:1, :]
    l1 = lbl_ref[1:2, :]
    lmx = jnp.maximum(l0, l1)
    e0 = jnp.exp(l0 - lmx)
    lb = e0 / (e0 + jnp.exp(l1 - lmx))
    nw = nw_ref[...]

    ri = lax.broadcasted_iota(jnp.int32, (C, C), 0)
    ci = lax.broadcasted_iota(jnp.int32, (C, C), 1)
    tril = jnp.where(ri >= ci, 1.0, 0.0).astype(BF16)

    def head_sum(t):
        return jnp.concatenate(
            [jnp.dot(t[:, 0:half], ones_ref[...], preferred_element_type=F32),
             jnp.dot(t[:, half:], ones_ref[...], preferred_element_type=F32)], axis=1)

    def chunk(c, _):
        r0 = pl.multiple_of(c * C, C)
        qq = q_ref[pl.ds(r0, C), :].astype(F32)
        qq = qq * _sigmoid(qq)
        f = lb + (1.0 - lb) * _sigmoid(f_ref[pl.ds(r0, C), :])
        gl = jnp.log(f)
        kk = 1.0 - f
        vv = i_ref[pl.ds(r0, C), :].astype(F32)
        g_hi = gl.astype(BF16)
        g_r = gl - g_hi.astype(F32)
        g_mid = g_r.astype(BF16)
        g_lo = (g_r - g_mid.astype(F32)).astype(BF16)
        b = (jnp.dot(tril, g_hi, preferred_element_type=F32)
             + jnp.dot(tril, g_mid, preferred_element_type=F32)
             + jnp.dot(tril, g_lo, preferred_element_type=F32))
        b = b * LOG2E

        bpad_ref[8:8 + C, :] = b
        kpad_ref[8:8 + C, :] = kk
        vpad_ref[8:8 + C, :] = vv
        for dd in range(8):
            shb_ref[dd] = bpad_ref[8 - dd:8 - dd + C, :]
            shk_ref[dd] = kpad_ref[8 - dd:8 - dd + C, :]
            shv_ref[dd] = vpad_ref[8 - dd:8 - dd + C, :]

        o_parts = []
        for j in range(C // 8):
            rows = C - 8 * j
            bt = b[8 * j:, :]
            qt = qq[8 * j:, :]
            terms = []
            vs_all = []
            for dd in range(8):
                terms.append(qt * shk_ref[dd, 0:rows, :] * jnp.exp2(bt - shb_ref[dd, 0:rows, :]))
                vs_all.append(shv_ref[dd, 0:rows, :])
            a = head_sum(jnp.concatenate(terms, axis=0).astype(BF16))
            av = a * jnp.concatenate(vs_all, axis=0)
            part = av[0:rows]
            for dd in range(1, 8):
                part = part + av[dd * rows:(dd + 1) * rows]
            o_parts.append(part)
        o = o_parts[0]
        for j in range(1, C // 8):
            o = o + jnp.concatenate([jnp.zeros((8 * j, WIDTH), F32), o_parts[j]], axis=0)

        qe = (qq * jnp.exp2(b)).astype(BF16)
        b_last = b[C - 1:C, :]
        kd = kk * jnp.exp2(b_last - b)
        dec = jnp.exp2(b_last)
        inter = []
        for a_i in range(2):
            sl = slice(a_i * half, (a_i + 1) * half)
            st = state_ref[a_i]
            inter.append(_nt_dot(qe[:, sl], st.astype(BF16)))
            upd = _tn_dot(vv[:, sl].astype(BF16), kd[:, sl].astype(BF16))
            state_ref[a_i] = st * dec[:, sl] + upd * mask_ref[...]
        o = o + jnp.concatenate(inter, axis=1)

        ms = head_sum((o * o).astype(BF16)) * (1.0 / HEAD_DIM)
        r = o * lax.rsqrt(ms + RMS_EPS) * nw
        r = r * _sigmoid(g_ref[pl.ds(r0, C), :].astype(F32))
        o_ref[pl.ds(r0, C), :] = r.astype(BF16)
        return 0

    lax.fori_loop(0, HGRN_ROWS // C, chunk, 0, unroll=4)


def _hgrn(proj3, f3, lb_logits, norm_w):
    blk = lambda col: pl.BlockSpec((None, HGRN_ROWS, WIDTH), lambda b, i, col=col: (b, i, col))
    return pl.pallas_call(
        _hgrn_kernel,
        grid=(BATCH, SEQ // HGRN_ROWS),
        in_specs=[
            blk(COL_HQ),
            pl.BlockSpec((None, HGRN_ROWS, WIDTH), lambda b, i: (b, i, 0)),
            blk(COL_HI),
            blk(COL_HG),
            pl.BlockSpec((2, WIDTH), lambda b, i: (0, 0)),
            pl.BlockSpec((1, WIDTH), lambda b, i: (0, 0)),
        ],
        out_specs=pl.BlockSpec((None, HGRN_ROWS, WIDTH), lambda b, i: (b, i, 0)),
        out_shape=jax.ShapeDtypeStruct((BATCH, SEQ, WIDTH), BF16),
        scratch_shapes=[
            pltpu.VMEM((2, WIDTH // 2, WIDTH // 2), F32),
            pltpu.VMEM((8 + HGRN_CHUNK, WIDTH), F32),
            pltpu.VMEM((8 + HGRN_CHUNK, WIDTH), F32),
            pltpu.VMEM((8 + HGRN_CHUNK, WIDTH), F32),
            pltpu.VMEM((8, HGRN_CHUNK, WIDTH), F32),
            pltpu.VMEM((8, HGRN_CHUNK, WIDTH), F32),
            pltpu.VMEM((8, HGRN_CHUNK, WIDTH), F32),
            pltpu.VMEM((WIDTH // 2, WIDTH // 2), BF16),
            pltpu.VMEM((WIDTH // 2, WIDTH // 2), F32),
        ],
        compiler_params=pltpu.CompilerParams(
            dimension_semantics=("arbitrary", "arbitrary"), vmem_limit_bytes=VMEM_LIMIT),
        name="hgrn2",
    )(proj3, f3, proj3, proj3, lb_logits, norm_w)


def _out_kernel(a_ref, r_ref, x_ref, wo_ref, g_ref, b_ref, wr_ref, br_ref,
                x1_ref, comb_ref):
    mix = jnp.dot(a_ref[...], wo_ref[0:WIDTH, :], preferred_element_type=F32)
    mix = mix + jnp.dot(r_ref[...], wo_ref[WIDTH:, :], preferred_element_type=F32)
    x1 = _layer_norm(ALPHA * x_ref[...] + mix, g_ref[...], b_ref[...])
    x1_ref[...] = x1

    x_hi = x1.astype(BF16)
    x_lo = (x1 - x_hi.astype(F32)).astype(BF16)
    wr = wr_ref[...]
    w_hi = wr.astype(BF16)
    w_lo = (wr - w_hi.astype(F32)).astype(BF16)
    logits = (jnp.dot(x_hi, w_hi, preferred_element_type=F32)
              + jnp.dot(x_lo, w_hi, preferred_element_type=F32)
              + jnp.dot(x_hi, w_lo, preferred_element_type=F32)) + br_ref[...]
    assert EXPERTS_PER_GROUP == 1 << 2
    lane_i = lax.broadcasted_iota(jnp.int32, logits.shape, 1)
    lane = lane_i.astype(F32)
    grp_of_lane = (lane_i >> 2).astype(F32)
    none = float(LANES)
    is_g = (lane_i >= N_EXPERTS) & (lane_i < N_EXPERTS + N_GROUPS)
    gl = jnp.where(is_g, logits, -jnp.inf)
    ge = jnp.exp(gl - jnp.max(gl, axis=-1, keepdims=True))
    gp = ge / jnp.sum(ge, axis=-1, keepdims=True)
    g_w = jnp.max(gp, axis=-1, keepdims=True)
    g_lane = jnp.min(jnp.where(is_g & (gp == g_w), lane, none), axis=-1, keepdims=True)
    g_idx = g_lane - float(N_EXPERTS)

    in_grp = (lane_i < N_EXPERTS) & (grp_of_lane == g_idx)
    el = jnp.where(in_grp, logits, -jnp.inf)
    ee = jnp.exp(el - jnp.max(el, axis=-1, keepdims=True))
    ep = ee / jnp.sum(ee, axis=-1, keepdims=True)
    p1 = jnp.max(ep, axis=-1, keepdims=True)
    i1 = jnp.min(jnp.where(in_grp & (ep == p1), lane, none), axis=-1, keepdims=True)
    rest = in_grp & (lane != i1)
    ep2 = jnp.where(rest, ep, -1.0)
    p2 = jnp.max(ep2, axis=-1, keepdims=True)
    i2 = jnp.min(jnp.where(rest & (ep2 == p2), lane, none), axis=-1, keepdims=True)
    den = p1 + p2
    comb = jnp.where(lane == i1, g_w * (p1 / den), 0.0)
    comb = jnp.where(lane == i2, g_w * (p2 / den), comb)
    comb_ref[...] = jnp.where(lane_i == GROUP_LANE, g_idx, comb)


def _out_proj(a2, r2, x2, wo_b, g1, b1, w_route, b_route):
    row = lambda w: pl.BlockSpec((OUT_TM, w), lambda i: (i, 0))
    full = lambda s: pl.BlockSpec(s, lambda i: (0, 0))
    return pl.pallas_call(
        _out_kernel,
        grid=(N_TOK // OUT_TM,),
        in_specs=[row(WIDTH), row(WIDTH), row(D_MODEL), full((D_MODEL, D_MODEL)),
                  full((1, D_MODEL)), full((1, D_MODEL)), full((D_MODEL, LANES)), full((1, LANES))],
        out_specs=[row(D_MODEL), row(LANES)],
        out_shape=[jax.ShapeDtypeStruct((N_TOK, D_MODEL), F32),
                   jax.ShapeDtypeStruct((N_TOK, LANES), F32)],
        compiler_params=pltpu.CompilerParams(
            dimension_semantics=("arbitrary",), vmem_limit_bytes=VMEM_LIMIT),
        name="out_proj_ln_route",
    )(a2, r2, x2, wo_b, g1, b1, w_route, b_route)


def _moe_kernel(x1_ref, comb_ref, wg_ref, wu_ref, wd_ref, g_ref, b_ref, o_ref,
                pt_ref, xs_ref, cs_ref, acc_ref, win_ref):
    e = pl.program_id(1)
    assert MOE_ALIGN == 1 << 7 and EXPERTS_PER_GROUP == 1 << 2

    @pl.when(e == 0)
    def _():
        comb = comb_ref[...]
        lane = lax.broadcasted_iota(jnp.int32, comb.shape, 1)
        lane_f = lane.astype(F32)
        gid = jnp.sum(jnp.where(lane == GROUP_LANE, comb, 0.0), axis=-1, keepdims=True)
        mine = lane_f == gid
        ti = lax.broadcasted_iota(jnp.int32, (MOE_TM, MOE_TM), 0)
        tj = lax.broadcasted_iota(jnp.int32, (MOE_TM, MOE_TM), 1)
        tril = jnp.where(ti >= tj, 1.0, 0.0).astype(BF16)
        incl = jnp.dot(tril, jnp.where(mine, 1.0, 0.0).astype(BF16),
                       preferred_element_type=F32)
        rank = jnp.sum(jnp.where(mine, incl, 0.0), axis=-1, keepdims=True) - 1.0
        counts = incl[MOE_TM - 1:MOE_TM, :]
        lane_row = lane[0:1, :]
        start = jnp.int32(0)
        base = jnp.zeros_like(gid)
        for g in range(N_GROUPS):
            cnt = jnp.sum(jnp.where(lane_row == g, counts, 0.0)).astype(jnp.int32)
            chunks = (cnt + (MOE_ALIGN - 1)) >> 7
            win_ref[0, g] = start
            win_ref[1, g] = chunks >> 1
            win_ref[2, g] = chunks & 1
            base = jnp.where(gid == float(g), start.astype(F32), base)
            start = start + chunks * MOE_ALIGN
        pos = base + rank
        r_iota = lax.broadcasted_iota(jnp.int32, (MOE_TM, MOE_ROWS), 1).astype(F32)
        pt = jnp.where(r_iota == pos, 1.0, 0.0).astype(BF16)
        pt_ref[...] = pt
        xs_ref[...] = _tn_dot(pt, x1_ref[...].astype(BF16)).astype(BF16)
        c_hi = comb.astype(BF16)
        c_lo = (comb - c_hi.astype(F32)).astype(BF16)
        cs_ref[...] = _tn_dot(pt, c_hi) + _tn_dot(pt, c_lo)
        acc_ref[...] = jnp.zeros_like(acc_ref)

    g = e >> 2
    row0 = win_ref[0, g]
    n_full = win_ref[1, g]

    def window(start, rows):
        r0 = pl.multiple_of(start, MOE_ALIGN)
        xw = xs_ref[pl.ds(r0, rows), :]
        hg = jnp.dot(xw, wg_ref[...], preferred_element_type=F32)
        hu = jnp.dot(xw, wu_ref[...], preferred_element_type=F32)
        h = (hg * _sigmoid(hg) * hu).astype(BF16)
        y = jnp.dot(h, wd_ref[...], preferred_element_type=F32)
        lane_w = lax.broadcasted_iota(jnp.int32, (rows, LANES), 1)
        c = jnp.sum(jnp.where(lane_w == e, cs_ref[pl.ds(r0, rows), :], 0.0),
                    axis=-1, keepdims=True)
        acc_ref[pl.ds(r0, rows), :] += c * y

    def full_window(k, _):
        window(row0 + k * MOE_WIN, MOE_WIN)
        return 0

    lax.fori_loop(0, n_full, full_window, 0)

    @pl.when(win_ref[2, g] == 1)
    def _():
        window(row0 + n_full * MOE_WIN, MOE_ALIGN)

    @pl.when(e == N_EXPERTS - 1)
    def _():
        moe = jnp.dot(pt_ref[...], acc_ref[...].astype(BF16), preferred_element_type=F32)
        o_ref[...] = _layer_norm(ALPHA * x1_ref[...] + moe, g_ref[...], b_ref[...])


def _moe(x1, comb, wg_b, wu_b, wd_b, g2, b2):
    return pl.pallas_call(
        _moe_kernel,
        grid=(N_TOK // MOE_TM, N_EXPERTS),
        in_specs=[
            pl.BlockSpec((MOE_TM, D_MODEL), lambda i, e: (i, 0)),
            pl.BlockSpec((MOE_TM, LANES), lambda i, e: (i, 0)),
            pl.BlockSpec((None, D_MODEL, D_EXPERT), lambda i, e: (e, 0, 0)),
            pl.BlockSpec((None, D_MODEL, D_EXPERT), lambda i, e: (e, 0, 0)),
            pl.BlockSpec((None, D_EXPERT, D_MODEL), lambda i, e: (e, 0, 0)),
            pl.BlockSpec((1, D_MODEL), lambda i, e: (0, 0)),
            pl.BlockSpec((1, D_MODEL), lambda i, e: (0, 0)),
        ],
        out_specs=pl.BlockSpec((MOE_TM, D_MODEL), lambda i, e: (i, 0)),
        out_shape=jax.ShapeDtypeStruct((N_TOK, D_MODEL), F32),
        scratch_shapes=[
            pltpu.VMEM((MOE_TM, MOE_ROWS), BF16),
            pltpu.VMEM((MOE_ROWS, D_MODEL), BF16),
            pltpu.VMEM((MOE_ROWS, LANES), F32),
            pltpu.VMEM((MOE_ROWS, D_MODEL), F32),
            pltpu.SMEM((3, N_GROUPS), jnp.int32),
        ],
        compiler_params=pltpu.CompilerParams(
            dimension_semantics=("arbitrary", "arbitrary"), vmem_limit_bytes=VMEM_LIMIT),
        name="moe_experts_ln",
    )(x1, comb, wg_b, wu_b, wd_b, g2, b2)


def kernel(x, w_in, t5_bias, hgrn_lb_logits, hgrn_norm_w, w_o, ln1_g, ln1_b, w_group, b_group,
           w_expert, b_expert, w_gate, w_up, w_down, ln2_g, ln2_b):
    x2 = x.reshape(N_TOK, D_MODEL)
    w_in_b = w_in[0].astype(BF16)
    w_qt_b = w_in_b[:, 0:WIDTH].T
    w_vt_b = w_in_b[:, 2 * WIDTH:3 * WIDTH].T
    proj, f_logits, qt3, vt3 = _proj(x2, w_in_b, w_qt_b, w_vt_b)
    proj3 = proj.reshape(BATCH, SEQ, ROW_WIDTH)

    a = _attention(t5_bias, qt3, proj3, vt3, _bias_tiles(t5_bias))
    r = _hgrn(proj3, f_logits.reshape(BATCH, SEQ, WIDTH), hgrn_lb_logits, hgrn_norm_w)

    pad = LANES - N_EXPERTS - N_GROUPS
    w_route = jnp.concatenate(
        [w_expert[0].transpose(1, 0, 2).reshape(D_MODEL, N_EXPERTS), w_group[0],
         jnp.zeros((D_MODEL, pad), F32)], axis=1)
    b_route = jnp.concatenate(
        [b_expert[0].reshape(N_EXPERTS), b_group[0], jnp.zeros((pad,), F32)]).reshape(1, LANES)
    x1, comb = _out_proj(
        a.reshape(N_TOK, WIDTH), r.reshape(N_TOK, WIDTH), x2, w_o[0].astype(BF16),
        ln1_g, ln1_b, w_route, b_route)

    out = _moe(x1, comb, w_gate[0].astype(BF16), w_up[0].astype(BF16),
               w_down[0].astype(BF16), ln2_g, ln2_b)
    return out.reshape(BATCH, SEQ, D_MODEL)
```

```python
import functools
import math

import jax
import jax.numpy as jnp
from jax import lax
from jax.experimental import pallas as pl
from jax.experimental.pallas import tpu as pltpu

F32 = jnp.float32
BF16 = jnp.bfloat16

D_MODEL = 1024
BATCH = 2
SEQ = 8192
N_TOK = BATCH * SEQ
HEADS = 8
HEAD_DIM = 64
WIDTH = HEADS * HEAD_DIM
IN_WIDTH = 7 * WIDTH
ROW_WIDTH = 5 * WIDTH
COL_HQ, COL_HF, COL_HI, COL_HG = 1, 2, 3, 4
MOBA_BLOCK = 256
N_BLOCKS = SEQ // MOBA_BLOCK
MOBA_TOPK = 3
HGRN_CHUNK = 32
N_BUCKETS = 32
MAX_DISTANCE = 128
N_GROUPS = 4
EXPERTS_PER_GROUP = 4
N_EXPERTS = 16
D_EXPERT = 512
LN_EPS = 1e-5
RMS_EPS = 1e-6
ALPHA = 2.0 ** 0.25
LOG2E = math.log2(math.e)

LANES = 128
BF16_SUBLANES = 16
NEG = -1e30
VMEM_LIMIT = 48 * 1024 * 1024

PROJ_TM = 1024
OUT_TM = 512
MOE_TM = 1024
MOE_ALIGN = 128
MOE_WIN = 256
MOE_ROWS = MOE_TM + N_GROUPS * MOE_ALIGN
GROUP_LANE = N_EXPERTS
HGRN_ROWS = 256
Q_TILES = 2
FAR_BLOCKS = 2
FAR_KEYS = FAR_BLOCKS * MOBA_BLOCK
N_FAR_STEPS = N_BLOCKS // FAR_BLOCKS

BIAS_HI_LANE = N_BLOCKS
BIAS_LO_LANE = N_BLOCKS + 1


def _nt_dot(a, b):
    return lax.dot_general(a, b, (((1,), (1,)), ((), ())), preferred_element_type=F32)


def _tn_dot(a, b):
    return lax.dot_general(a, b, (((0,), (0,)), ((), ())), preferred_element_type=F32)


def _sigmoid(x):
    return 1.0 / (1.0 + jnp.exp2(x * (-LOG2E)))


def _layer_norm(y, g, b):
    mu = jnp.mean(y, axis=-1, keepdims=True)
    d = y - mu
    var = jnp.mean(d * d, axis=-1, keepdims=True)
    return d * lax.rsqrt(var + LN_EPS) * g + b


def _bias_tile_kernel(tab_ref, out_ref):
    h = pl.program_id(0)
    ki = lax.broadcasted_iota(jnp.int32, (MOBA_BLOCK, MOBA_BLOCK), 0)
    qi = lax.broadcasted_iota(jnp.int32, (MOBA_BLOCK, MOBA_BLOCK), 1)
    max_exact = N_BUCKETS // 2
    for w in range(2):
        n = jnp.maximum(qi - ki + MOBA_BLOCK * w, 0)
        nf = jnp.maximum(n, 1).astype(F32)
        scaled = (jnp.log(nf / max_exact) / math.log(MAX_DISTANCE / max_exact)
                  * (N_BUCKETS - max_exact))
        large = max_exact + jnp.floor(jnp.maximum(scaled, 0.0)).astype(jnp.int32)
        large = jnp.minimum(large, N_BUCKETS - 1)
        bucket = jnp.where(n < max_exact, n, large)
        acc = jnp.zeros((MOBA_BLOCK, MOBA_BLOCK), F32)
        for bk in range(N_BUCKETS):
            acc = jnp.where(bucket == bk, tab_ref[bk, h], acc)
        out_ref[0, w] = acc


def _bias_tiles(t5_bias):
    return pl.pallas_call(
        _bias_tile_kernel,
        grid=(HEADS,),
        in_specs=[pl.BlockSpec(memory_space=pltpu.SMEM)],
        out_specs=pl.BlockSpec((1, 2, MOBA_BLOCK, MOBA_BLOCK), lambda h: (h, 0, 0, 0)),
        out_shape=jax.ShapeDtypeStruct((HEADS, 2, MOBA_BLOCK, MOBA_BLOCK), F32),
        name="t5_bias_tiles",
    )(t5_bias)


def _proj_kernel(x_ref, w_ref, wqt_ref, wvt_ref, o_ref, f_ref, qt_ref, vt_ref, xb_ref):
    n = pl.program_id(1)

    def feature_major(wt_ref, t_ref):
        t = _nt_dot(wt_ref[...], xb_ref[...]).astype(BF16)
        for c in range(PROJ_TM // MOBA_BLOCK):
            t_ref[c] = t[:, c * MOBA_BLOCK:(c + 1) * MOBA_BLOCK]

    @pl.when(n == 0)
    def _():
        xb_ref[...] = x_ref[...].astype(BF16)
        feature_major(wqt_ref, qt_ref)

    acc = jnp.dot(xb_ref[...], w_ref[...], preferred_element_type=F32)
    o_ref[...] = acc.astype(BF16)

    @pl.when(n == 1)
    def _():
        feature_major(wvt_ref, vt_ref)

    @pl.when(n == COL_HF)
    def _():
        f_ref[...] = acc


def _proj(x2, w_in_b, w_qt_b, w_vt_b):
    tiles_per_seq = SEQ // PROJ_TM
    blocks_per_tile = PROJ_TM // MOBA_BLOCK
    t_spec = pl.BlockSpec((None, blocks_per_tile, WIDTH, MOBA_BLOCK),
                          lambda i, n: (i // tiles_per_seq, i % tiles_per_seq, 0, 0))
    t_shape = jax.ShapeDtypeStruct((BATCH, N_BLOCKS, WIDTH, MOBA_BLOCK), BF16)
    return pl.pallas_call(
        _proj_kernel,
        grid=(N_TOK // PROJ_TM, ROW_WIDTH // WIDTH),
        in_specs=[
            pl.BlockSpec((PROJ_TM, D_MODEL), lambda i, n: (i, 0)),
            pl.BlockSpec((D_MODEL, WIDTH), lambda i, n: (0, jnp.where(n == 0, 1, n + 2))),
            pl.BlockSpec((WIDTH, D_MODEL), lambda i, n: (0, 0)),
            pl.BlockSpec((WIDTH, D_MODEL), lambda i, n: (0, 0)),
        ],
        out_specs=[
            pl.BlockSpec((PROJ_TM, WIDTH), lambda i, n: (i, n)),
            pl.BlockSpec((PROJ_TM, WIDTH), lambda i, n: (i, 0)),
            t_spec,
            t_spec,
        ],
        out_shape=[
            jax.ShapeDtypeStruct((N_TOK, ROW_WIDTH), BF16),
            jax.ShapeDtypeStruct((N_TOK, WIDTH), F32),
            t_shape,
            t_shape,
        ],
        scratch_shapes=[pltpu.VMEM((PROJ_TM, D_MODEL), BF16)],
        compiler_params=pltpu.CompilerParams(
            dimension_semantics=("arbitrary", "arbitrary"), vmem_limit_bytes=VMEM_LIMIT),
        name="in_proj",
    )(x2, w_in_b, w_qt_b, w_vt_b)


def _colreduce(x, op):
    r = x[0:8]
    for i in range(1, x.shape[0] // 8):
        r = op(r, x[8 * i:8 * i + 8])
    for shift in (4, 2, 1):
        r = op(r, pltpu.roll(r, shift, 0))
    return r


def _attn_kernel(tab_ref, qt_ref, k_ref, vt_ref, bias_ref, o_ref,
                 kmh_ref, kml_ref, ext_ref, s_ref, near_ref):
    assert MOBA_BLOCK == 1 << 8
    pair = pl.program_id(1)
    step_i = pl.program_id(2)
    lane = lax.broadcasted_iota(jnp.int32, (MOBA_BLOCK, LANES), 1)

    @pl.when(step_i == 0)
    def _():
        r = lax.broadcasted_iota(jnp.int32, (LANES, SEQ), 1)
        nb = lax.broadcasted_iota(jnp.int32, (LANES, SEQ), 0)
        avg = jnp.where((r >> 8) == nb, 1.0 / MOBA_BLOCK, 0.0).astype(BF16)
        km = jnp.dot(avg, k_ref[...], preferred_element_type=F32)
        hi = km.astype(BF16)
        kmh_ref[...] = hi
        kml_ref[...] = (km - hi.astype(F32)).astype(BF16)
        kr = lax.broadcasted_iota(jnp.int32, (SEQ, LANES), 0) >> 8
        kl = lax.broadcasted_iota(jnp.int32, (SEQ, LANES), 1)
        ext_ref[...] = jnp.where((kl == kr) | (kl == BIAS_HI_LANE) | (kl == BIAS_LO_LANE),
                                 1.0, 0.0).astype(BF16)

    krow = lax.broadcasted_iota(jnp.int32, (MOBA_BLOCK, MOBA_BLOCK), 0)
    qcol = lax.broadcasted_iota(jnp.int32, (MOBA_BLOCK, MOBA_BLOCK), 1)
    frow = lax.broadcasted_iota(jnp.int32, (LANES, MOBA_BLOCK), 0)
    nrow = lax.broadcasted_iota(jnp.int32, (N_BLOCKS, MOBA_BLOCK), 0)
    nrow_f = nrow.astype(F32)
    brow = lax.broadcasted_iota(jnp.int32, (8, MOBA_BLOCK), 0)
    ones_rows = jnp.ones((BF16_SUBLANES, MOBA_BLOCK), BF16)
    zero_rows = jnp.zeros((LANES - N_BLOCKS - 8, MOBA_BLOCK), F32)
    km_hi = kmh_ref[0:N_BLOCKS, :]
    km_lo = kml_ref[0:N_BLOCKS, :]

    def v_aug(hh, block):
        vt = vt_ref[block, hh * HEAD_DIM:(hh + 1) * HEAD_DIM, :]
        return jnp.concatenate([vt, ones_rows], axis=0)

    def rows_to_tile(x8):
        return jnp.concatenate([x8] * (N_BLOCKS // 8), axis=0)

    q_far = []
    prev_masks = []
    carry = []
    for t in range(Q_TILES):
        own = step_i * Q_TILES + t
        prev = jnp.maximum(own - 1, 0)
        qt2 = qt_ref[t]
        k_own = k_ref[pl.ds(pl.multiple_of(own * MOBA_BLOCK, MOBA_BLOCK), MOBA_BLOCK), :]
        k_prev = k_ref[pl.ds(pl.multiple_of(prev * MOBA_BLOCK, MOBA_BLOCK), MOBA_BLOCK), :]
        for hh in range(2):
            head = pair * 2 + hh
            in_head = (frow >= hh * HEAD_DIM) & (frow < (hh + 1) * HEAD_DIM)
            qs = jnp.where(in_head, qt2, jnp.zeros_like(qt2)) * jnp.asarray(HEAD_DIM ** -0.5, BF16)

            gate = (jnp.dot(km_hi, qs, preferred_element_type=F32)
                    + jnp.dot(km_lo, qs, preferred_element_type=F32))
            eligible = nrow < own
            g = jnp.where(eligible, gate, -jnp.inf)
            sel = jnp.zeros((N_BLOCKS, MOBA_BLOCK), jnp.bool_)
            for _ in range(MOBA_TOPK):
                mx = rows_to_tile(_colreduce(g, jnp.maximum))
                idx = rows_to_tile(_colreduce(jnp.where(g == mx, nrow_f, float(LANES)),
                                              jnp.minimum))
                hit = nrow_f == idx
                sel = sel | hit
                g = jnp.where(hit, -jnp.inf, g)
            sel = sel & eligible

            b31 = jnp.full((8, MOBA_BLOCK), tab_ref[N_BUCKETS - 1, head], F32)
            b31_hi = b31.astype(BF16).astype(F32)
            bias_rows = jnp.where(brow == 0, b31_hi, jnp.where(brow == 1, b31 - b31_hi, 0.0))

            aug = jnp.concatenate([jnp.where(sel & (nrow != own - 1), 0.0, NEG), bias_rows,
                                   zero_rows], axis=0)
            q_far.append(jnp.concatenate([qs, aug.astype(BF16)], axis=0))

            prev_sel = _colreduce(jnp.where((nrow == own - 1) & jnp.logical_not(sel), NEG, 0.0),
                                  jnp.add)
            prev_masks.append(jnp.where(own >= 1, prev_sel[0:1], NEG))
            near_ref[2 * t + hh, 0:MOBA_BLOCK, :] = (
                jnp.dot(k_prev, qs, preferred_element_type=F32) + bias_ref[hh, 1])
            s_own = jnp.dot(k_own, qs, preferred_element_type=F32) + bias_ref[hh, 0]
            near_ref[2 * t + hh, MOBA_BLOCK:, :] = jnp.where(qcol >= krow, s_own, NEG)
            carry += [jnp.full((1, MOBA_BLOCK), NEG, F32),
                      jnp.zeros((HEAD_DIM + BF16_SUBLANES, MOBA_BLOCK), F32)]

    q_cat = jnp.concatenate(q_far, axis=1)
    n_tiles = 2 * Q_TILES

    def absorb_scores(s, va, m, acc):
        m_new = jnp.maximum(m, jnp.max(s, axis=0, keepdims=True))
        alpha = jnp.exp(m - m_new)
        p = jnp.exp(s - m_new).astype(BF16)
        return m_new, alpha * acc + jnp.dot(va, p, preferred_element_type=F32)

    def scores(step, slot):
        start = pl.multiple_of(jnp.minimum(step, N_FAR_STEPS - 1) * FAR_KEYS, FAR_KEYS)
        k_aug = jnp.concatenate([k_ref[pl.ds(start, FAR_KEYS), :],
                                 ext_ref[pl.ds(start, FAR_KEYS), :]], axis=1)
        s_ref[slot] = jnp.dot(k_aug, q_cat, preferred_element_type=F32)

    def absorb(step, slot, carry):
        block0 = step * FAR_BLOCKS
        out = []
        for j in range(n_tiles):
            hh = j % 2
            s = s_ref[slot, :, j * MOBA_BLOCK:(j + 1) * MOBA_BLOCK]
            va = jnp.concatenate([v_aug(hh, block0 + c) for c in range(FAR_BLOCKS)], axis=1)
            out += absorb_scores(s, va, carry[2 * j], carry[2 * j + 1])
        return tuple(out)

    n_steps = step_i
    scores(0, 0)

    def two_steps(base, carry):
        scores(base + 1, 1)
        carry = absorb(base, 0, carry)
        scores(base + 2, 0)
        return absorb(base + 1, 1, carry)

    def far_body(i, carry):
        return two_steps(4 * i + 2, two_steps(4 * i, carry))

    n_quads = n_steps // 4
    carry = lax.fori_loop(0, n_quads, far_body, tuple(carry))
    carry = lax.cond(n_steps % 4 >= 2,
                     lambda c: two_steps(4 * n_quads, c), lambda c: c, carry)
    carry = lax.cond(n_steps % 2 == 1,
                     lambda c: absorb(n_steps - 1, 0, c), lambda c: c, carry)

    for t in range(Q_TILES):
        own = step_i * Q_TILES + t
        outs = []
        for hh in range(2):
            j = 2 * t + hh
            va = jnp.concatenate([v_aug(hh, jnp.maximum(own - 1, 0)), v_aug(hh, own)], axis=1)
            s = jnp.concatenate([near_ref[j, 0:MOBA_BLOCK, :] + prev_masks[j],
                                 near_ref[j, MOBA_BLOCK:, :]], axis=0)
            _, acc = absorb_scores(s, va, carry[2 * j], carry[2 * j + 1])
            outs.append(acc[0:HEAD_DIM] / acc[HEAD_DIM:HEAD_DIM + 1])
        o_ref[t * MOBA_BLOCK:(t + 1) * MOBA_BLOCK, :] = (
            jnp.concatenate(outs, axis=0).T.astype(BF16))


def _attention(t5_bias, qt3, proj3, vt3, bias_tiles):
    n_pairs = HEADS // 2
    assert FAR_BLOCKS == Q_TILES
    return pl.pallas_call(
        _attn_kernel,
        grid=(BATCH, n_pairs, N_BLOCKS // Q_TILES),
        in_specs=[
            pl.BlockSpec(memory_space=pltpu.SMEM),
            pl.BlockSpec((None, Q_TILES, LANES, MOBA_BLOCK), lambda b, p, i: (b, i, p, 0)),
            pl.BlockSpec((None, SEQ, LANES), lambda b, p, i: (b, 0, p)),
            pl.BlockSpec((None, N_BLOCKS, LANES, MOBA_BLOCK), lambda b, p, i: (b, 0, p, 0)),
            pl.BlockSpec((2, 2, MOBA_BLOCK, MOBA_BLOCK), lambda b, p, i: (p, 0, 0, 0)),
        ],
        out_specs=pl.BlockSpec((None, Q_TILES * MOBA_BLOCK, LANES), lambda b, p, i: (b, i, p)),
        out_shape=jax.ShapeDtypeStruct((BATCH, SEQ, WIDTH), BF16),
        scratch_shapes=[
            pltpu.VMEM((LANES, LANES), BF16),
            pltpu.VMEM((LANES, LANES), BF16),
            pltpu.VMEM((SEQ, LANES), BF16),
            pltpu.VMEM((2, FAR_KEYS, 2 * Q_TILES * MOBA_BLOCK), F32),
            pltpu.VMEM((2 * Q_TILES, 2 * MOBA_BLOCK, MOBA_BLOCK), F32),
        ],
        compiler_params=pltpu.CompilerParams(
            dimension_semantics=("arbitrary", "arbitrary", "arbitrary"),
            vmem_limit_bytes=VMEM_LIMIT),
        name="moba_attention",
    )(t5_bias, qt3, proj3, vt3, bias_tiles)


def _hgrn_kernel(q_ref, f_ref, i_ref, g_ref, lbl_ref, nw_ref, o_ref,
                 state_ref, bpad_ref, kpad_ref, vpad_ref, shb_ref, shk_ref, shv_ref,
                 ones_ref, mask_ref):
    C = HGRN_CHUNK
    half = WIDTH // 2
    assert HEAD_DIM == 1 << 6

    @pl.when(pl.program_id(1) == 0)
    def _():
        state_ref[...] = jnp.zeros_like(state_ref)
        sr = lax.broadcasted_iota(jnp.int32, (half, half), 0) >> 6
        sc = lax.broadcasted_iota(jnp.int32, (half, half), 1) >> 6
        same = jnp.where(sr == sc, 1.0, 0.0)
        ones_ref[...] = same.astype(BF16)
        mask_ref[...] = same

    zpad = jnp.zeros((8, WIDTH), F32)
    bpad_ref[0:8, :] = zpad
    kpad_ref[0:8, :] = zpad
    vpad_ref[0:8, :] = zpad

    l0 = lbl_ref[0:1, :]
    l1 = lbl_ref[1:2, :]
    lmx = jnp.maximum(l0, l1)
    e0 = jnp.exp(l0 - lmx)
    lb = e0 / (e0 + jnp.exp(l1 - lmx))
    nw = nw_ref[...]

    ri = lax.broadcasted_iota(jnp.int32, (C, C), 0)
    ci = lax.broadcasted_iota(jnp.int32, (C, C), 1)
    tril = jnp.where(ri >= ci, 1.0, 0.0).astype(BF16)

    def head_sum(t):
        return jnp.concatenate(
            [jnp.dot(t[:, 0:half], ones_ref[...], preferred_element_type=F32),
             jnp.dot(t[:, half:], ones_ref[...], preferred_element_type=F32)], axis=1)

    def chunk(c, _):
        r0 = pl.multiple_of(c * C, C)
        qq = q_ref[pl.ds(r0, C), :].astype(F32)
        qq = qq * _sigmoid(qq)
        f = lb + (1.0 - lb) * _sigmoid(f_ref[pl.ds(r0, C), :])
        gl = jnp.log(f)
        kk = 1.0 - f
        vv = i_ref[pl.ds(r0, C), :].astype(F32)
        g_hi = gl.astype(BF16)
        g_r = gl - g_hi.astype(F32)
        g_mid = g_r.astype(BF16)
        g_lo = (g_r - g_mid.astype(F32)).astype(BF16)
        b = (jnp.dot(tril, g_hi, preferred_element_type=F32)
             + jnp.dot(tril, g_mid, preferred_element_type=F32)
             + jnp.dot(tril, g_lo, preferred_element_type=F32))
        b = b * LOG2E

        bpad_ref[8:8 + C, :] = b
        kpad_ref[8:8 + C, :] = kk
        vpad_ref[8:8 + C, :] = vv
        for dd in range(8):
            shb_ref[dd] = bpad_ref[8 - dd:8 - dd + C, :]
            shk_ref[dd] = kpad_ref[8 - dd:8 - dd + C, :]
            shv_ref[dd] = vpad_ref[8 - dd:8 - dd + C, :]

        o_parts = []
        for j in range(C // 8):
            rows = C - 8 * j
            bt = b[8 * j:, :]
            qt = qq[8 * j:, :]
            terms = []
            vs_all = []
            for dd in range(8):
                terms.append(qt * shk_ref[dd, 0:rows, :] * jnp.exp2(bt - shb_ref[dd, 0:rows, :]))
                vs_all.append(shv_ref[dd, 0:rows, :])
            a = head_sum(jnp.concatenate(terms, axis=0).astype(BF16))
            av = a * jnp.concatenate(vs_all, axis=0)
            part = av[0:rows]
            for dd in range(1, 8):
                part = part + av[dd * rows:(dd + 1) * rows]
            o_parts.append(part)
        o = o_parts[0]
        for j in range(1, C // 8):
            o = o + jnp.concatenate([jnp.zeros((8 * j, WIDTH), F32), o_parts[j]], axis=0)

        qe = (qq * jnp.exp2(b)).astype(BF16)
        b_last = b[C - 1:C, :]
        kd = kk * jnp.exp2(b_last - b)
        dec = jnp.exp2(b_last)
        inter = []
        for a_i in range(2):
            sl = slice(a_i * half, (a_i + 1) * half)
            st = state_ref[a_i]
            inter.append(_nt_dot(qe[:, sl], st.astype(BF16)))
            upd = _tn_dot(vv[:, sl].astype(BF16), kd[:, sl].astype(BF16))
            state_ref[a_i] = st * dec[:, sl] + upd * mask_ref[...]
        o = o + jnp.concatenate(inter, axis=1)

        ms = head_sum((o * o).astype(BF16)) * (1.0 / HEAD_DIM)
        r = o * lax.rsqrt(ms + RMS_EPS) * nw
        r = r * _sigmoid(g_ref[pl.ds(r0, C), :].astype(F32))
        o_ref[pl.ds(r0, C), :] = r.astype(BF16)
        return 0

    lax.fori_loop(0, HGRN_ROWS // C, chunk, 0, unroll=8)


def _hgrn(proj3, f3, lb_logits, norm_w):
    blk = lambda col: pl.BlockSpec((None, HGRN_ROWS, WIDTH), lambda b, i, col=col: (b, i, col))
    return pl.pallas_call(
        _hgrn_kernel,
        grid=(BATCH, SEQ // HGRN_ROWS),
        in_specs=[
            blk(COL_HQ),
            pl.BlockSpec((None, HGRN_ROWS, WIDTH), lambda b, i: (b, i, 0)),
            blk(COL_HI),
            blk(COL_HG),
            pl.BlockSpec((2, WIDTH), lambda b, i: (0, 0)),
            pl.BlockSpec((1, WIDTH), lambda b, i: (0, 0)),
        ],
        out_specs=pl.BlockSpec((None, HGRN_ROWS, WIDTH), lambda b, i: (b, i, 0)),
        out_shape=jax.ShapeDtypeStruct((BATCH, SEQ, WIDTH), BF16),
        scratch_shapes=[
            pltpu.VMEM((2, WIDTH // 2, WIDTH // 2), F32),
            pltpu.VMEM((8 + HGRN_CHUNK, WIDTH), F32),
            pltpu.VMEM((8 + HGRN_CHUNK, WIDTH), F32),
            pltpu.VMEM((8 + HGRN_CHUNK, WIDTH), F32),
            pltpu.VMEM((8, HGRN_CHUNK, WIDTH), F32),
            pltpu.VMEM((8, HGRN_CHUNK, WIDTH), F32),
            pltpu.VMEM((8, HGRN_CHUNK, WIDTH), F32),
            pltpu.VMEM((WIDTH // 2, WIDTH // 2), BF16),
            pltpu.VMEM((WIDTH // 2, WIDTH // 2), F32),
        ],
        compiler_params=pltpu.CompilerParams(
            dimension_semantics=("arbitrary", "arbitrary"), vmem_limit_bytes=VMEM_LIMIT),
        name="hgrn2",
    )(proj3, f3, proj3, proj3, lb_logits, norm_w)


def _out_kernel(a_ref, r_ref, x_ref, wo_ref, g_ref, b_ref, wr_ref, br_ref,
                x1_ref, comb_ref):
    mix = jnp.dot(a_ref[...], wo_ref[0:WIDTH, :], preferred_element_type=F32)
    mix = mix + jnp.dot(r_ref[...], wo_ref[WIDTH:, :], preferred_element_type=F32)
    x1 = _layer_norm(ALPHA * x_ref[...] + mix, g_ref[...], b_ref[...])
    x1_ref[...] = x1

    x_hi = x1.astype(BF16)
    x_lo = (x1 - x_hi.astype(F32)).astype(BF16)
    wr = wr_ref[...]
    w_hi = wr.astype(BF16)
    w_lo = (wr - w_hi.astype(F32)).astype(BF16)
    logits = (jnp.dot(x_hi, w_hi, preferred_element_type=F32)
              + jnp.dot(x_lo, w_hi, preferred_element_type=F32)
              + jnp.dot(x_hi, w_lo, preferred_element_type=F32)) + br_ref[...]
    assert EXPERTS_PER_GROUP == 1 << 2
    lane_i = lax.broadcasted_iota(jnp.int32, logits.shape, 1)
    lane = lane_i.astype(F32)
    grp_of_lane = (lane_i >> 2).astype(F32)
    none = float(LANES)
    is_g = (lane_i >= N_EXPERTS) & (lane_i < N_EXPERTS + N_GROUPS)
    gl = jnp.where(is_g, logits, -jnp.inf)
    ge = jnp.exp(gl - jnp.max(gl, axis=-1, keepdims=True))
    gp = ge / jnp.sum(ge, axis=-1, keepdims=True)
    g_w = jnp.max(gp, axis=-1, keepdims=True)
    g_lane = jnp.min(jnp.where(is_g & (gp == g_w), lane, none), axis=-1, keepdims=True)
    g_idx = g_lane - float(N_EXPERTS)

    in_grp = (lane_i < N_EXPERTS) & (grp_of_lane == g_idx)
    el = jnp.where(in_grp, logits, -jnp.inf)
    ee = jnp.exp(el - jnp.max(el, axis=-1, keepdims=True))
    ep = ee / jnp.sum(ee, axis=-1, keepdims=True)
    p1 = jnp.max(ep, axis=-1, keepdims=True)
    i1 = jnp.min(jnp.where(in_grp & (ep == p1), lane, none), axis=-1, keepdims=True)
    rest = in_grp & (lane != i1)
    ep2 = jnp.where(rest, ep, -1.0)
    p2 = jnp.max(ep2, axis=-1, keepdims=True)
    i2 = jnp.min(jnp.where(rest & (ep2 == p2), lane, none), axis=-1, keepdims=True)
    den = p1 + p2
    comb = jnp.where(lane == i1, g_w * (p1 / den), 0.0)
    comb = jnp.where(lane == i2, g_w * (p2 / den), comb)
    comb_ref[...] = jnp.where(lane_i == GROUP_LANE, g_idx, comb)


def _out_proj(a2, r2, x2, wo_b, g1, b1, w_route, b_route):
    row = lambda w: pl.BlockSpec((OUT_TM, w), lambda i: (i, 0))
    full = lambda s: pl.BlockSpec(s, lambda i: (0, 0))
    return pl.pallas_call(
        _out_kernel,
        grid=(N_TOK // OUT_TM,),
        in_specs=[row(WIDTH), row(WIDTH), row(D_MODEL), full((D_MODEL, D_MODEL)),
                  full((1, D_MODEL)), full((1, D_MODEL)), full((D_MODEL, LANES)), full((1, LANES))],
        out_specs=[row(D_MODEL), row(LANES)],
        out_shape=[jax.ShapeDtypeStruct((N_TOK, D_MODEL), F32),
                   jax.ShapeDtypeStruct((N_TOK, LANES), F32)],
        compiler_params=pltpu.CompilerParams(
            dimension_semantics=("arbitrary",), vmem_limit_bytes=VMEM_LIMIT),
        name="out_proj_ln_route",
    )(a2, r2, x2, wo_b, g1, b1, w_route, b_route)


def _moe_kernel(x1_ref, comb_ref, wg_ref, wu_ref, wd_ref, g_ref, b_ref, o_ref,
                pt_ref, xs_ref, cs_ref, acc_ref, win_ref):
    e = pl.program_id(1)
    assert MOE_ALIGN == 1 << 7 and EXPERTS_PER_GROUP == 1 << 2

    @pl.when(e == 0)
    def _():
        comb = comb_ref[...]
        lane = lax.broadcasted_iota(jnp.int32, comb.shape, 1)
        lane_f = lane.astype(F32)
        gid = jnp.sum(jnp.where(lane == GROUP_LANE, comb, 0.0), axis=-1, keepdims=True)
        mine = lane_f == gid
        ti = lax.broadcasted_iota(jnp.int32, (MOE_TM, MOE_TM), 0)
        tj = lax.broadcasted_iota(jnp.int32, (MOE_TM, MOE_TM), 1)
        tril = jnp.where(ti >= tj, 1.0, 0.0).astype(BF16)
        incl = jnp.dot(tril, jnp.where(mine, 1.0, 0.0).astype(BF16),
                       preferred_element_type=F32)
        rank = jnp.sum(jnp.where(mine, incl, 0.0), axis=-1, keepdims=True) - 1.0
        counts = incl[MOE_TM - 1:MOE_TM, :]
        lane_row = lane[0:1, :]
        start = jnp.int32(0)
        base = jnp.zeros_like(gid)
        for g in range(N_GROUPS):
            cnt = jnp.sum(jnp.where(lane_row == g, counts, 0.0)).astype(jnp.int32)
            chunks = (cnt + (MOE_ALIGN - 1)) >> 7
            win_ref[0, g] = start
            win_ref[1, g] = chunks >> 1
            win_ref[2, g] = chunks & 1
            base = jnp.where(gid == float(g), start.astype(F32), base)
            start = start + chunks * MOE_ALIGN
        pos = base + rank
        r_iota = lax.broadcasted_iota(jnp.int32, (MOE_TM, MOE_ROWS), 1).astype(F32)
        pt = jnp.where(r_iota == pos, 1.0, 0.0).astype(BF16)
        pt_ref[...] = pt
        xs_ref[...] = _tn_dot(pt, x1_ref[...].astype(BF16)).astype(BF16)
        c_hi = comb.astype(BF16)
        c_lo = (comb - c_hi.astype(F32)).astype(BF16)
        cs_ref[...] = _tn_dot(pt, c_hi) + _tn_dot(pt, c_lo)
        acc_ref[...] = jnp.zeros_like(acc_ref)

    g = e >> 2
    row0 = win_ref[0, g]
    n_full = win_ref[1, g]

    def window(start, rows):
        r0 = pl.multiple_of(start, MOE_ALIGN)
        xw = xs_ref[pl.ds(r0, rows), :]
        hg = jnp.dot(xw, wg_ref[...], preferred_element_type=F32)
        hu = jnp.dot(xw, wu_ref[...], preferred_element_type=F32)
        h = (hg * _sigmoid(hg) * hu).astype(BF16)
        y = jnp.dot(h, wd_ref[...], preferred_element_type=F32)
        lane_w = lax.broadcasted_iota(jnp.int32, (rows, LANES), 1)
        c = jnp.sum(jnp.where(lane_w == e, cs_ref[pl.ds(r0, rows), :], 0.0),
                    axis=-1, keepdims=True)
        acc_ref[pl.ds(r0, rows), :] += c * y

    def full_window(k, _):
        window(row0 + k * MOE_WIN, MOE_WIN)
        return 0

    lax.fori_loop(0, n_full, full_window, 0)

    @pl.when(win_ref[2, g] == 1)
    def _():
        window(row0 + n_full * MOE_WIN, MOE_ALIGN)

    @pl.when(e == N_EXPERTS - 1)
    def _():
        moe = jnp.dot(pt_ref[...], acc_ref[...].astype(BF16), preferred_element_type=F32)
        o_ref[...] = _layer_norm(ALPHA * x1_ref[...] + moe, g_ref[...], b_ref[...])


def _moe(x1, comb, wg_b, wu_b, wd_b, g2, b2):
    return pl.pallas_call(
        _moe_kernel,
        grid=(N_TOK // MOE_TM, N_EXPERTS),
        in_specs=[
            pl.BlockSpec((MOE_TM, D_MODEL), lambda i, e: (i, 0)),
            pl.BlockSpec((MOE_TM, LANES), lambda i, e: (i, 0)),
            pl.BlockSpec((None, D_MODEL, D_EXPERT), lambda i, e: (e, 0, 0)),
            pl.BlockSpec((None, D_MODEL, D_EXPERT), lambda i, e: (e, 0, 0)),
            pl.BlockSpec((None, D_EXPERT, D_MODEL), lambda i, e: (e, 0, 0)),
            pl.BlockSpec((1, D_MODEL), lambda i, e: (0, 0)),
            pl.BlockSpec((1, D_MODEL), lambda i, e: (0, 0)),
        ],
        out_specs=pl.BlockSpec((MOE_TM, D_MODEL), lambda i, e: (i, 0)),
        out_shape=jax.ShapeDtypeStruct((N_TOK, D_MODEL), F32),
        scratch_shapes=[
            pltpu.VMEM((MOE_TM, MOE_ROWS), BF16),
            pltpu.VMEM((MOE_ROWS, D_MODEL), BF16),
            pltpu.VMEM((MOE_ROWS, LANES), F32),
            pltpu.VMEM((MOE_ROWS, D_MODEL), F32),
            pltpu.SMEM((3, N_GROUPS), jnp.int32),
        ],
        compiler_params=pltpu.CompilerParams(
            dimension_semantics=("arbitrary", "arbitrary"), vmem_limit_bytes=VMEM_LIMIT),
        name="moe_experts_ln",
    )(x1, comb, wg_b, wu_b, wd_b, g2, b2)


def kernel(x, w_in, t5_bias, hgrn_lb_logits, hgrn_norm_w, w_o, ln1_g, ln1_b, w_group, b_group,
           w_expert, b_expert, w_gate, w_up, w_down, ln2_g, ln2_b):
    x2 = x.reshape(N_TOK, D_MODEL)
    w_in_b = w_in[0].astype(BF16)
    w_qt_b = w_in_b[:, 0:WIDTH].T
    w_vt_b = w_in_b[:, 2 * WIDTH:3 * WIDTH].T
    proj, f_logits, qt3, vt3 = _proj(x2, w_in_b, w_qt_b, w_vt_b)
    proj3 = proj.reshape(BATCH, SEQ, ROW_WIDTH)

    a = _attention(t5_bias, qt3, proj3, vt3, _bias_tiles(t5_bias))
    r = _hgrn(proj3, f_logits.reshape(BATCH, SEQ, WIDTH), hgrn_lb_logits, hgrn_norm_w)

    pad = LANES - N_EXPERTS - N_GROUPS
    w_route = jnp.concatenate(
        [w_expert[0].transpose(1, 0, 2).reshape(D_MODEL, N_EXPERTS), w_group[0],
         jnp.zeros((D_MODEL, pad), F32)], axis=1)
    b_route = jnp.concatenate(
        [b_expert[0].reshape(N_EXPERTS), b_group[0], jnp.zeros((pad,), F32)]).reshape(1, LANES)
    x1, comb = _out_proj(
        a.reshape(N_TOK, WIDTH), r.reshape(N_TOK, WIDTH), x2, w_o[0].astype(BF16),
        ln1_g, ln1_b, w_route, b_route)

    out = _moe(x1, comb, w_gate[0].astype(BF16), w_up[0].astype(BF16),
               w_down[0].astype(BF16), ln2_g, ln2_b)
    return out.reshape(BATCH, SEQ, D_MODEL)
```

```python
import functools
import math

import jax
import jax.numpy as jnp
from jax import lax
from jax.experimental import pallas as pl
from jax.experimental.pallas import tpu as pltpu

F32 = jnp.float32
BF16 = jnp.bfloat16

D_MODEL = 1024
BATCH = 2
SEQ = 8192
N_TOK = BATCH * SEQ
HEADS = 8
HEAD_DIM = 64
WIDTH = HEADS * HEAD_DIM
IN_WIDTH = 7 * WIDTH
ROW_WIDTH = 5 * WIDTH
COL_HQ, COL_HF, COL_HI, COL_HG = 1, 2, 3, 4
MOBA_BLOCK = 256
N_BLOCKS = SEQ // MOBA_BLOCK
MOBA_TOPK = 3
HGRN_CHUNK = 32
N_BUCKETS = 32
MAX_DISTANCE = 128
N_GROUPS = 4
EXPERTS_PER_GROUP = 4
N_EXPERTS = 16
D_EXPERT = 512
LN_EPS = 1e-5
RMS_EPS = 1e-6
ALPHA = 2.0 ** 0.25
LOG2E = math.log2(math.e)

LANES = 128
BF16_SUBLANES = 16
NEG = -1e30
VMEM_LIMIT = 48 * 1024 * 1024
MOE_VMEM_LIMIT = 56 * 1024 * 1024

PROJ_TM = 1024
OUT_TM = 512
MOE_TM = 1024
MOE_ALIGN = 128
MOE_WIN = 256
MOE_EXPERTS_PER_STEP = 2
MOE_ROWS = MOE_TM + N_GROUPS * MOE_ALIGN
GROUP_LANE = N_EXPERTS
HGRN_ROWS = 256
Q_TILES = 2
FAR_BLOCKS = 2
FAR_KEYS = FAR_BLOCKS * MOBA_BLOCK
N_FAR_STEPS = N_BLOCKS // FAR_BLOCKS

BIAS_HI_LANE = N_BLOCKS
BIAS_LO_LANE = N_BLOCKS + 1


def _nt_dot(a, b):
    return lax.dot_general(a, b, (((1,), (1,)), ((), ())), preferred_element_type=F32)


def _tn_dot(a, b):
    return lax.dot_general(a, b, (((0,), (0,)), ((), ())), preferred_element_type=F32)


def _sigmoid(x):
    return 1.0 / (1.0 + jnp.exp2(x * (-LOG2E)))


def _layer_norm(y, g, b):
    mu = jnp.mean(y, axis=-1, keepdims=True)
    d = y - mu
    var = jnp.mean(d * d, axis=-1, keepdims=True)
    return d * lax.rsqrt(var + LN_EPS) * g + b


def _bias_tile_kernel(tab_ref, out_ref):
    h = pl.program_id(0)
    ki = lax.broadcasted_iota(jnp.int32, (MOBA_BLOCK, MOBA_BLOCK), 0)
    qi = lax.broadcasted_iota(jnp.int32, (MOBA_BLOCK, MOBA_BLOCK), 1)
    max_exact = N_BUCKETS // 2
    for w in range(2):
        n = jnp.maximum(qi - ki + MOBA_BLOCK * w, 0)
        nf = jnp.maximum(n, 1).astype(F32)
        scaled = (jnp.log(nf / max_exact) / math.log(MAX_DISTANCE / max_exact)
                  * (N_BUCKETS - max_exact))
        large = max_exact + jnp.floor(jnp.maximum(scaled, 0.0)).astype(jnp.int32)
        large = jnp.minimum(large, N_BUCKETS - 1)
        bucket = jnp.where(n < max_exact, n, large)
        acc = jnp.zeros((MOBA_BLOCK, MOBA_BLOCK), F32)
        for bk in range(N_BUCKETS):
            acc = jnp.where(bucket == bk, tab_ref[bk, h], acc)
        out_ref[0, w] = acc


def _bias_tiles(t5_bias):
    return pl.pallas_call(
        _bias_tile_kernel,
        grid=(HEADS,),
        in_specs=[pl.BlockSpec(memory_space=pltpu.SMEM)],
        out_specs=pl.BlockSpec((1, 2, MOBA_BLOCK, MOBA_BLOCK), lambda h: (h, 0, 0, 0)),
        out_shape=jax.ShapeDtypeStruct((HEADS, 2, MOBA_BLOCK, MOBA_BLOCK), F32),
        name="t5_bias_tiles",
    )(t5_bias)


def _proj_kernel(x_ref, w_ref, wqt_ref, wvt_ref, o_ref, f_ref, qt_ref, vt_ref, xb_ref):
    n = pl.program_id(1)

    def feature_major(wt_ref, t_ref):
        t = _nt_dot(wt_ref[...], xb_ref[...]).astype(BF16)
        for c in range(PROJ_TM // MOBA_BLOCK):
            t_ref[c] = t[:, c * MOBA_BLOCK:(c + 1) * MOBA_BLOCK]

    @pl.when(n == 0)
    def _():
        xb_ref[...] = x_ref[...].astype(BF16)
        feature_major(wqt_ref, qt_ref)

    acc = jnp.dot(xb_ref[...], w_ref[...], preferred_element_type=F32)
    o_ref[...] = acc.astype(BF16)

    @pl.when(n == 1)
    def _():
        feature_major(wvt_ref, vt_ref)

    @pl.when(n == COL_HF)
    def _():
        f_ref[...] = acc


def _proj(x2, w_in_b, w_qt_b, w_vt_b):
    tiles_per_seq = SEQ // PROJ_TM
    blocks_per_tile = PROJ_TM // MOBA_BLOCK
    t_spec = pl.BlockSpec((None, blocks_per_tile, WIDTH, MOBA_BLOCK),
                          lambda i, n: (i // tiles_per_seq, i % tiles_per_seq, 0, 0))
    t_shape = jax.ShapeDtypeStruct((BATCH, N_BLOCKS, WIDTH, MOBA_BLOCK), BF16)
    return pl.pallas_call(
        _proj_kernel,
        grid=(N_TOK // PROJ_TM, ROW_WIDTH // WIDTH),
        in_specs=[
            pl.BlockSpec((PROJ_TM, D_MODEL), lambda i, n: (i, 0)),
            pl.BlockSpec((D_MODEL, WIDTH), lambda i, n: (0, jnp.where(n == 0, 1, n + 2))),
            pl.BlockSpec((WIDTH, D_MODEL), lambda i, n: (0, 0)),
            pl.BlockSpec((WIDTH, D_MODEL), lambda i, n: (0, 0)),
        ],
        out_specs=[
            pl.BlockSpec((PROJ_TM, WIDTH), lambda i, n: (i, n)),
            pl.BlockSpec((PROJ_TM, WIDTH), lambda i, n: (i, 0)),
            t_spec,
            t_spec,
        ],
        out_shape=[
            jax.ShapeDtypeStruct((N_TOK, ROW_WIDTH), BF16),
            jax.ShapeDtypeStruct((N_TOK, WIDTH), F32),
            t_shape,
            t_shape,
        ],
        scratch_shapes=[pltpu.VMEM((PROJ_TM, D_MODEL), BF16)],
        compiler_params=pltpu.CompilerParams(
            dimension_semantics=("arbitrary", "arbitrary"), vmem_limit_bytes=VMEM_LIMIT),
        name="in_proj",
    )(x2, w_in_b, w_qt_b, w_vt_b)


def _colreduce(x, op):
    r = x[0:8]
    for i in range(1, x.shape[0] // 8):
        r = op(r, x[8 * i:8 * i + 8])
    for shift in (4, 2, 1):
        r = op(r, pltpu.roll(r, shift, 0))
    return r


def _attn_kernel(tab_ref, qt_ref, k_ref, vt_ref, bias_ref, o_ref,
                 kmh_ref, kml_ref, ext_ref, s_ref, near_ref):
    assert MOBA_BLOCK == 1 << 8
    pair = pl.program_id(1)
    step_i = pl.program_id(2)
    lane = lax.broadcasted_iota(jnp.int32, (MOBA_BLOCK, LANES), 1)

    @pl.when(step_i == 0)
    def _():
        r = lax.broadcasted_iota(jnp.int32, (LANES, SEQ), 1)
        nb = lax.broadcasted_iota(jnp.int32, (LANES, SEQ), 0)
        avg = jnp.where((r >> 8) == nb, 1.0 / MOBA_BLOCK, 0.0).astype(BF16)
        km = jnp.dot(avg, k_ref[...], preferred_element_type=F32)
        hi = km.astype(BF16)
        kmh_ref[...] = hi
        kml_ref[...] = (km - hi.astype(F32)).astype(BF16)
        kr = lax.broadcasted_iota(jnp.int32, (SEQ, LANES), 0) >> 8
        kl = lax.broadcasted_iota(jnp.int32, (SEQ, LANES), 1)
        ext_ref[...] = jnp.where((kl == kr) | (kl == BIAS_HI_LANE) | (kl == BIAS_LO_LANE),
                                 1.0, 0.0).astype(BF16)

    krow = lax.broadcasted_iota(jnp.int32, (MOBA_BLOCK, MOBA_BLOCK), 0)
    qcol = lax.broadcasted_iota(jnp.int32, (MOBA_BLOCK, MOBA_BLOCK), 1)
    frow = lax.broadcasted_iota(jnp.int32, (LANES, MOBA_BLOCK), 0)
    nrow = lax.broadcasted_iota(jnp.int32, (N_BLOCKS, MOBA_BLOCK), 0)
    nrow_f = nrow.astype(F32)
    brow = lax.broadcasted_iota(jnp.int32, (8, MOBA_BLOCK), 0)
    ones_rows = jnp.ones((BF16_SUBLANES, MOBA_BLOCK), BF16)
    zero_rows = jnp.zeros((LANES - N_BLOCKS - 8, MOBA_BLOCK), F32)
    km_hi = kmh_ref[0:N_BLOCKS, :]
    km_lo = kml_ref[0:N_BLOCKS, :]

    def v_aug(hh, block):
        vt = vt_ref[block, hh * HEAD_DIM:(hh + 1) * HEAD_DIM, :]
        return jnp.concatenate([vt, ones_rows], axis=0)

    def rows_to_tile(x8):
        return jnp.concatenate([x8] * (N_BLOCKS // 8), axis=0)

    q_far = []
    prev_masks = []
    carry = []
    for t in range(Q_TILES):
        own = step_i * Q_TILES + t
        prev = jnp.maximum(own - 1, 0)
        qt2 = qt_ref[t]
        k_own = k_ref[pl.ds(pl.multiple_of(own * MOBA_BLOCK, MOBA_BLOCK), MOBA_BLOCK), :]
        k_prev = k_ref[pl.ds(pl.multiple_of(prev * MOBA_BLOCK, MOBA_BLOCK), MOBA_BLOCK), :]
        for hh in range(2):
            head = pair * 2 + hh
            in_head = (frow >= hh * HEAD_DIM) & (frow < (hh + 1) * HEAD_DIM)
            qs = jnp.where(in_head, qt2, jnp.zeros_like(qt2)) * jnp.asarray(HEAD_DIM ** -0.5, BF16)

            gate = (jnp.dot(km_hi, qs, preferred_element_type=F32)
                    + jnp.dot(km_lo, qs, preferred_element_type=F32))
            eligible = nrow < own
            g = jnp.where(eligible, gate, -jnp.inf)
            sel = jnp.zeros((N_BLOCKS, MOBA_BLOCK), jnp.bool_)
            for _ in range(MOBA_TOPK):
                mx = rows_to_tile(_colreduce(g, jnp.maximum))
                idx = rows_to_tile(_colreduce(jnp.where(g == mx, nrow_f, float(LANES)),
                                              jnp.minimum))
                hit = nrow_f == idx
                sel = sel | hit
                g = jnp.where(hit, -jnp.inf, g)
            sel = sel & eligible

            b31 = jnp.full((8, MOBA_BLOCK), tab_ref[N_BUCKETS - 1, head], F32)
            b31_hi = b31.astype(BF16).astype(F32)
            bias_rows = jnp.where(brow == 0, b31_hi, jnp.where(brow == 1, b31 - b31_hi, 0.0))

            aug = jnp.concatenate([jnp.where(sel & (nrow != own - 1), 0.0, NEG), bias_rows,
                                   zero_rows], axis=0)
            q_far.append(jnp.concatenate([qs, aug.astype(BF16)], axis=0))

            prev_sel = _colreduce(jnp.where((nrow == own - 1) & jnp.logical_not(sel), NEG, 0.0),
                                  jnp.add)
            prev_masks.append(jnp.where(own >= 1, prev_sel[0:1], NEG))
            near_ref[2 * t + hh, 0:MOBA_BLOCK, :] = (
                jnp.dot(k_prev, qs, preferred_element_type=F32) + bias_ref[hh, 1])
            s_own = jnp.dot(k_own, qs, preferred_element_type=F32) + bias_ref[hh, 0]
            near_ref[2 * t + hh, MOBA_BLOCK:, :] = jnp.where(qcol >= krow, s_own, NEG)
            carry += [jnp.full((1, MOBA_BLOCK), NEG, F32),
                      jnp.zeros((HEAD_DIM + BF16_SUBLANES, MOBA_BLOCK), F32)]

    q_cat = jnp.concatenate(q_far, axis=1)
    n_tiles = 2 * Q_TILES

    def absorb_scores(s, va, m, acc):
        m_new = jnp.maximum(m, jnp.max(s, axis=0, keepdims=True))
        alpha = jnp.exp(m - m_new)
        p = jnp.exp(s - m_new).astype(BF16)
        return m_new, alpha * acc + jnp.dot(va, p, preferred_element_type=F32)

    def scores(step, slot):
        start = pl.multiple_of(jnp.minimum(step, N_FAR_STEPS - 1) * FAR_KEYS, FAR_KEYS)
        k_aug = jnp.concatenate([k_ref[pl.ds(start, FAR_KEYS), :],
                                 ext_ref[pl.ds(start, FAR_KEYS), :]], axis=1)
        s_ref[slot] = jnp.dot(k_aug, q_cat, preferred_element_type=F32)

    def absorb(step, slot, carry):
        block0 = step * FAR_BLOCKS
        out = []
        for j in range(n_tiles):
            hh = j % 2
            s = s_ref[slot, :, j * MOBA_BLOCK:(j + 1) * MOBA_BLOCK]
            va = jnp.concatenate([v_aug(hh, block0 + c) for c in range(FAR_BLOCKS)], axis=1)
            out += absorb_scores(s, va, carry[2 * j], carry[2 * j + 1])
        return tuple(out)

    n_steps = step_i
    scores(0, 0)

    def two_steps(base, carry):
        scores(base + 1, 1)
        carry = absorb(base, 0, carry)
        scores(base + 2, 0)
        return absorb(base + 1, 1, carry)

    def far_body(i, carry):
        return two_steps(4 * i + 2, two_steps(4 * i, carry))

    n_quads = n_steps // 4
    carry = lax.fori_loop(0, n_quads, far_body, tuple(carry))
    carry = lax.cond(n_steps % 4 >= 2,
                     lambda c: two_steps(4 * n_quads, c), lambda c: c, carry)
    carry = lax.cond(n_steps % 2 == 1,
                     lambda c: absorb(n_steps - 1, 0, c), lambda c: c, carry)

    for t in range(Q_TILES):
        own = step_i * Q_TILES + t
        outs = []
        for hh in range(2):
            j = 2 * t + hh
            va = jnp.concatenate([v_aug(hh, jnp.maximum(own - 1, 0)), v_aug(hh, own)], axis=1)
            s = jnp.concatenate([near_ref[j, 0:MOBA_BLOCK, :] + prev_masks[j],
                                 near_ref[j, MOBA_BLOCK:, :]], axis=0)
            _, acc = absorb_scores(s, va, carry[2 * j], carry[2 * j + 1])
            outs.append(acc[0:HEAD_DIM] / acc[HEAD_DIM:HEAD_DIM + 1])
        o_ref[t * MOBA_BLOCK:(t + 1) * MOBA_BLOCK, :] = (
            jnp.concatenate(outs, axis=0).T.astype(BF16))


def _attention(t5_bias, qt3, proj3, vt3, bias_tiles):
    n_pairs = HEADS // 2
    assert FAR_BLOCKS == Q_TILES
    return pl.pallas_call(
        _attn_kernel,
        grid=(BATCH, n_pairs, N_BLOCKS // Q_TILES),
        in_specs=[
            pl.BlockSpec(memory_space=pltpu.SMEM),
            pl.BlockSpec((None, Q_TILES, LANES, MOBA_BLOCK), lambda b, p, i: (b, i, p, 0)),
            pl.BlockSpec((None, SEQ, LANES), lambda b, p, i: (b, 0, p)),
            pl.BlockSpec((None, N_BLOCKS, LANES, MOBA_BLOCK), lambda b, p, i: (b, 0, p, 0)),
            pl.BlockSpec((2, 2, MOBA_BLOCK, MOBA_BLOCK), lambda b, p, i: (p, 0, 0, 0)),
        ],
        out_specs=pl.BlockSpec((None, Q_TILES * MOBA_BLOCK, LANES), lambda b, p, i: (b, i, p)),
        out_shape=jax.ShapeDtypeStruct((BATCH, SEQ, WIDTH), BF16),
        scratch_shapes=[
            pltpu.VMEM((LANES, LANES), BF16),
            pltpu.VMEM((LANES, LANES), BF16),
            pltpu.VMEM((SEQ, LANES), BF16),
            pltpu.VMEM((2, FAR_KEYS, 2 * Q_TILES * MOBA_BLOCK), F32),
            pltpu.VMEM((2 * Q_TILES, 2 * MOBA_BLOCK, MOBA_BLOCK), F32),
        ],
        compiler_params=pltpu.CompilerParams(
            dimension_semantics=("arbitrary", "arbitrary", "arbitrary"),
            vmem_limit_bytes=VMEM_LIMIT),
        name="moba_attention",
    )(t5_bias, qt3, proj3, vt3, bias_tiles)


def _hgrn_kernel(q_ref, f_ref, i_ref, g_ref, lbl_ref, nw_ref, o_ref,
                 state_ref, bpad_ref, kpad_ref, vpad_ref, shb_ref, shk_ref, shv_ref,
                 ones_ref, mask_ref):
    C = HGRN_CHUNK
    half = WIDTH // 2
    assert HEAD_DIM == 1 << 6

    @pl.when(pl.program_id(1) == 0)
    def _():
        state_ref[...] = jnp.zeros_like(state_ref)
        sr = lax.broadcasted_iota(jnp.int32, (half, half), 0) >> 6
        sc = lax.broadcasted_iota(jnp.int32, (half, half), 1) >> 6
        same = jnp.where(sr == sc, 1.0, 0.0)
        ones_ref[...] = same.astype(BF16)
        mask_ref[...] = same

    zpad = jnp.zeros((8, WIDTH), F32)
    bpad_ref[0:8, :] = zpad
    kpad_ref[0:8, :] = zpad
    vpad_ref[0:8, :] = zpad

    l0 = lbl_ref[0:1, :]
    l1 = lbl_ref[1:2, :]
    lmx = jnp.maximum(l0, l1)
    e0 = jnp.exp(l0 - lmx)
    lb = e0 / (e0 + jnp.exp(l1 - lmx))
    nw = nw_ref[...]

    ri = lax.broadcasted_iota(jnp.int32, (C, C), 0)
    ci = lax.broadcasted_iota(jnp.int32, (C, C), 1)
    tril = jnp.where(ri >= ci, 1.0, 0.0).astype(BF16)

    def head_sum(t):
        return jnp.concatenate(
            [jnp.dot(t[:, 0:half], ones_ref[...], preferred_element_type=F32),
             jnp.dot(t[:, half:], ones_ref[...], preferred_element_type=F32)], axis=1)

    def chunk(c, _):
        r0 = pl.multiple_of(c * C, C)
        qq = q_ref[pl.ds(r0, C), :].astype(F32)
        qq = qq * _sigmoid(qq)
        f = lb + (1.0 - lb) * _sigmoid(f_ref[pl.ds(r0, C), :])
        gl = jnp.log(f)
        kk = 1.0 - f
        vv = i_ref[pl.ds(r0, C), :].astype(F32)
        g_hi = gl.astype(BF16)
        g_r = gl - g_hi.astype(F32)
        g_mid = g_r.astype(BF16)
        g_lo = (g_r - g_mid.astype(F32)).astype(BF16)
        b = (jnp.dot(tril, g_hi, preferred_element_type=F32)
             + jnp.dot(tril, g_mid, preferred_element_type=F32)
             + jnp.dot(tril, g_lo, preferred_element_type=F32))
        b = b * LOG2E

        bpad_ref[8:8 + C, :] = b
        kpad_ref[8:8 + C, :] = kk
        vpad_ref[8:8 + C, :] = vv
        for dd in range(8):
            shb_ref[dd] = bpad_ref[8 - dd:8 - dd + C, :]
            shk_ref[dd] = kpad_ref[8 - dd:8 - dd + C, :]
            shv_ref[dd] = vpad_ref[8 - dd:8 - dd + C, :]

        o_parts = []
        for j in range(C // 8):
            rows = C - 8 * j
            bt = b[8 * j:, :]
            qt = qq[8 * j:, :]
            terms = []
            vs_all = []
            for dd in range(8):
                terms.append(qt * shk_ref[dd, 0:rows, :] * jnp.exp2(bt - shb_ref[dd, 0:rows, :]))
                vs_all.append(shv_ref[dd, 0:rows, :])
            a = head_sum(jnp.concatenate(terms, axis=0).astype(BF16))
            av = a * jnp.concatenate(vs_all, axis=0)
            part = av[0:rows]
            for dd in range(1, 8):
                part = part + av[dd * rows:(dd + 1) * rows]
            o_parts.append(part)
        o = o_parts[0]
        for j in range(1, C // 8):
            o = o + jnp.concatenate([jnp.zeros((8 * j, WIDTH), F32), o_parts[j]], axis=0)

        qe = (qq * jnp.exp2(b)).astype(BF16)
        b_last = b[C - 1:C, :]
        kd = kk * jnp.exp2(b_last - b)
        dec = jnp.exp2(b_last)
        inter = []
        for a_i in range(2):
            sl = slice(a_i * half, (a_i + 1) * half)
            st = state_ref[a_i]
            inter.append(_nt_dot(qe[:, sl], st.astype(BF16)))
            upd = _tn_dot(vv[:, sl].astype(BF16), kd[:, sl].astype(BF16))
            state_ref[a_i] = st * dec[:, sl] + upd * mask_ref[...]
        o = o + jnp.concatenate(inter, axis=1)

        ms = head_sum((o * o).astype(BF16)) * (1.0 / HEAD_DIM)
        r = o * lax.rsqrt(ms + RMS_EPS) * nw
        r = r * _sigmoid(g_ref[pl.ds(r0, C), :].astype(F32))
        o_ref[pl.ds(r0, C), :] = r.astype(BF16)
        return 0

    lax.fori_loop(0, HGRN_ROWS // C, chunk, 0, unroll=8)


def _hgrn(proj3, f3, lb_logits, norm_w):
    blk = lambda col: pl.BlockSpec((None, HGRN_ROWS, WIDTH), lambda b, i, col=col: (b, i, col))
    return pl.pallas_call(
        _hgrn_kernel,
        grid=(BATCH, SEQ // HGRN_ROWS),
        in_specs=[
            blk(COL_HQ),
            pl.BlockSpec((None, HGRN_ROWS, WIDTH), lambda b, i: (b, i, 0)),
            blk(COL_HI),
            blk(COL_HG),
            pl.BlockSpec((2, WIDTH), lambda b, i: (0, 0)),
            pl.BlockSpec((1, WIDTH), lambda b, i: (0, 0)),
        ],
        out_specs=pl.BlockSpec((None, HGRN_ROWS, WIDTH), lambda b, i: (b, i, 0)),
        out_shape=jax.ShapeDtypeStruct((BATCH, SEQ, WIDTH), BF16),
        scratch_shapes=[
            pltpu.VMEM((2, WIDTH // 2, WIDTH // 2), F32),
            pltpu.VMEM((8 + HGRN_CHUNK, WIDTH), F32),
            pltpu.VMEM((8 + HGRN_CHUNK, WIDTH), F32),
            pltpu.VMEM((8 + HGRN_CHUNK, WIDTH), F32),
            pltpu.VMEM((8, HGRN_CHUNK, WIDTH), F32),
            pltpu.VMEM((8, HGRN_CHUNK, WIDTH), F32),
            pltpu.VMEM((8, HGRN_CHUNK, WIDTH), F32),
            pltpu.VMEM((WIDTH // 2, WIDTH // 2), BF16),
            pltpu.VMEM((WIDTH // 2, WIDTH // 2), F32),
        ],
        compiler_params=pltpu.CompilerParams(
            dimension_semantics=("arbitrary", "arbitrary"), vmem_limit_bytes=VMEM_LIMIT),
        name="hgrn2",
    )(proj3, f3, proj3, proj3, lb_logits, norm_w)


def _out_kernel(a_ref, r_ref, x_ref, wo_ref, g_ref, b_ref, wr_ref, br_ref,
                x1_ref, comb_ref):
    mix = jnp.dot(a_ref[...], wo_ref[0:WIDTH, :], preferred_element_type=F32)
    mix = mix + jnp.dot(r_ref[...], wo_ref[WIDTH:, :], preferred_element_type=F32)
    x1 = _layer_norm(ALPHA * x_ref[...] + mix, g_ref[...], b_ref[...])
    x1_ref[...] = x1

    x_hi = x1.astype(BF16)
    x_lo = (x1 - x_hi.astype(F32)).astype(BF16)
    wr = wr_ref[...]
    w_hi = wr.astype(BF16)
    w_lo = (wr - w_hi.astype(F32)).astype(BF16)
    logits = (jnp.dot(x_hi, w_hi, preferred_element_type=F32)
              + jnp.dot(x_lo, w_hi, preferred_element_type=F32)
              + jnp.dot(x_hi, w_lo, preferred_element_type=F32)) + br_ref[...]
    assert EXPERTS_PER_GROUP == 1 << 2
    lane_i = lax.broadcasted_iota(jnp.int32, logits.shape, 1)
    lane = lane_i.astype(F32)
    grp_of_lane = (lane_i >> 2).astype(F32)
    none = float(LANES)
    is_g = (lane_i >= N_EXPERTS) & (lane_i < N_EXPERTS + N_GROUPS)
    gl = jnp.where(is_g, logits, -jnp.inf)
    ge = jnp.exp(gl - jnp.max(gl, axis=-1, keepdims=True))
    gp = ge / jnp.sum(ge, axis=-1, keepdims=True)
    g_w = jnp.max(gp, axis=-1, keepdims=True)
    g_lane = jnp.min(jnp.where(is_g & (gp == g_w), lane, none), axis=-1, keepdims=True)
    g_idx = g_lane - float(N_EXPERTS)

    in_grp = (lane_i < N_EXPERTS) & (grp_of_lane == g_idx)
    el = jnp.where(in_grp, logits, -jnp.inf)
    ee = jnp.exp(el - jnp.max(el, axis=-1, keepdims=True))
    ep = ee / jnp.sum(ee, axis=-1, keepdims=True)
    p1 = jnp.max(ep, axis=-1, keepdims=True)
    i1 = jnp.min(jnp.where(in_grp & (ep == p1), lane, none), axis=-1, keepdims=True)
    rest = in_grp & (lane != i1)
    ep2 = jnp.where(rest, ep, -1.0)
    p2 = jnp.max(ep2, axis=-1, keepdims=True)
    i2 = jnp.min(jnp.where(rest & (ep2 == p2), lane, none), axis=-1, keepdims=True)
    den = p1 + p2
    comb = jnp.where(lane == i1, g_w * (p1 / den), 0.0)
    comb = jnp.where(lane == i2, g_w * (p2 / den), comb)
    comb_ref[...] = jnp.where(lane_i == GROUP_LANE, g_idx, comb)


def _out_proj(a2, r2, x2, wo_b, g1, b1, w_route, b_route):
    row = lambda w: pl.BlockSpec((OUT_TM, w), lambda i: (i, 0))
    full = lambda s: pl.BlockSpec(s, lambda i: (0, 0))
    return pl.pallas_call(
        _out_kernel,
        grid=(N_TOK // OUT_TM,),
        in_specs=[row(WIDTH), row(WIDTH), row(D_MODEL), full((D_MODEL, D_MODEL)),
                  full((1, D_MODEL)), full((1, D_MODEL)), full((D_MODEL, LANES)), full((1, LANES))],
        out_specs=[row(D_MODEL), row(LANES)],
        out_shape=[jax.ShapeDtypeStruct((N_TOK, D_MODEL), F32),
                   jax.ShapeDtypeStruct((N_TOK, LANES), F32)],
        compiler_params=pltpu.CompilerParams(
            dimension_semantics=("arbitrary",), vmem_limit_bytes=VMEM_LIMIT),
        name="out_proj_ln_route",
    )(a2, r2, x2, wo_b, g1, b1, w_route, b_route)


def _moe_kernel(x1_ref, comb_ref, wg_ref, wu_ref, wd_ref, g_ref, b_ref, o_ref,
                pt_ref, xs_ref, cs_ref, acc_ref, win_ref):
    step = pl.program_id(1)
    assert MOE_ALIGN == 1 << 7 and EXPERTS_PER_GROUP == 1 << 2
    assert EXPERTS_PER_GROUP % MOE_EXPERTS_PER_STEP == 0

    @pl.when(step == 0)
    def _():
        comb = comb_ref[...]
        lane = lax.broadcasted_iota(jnp.int32, comb.shape, 1)
        lane_f = lane.astype(F32)
        gid = jnp.sum(jnp.where(lane == GROUP_LANE, comb, 0.0), axis=-1, keepdims=True)
        mine = lane_f == gid
        ti = lax.broadcasted_iota(jnp.int32, (MOE_TM, MOE_TM), 0)
        tj = lax.broadcasted_iota(jnp.int32, (MOE_TM, MOE_TM), 1)
        tril = jnp.where(ti >= tj, 1.0, 0.0).astype(BF16)
        incl = jnp.dot(tril, jnp.where(mine, 1.0, 0.0).astype(BF16),
                       preferred_element_type=F32)
        rank = jnp.sum(jnp.where(mine, incl, 0.0), axis=-1, keepdims=True) - 1.0
        counts = incl[MOE_TM - 1:MOE_TM, :]
        lane_row = lane[0:1, :]
        start = jnp.int32(0)
        base = jnp.zeros_like(gid)
        for g in range(N_GROUPS):
            cnt = jnp.sum(jnp.where(lane_row == g, counts, 0.0)).astype(jnp.int32)
            chunks = (cnt + (MOE_ALIGN - 1)) >> 7
            win_ref[0, g] = start
            win_ref[1, g] = chunks >> 1
            win_ref[2, g] = chunks & 1
            base = jnp.where(gid == float(g), start.astype(F32), base)
            start = start + chunks * MOE_ALIGN
        pos = base + rank
        r_iota = lax.broadcasted_iota(jnp.int32, (MOE_TM, MOE_ROWS), 1).astype(F32)
        pt = jnp.where(r_iota == pos, 1.0, 0.0).astype(BF16)
        pt_ref[...] = pt
        xs_ref[...] = _tn_dot(pt, x1_ref[...].astype(BF16)).astype(BF16)
        c_hi = comb.astype(BF16)
        c_lo = (comb - c_hi.astype(F32)).astype(BF16)
        cs_ref[...] = _tn_dot(pt, c_hi) + _tn_dot(pt, c_lo)
        acc_ref[...] = jnp.zeros_like(acc_ref)

    e0 = step * MOE_EXPERTS_PER_STEP
    g = e0 >> 2
    row0 = win_ref[0, g]
    n_full = win_ref[1, g]

    def window(start, rows):
        r0 = pl.multiple_of(start, MOE_ALIGN)
        xw = xs_ref[pl.ds(r0, rows), :]
        csw = cs_ref[pl.ds(r0, rows), :]
        lane_w = lax.broadcasted_iota(jnp.int32, (rows, LANES), 1)
        y = None
        for j in range(MOE_EXPERTS_PER_STEP):
            hg = jnp.dot(xw, wg_ref[j], preferred_element_type=F32)
            hu = jnp.dot(xw, wu_ref[j], preferred_element_type=F32)
            h = (hg * _sigmoid(hg) * hu).astype(BF16)
            c = jnp.sum(jnp.where(lane_w == e0 + j, csw, 0.0), axis=-1, keepdims=True)
            yj = c * jnp.dot(h, wd_ref[j], preferred_element_type=F32)
            y = yj if y is None else y + yj
        acc_ref[pl.ds(r0, rows), :] += y

    def full_window(k, _):
        window(row0 + k * MOE_WIN, MOE_WIN)
        return 0

    lax.fori_loop(0, n_full, full_window, 0)

    @pl.when(win_ref[2, g] == 1)
    def _():
        window(row0 + n_full * MOE_WIN, MOE_ALIGN)

    @pl.when(step == N_EXPERTS // MOE_EXPERTS_PER_STEP - 1)
    def _():
        moe = jnp.dot(pt_ref[...], acc_ref[...].astype(BF16), preferred_element_type=F32)
        o_ref[...] = _layer_norm(ALPHA * x1_ref[...] + moe, g_ref[...], b_ref[...])


def _moe(x1, comb, wg_b, wu_b, wd_b, g2, b2):
    return pl.pallas_call(
        _moe_kernel,
        grid=(N_TOK // MOE_TM, N_EXPERTS // MOE_EXPERTS_PER_STEP),
        in_specs=[
            pl.BlockSpec((MOE_TM, D_MODEL), lambda i, e: (i, 0)),
            pl.BlockSpec((MOE_TM, LANES), lambda i, e: (i, 0)),
            pl.BlockSpec((MOE_EXPERTS_PER_STEP, D_MODEL, D_EXPERT), lambda i, e: (e, 0, 0)),
            pl.BlockSpec((MOE_EXPERTS_PER_STEP, D_MODEL, D_EXPERT), lambda i, e: (e, 0, 0)),
            pl.BlockSpec((MOE_EXPERTS_PER_STEP, D_EXPERT, D_MODEL), lambda i, e: (e, 0, 0)),
            pl.BlockSpec((1, D_MODEL), lambda i, e: (0, 0)),
            pl.BlockSpec((1, D_MODEL), lambda i, e: (0, 0)),
        ],
        out_specs=pl.BlockSpec((MOE_TM, D_MODEL), lambda i, e: (i, 0)),
        out_shape=jax.ShapeDtypeStruct((N_TOK, D_MODEL), F32),
        scratch_shapes=[
            pltpu.VMEM((MOE_TM, MOE_ROWS), BF16),
            pltpu.VMEM((MOE_ROWS, D_MODEL), BF16),
            pltpu.VMEM((MOE_ROWS, LANES), F32),
            pltpu.VMEM((MOE_ROWS, D_MODEL), F32),
            pltpu.SMEM((3, N_GROUPS), jnp.int32),
        ],
        compiler_params=pltpu.CompilerParams(
            dimension_semantics=("arbitrary", "arbitrary"), vmem_limit_bytes=MOE_VMEM_LIMIT),
        name="moe_experts_ln",
    )(x1, comb, wg_b, wu_b, wd_b, g2, b2)


def kernel(x, w_in, t5_bias, hgrn_lb_logits, hgrn_norm_w, w_o, ln1_g, ln1_b, w_group, b_group,
           w_expert, b_expert, w_gate, w_up, w_down, ln2_g, ln2_b):
    x2 = x.reshape(N_TOK, D_MODEL)
    w_in_b = w_in[0].astype(BF16)
    w_qt_b = w_in_b[:, 0:WIDTH].T
    w_vt_b = w_in_b[:, 2 * WIDTH:3 * WIDTH].T
    proj, f_logits, qt3, vt3 = _proj(x2, w_in_b, w_qt_b, w_vt_b)
    proj3 = proj.reshape(BATCH, SEQ, ROW_WIDTH)

    a = _attention(t5_bias, qt3, proj3, vt3, _bias_tiles(t5_bias))
    r = _hgrn(proj3, f_logits.reshape(BATCH, SEQ, WIDTH), hgrn_lb_logits, hgrn_norm_w)

    pad = LANES - N_EXPERTS - N_GROUPS
    w_route = jnp.concatenate(
        [w_expert[0].transpose(1, 0, 2).reshape(D_MODEL, N_EXPERTS), w_group[0],
         jnp.zeros((D_MODEL, pad), F32)], axis=1)
    b_route = jnp.concatenate(
        [b_expert[0].reshape(N_EXPERTS), b_group[0], jnp.zeros((pad,), F32)]).reshape(1, LANES)
    x1, comb = _out_proj(
        a.reshape(N_TOK, WIDTH), r.reshape(N_TOK, WIDTH), x2, w_o[0].astype(BF16),
        ln1_g, ln1_b, w_route, b_route)

    out = _moe(x1, comb, w_gate[0].astype(BF16), w_up[0].astype(BF16),
               w_down[0].astype(BF16), ln2_g, ln2_b)
    return out.reshape(BATCH, SEQ, D_MODEL)
```

```python
import functools
import math

import jax
import jax.numpy as jnp
from jax import lax
from jax.experimental import pallas as pl
from jax.experimental.pallas import tpu as pltpu

F32 = jnp.float32
BF16 = jnp.bfloat16

D_MODEL = 1024
BATCH = 2
SEQ = 8192
N_TOK = BATCH * SEQ
HEADS = 8
HEAD_DIM = 64
WIDTH = HEADS * HEAD_DIM
IN_WIDTH = 7 * WIDTH
ROW_WIDTH = 5 * WIDTH
COL_HQ, COL_HF, COL_HI, COL_HG = 1, 2, 3, 4
MOBA_BLOCK = 256
N_BLOCKS = SEQ // MOBA_BLOCK
MOBA_TOPK = 3
HGRN_CHUNK = 64
HGRN_SAFE_LOG2 = 100.0
N_BUCKETS = 32
MAX_DISTANCE = 128
N_GROUPS = 4
EXPERTS_PER_GROUP = 4
N_EXPERTS = 16
D_EXPERT = 512
LN_EPS = 1e-5
RMS_EPS = 1e-6
ALPHA = 2.0 ** 0.25
LOG2E = math.log2(math.e)

LANES = 128
BF16_SUBLANES = 16
NEG = -1e30
VMEM_LIMIT = 48 * 1024 * 1024
MOE_VMEM_LIMIT = 56 * 1024 * 1024

PROJ_TM = 1024
OUT_TM = 512
MOE_TM = 1024
MOE_ALIGN = 128
MOE_WIN = 256
MOE_EXPERTS_PER_STEP = 2
MOE_ROWS = MOE_TM + N_GROUPS * MOE_ALIGN
GROUP_LANE = N_EXPERTS
HGRN_ROWS = 256
Q_TILES = 2
FAR_BLOCKS = 2
FAR_KEYS = FAR_BLOCKS * MOBA_BLOCK
N_FAR_STEPS = N_BLOCKS // FAR_BLOCKS

BIAS_HI_LANE = N_BLOCKS
BIAS_LO_LANE = N_BLOCKS + 1


def _nt_dot(a, b):
    return lax.dot_general(a, b, (((1,), (1,)), ((), ())), preferred_element_type=F32)


def _tn_dot(a, b):
    return lax.dot_general(a, b, (((0,), (0,)), ((), ())), preferred_element_type=F32)


def _sigmoid(x):
    return 1.0 / (1.0 + jnp.exp2(x * (-LOG2E)))


def _layer_norm(y, g, b):
    mu = jnp.mean(y, axis=-1, keepdims=True)
    d = y - mu
    var = jnp.mean(d * d, axis=-1, keepdims=True)
    return d * lax.rsqrt(var + LN_EPS) * g + b


def _bias_tile_kernel(tab_ref, out_ref):
    h = pl.program_id(0)
    ki = lax.broadcasted_iota(jnp.int32, (MOBA_BLOCK, MOBA_BLOCK), 0)
    qi = lax.broadcasted_iota(jnp.int32, (MOBA_BLOCK, MOBA_BLOCK), 1)
    max_exact = N_BUCKETS // 2
    for w in range(2):
        n = jnp.maximum(qi - ki + MOBA_BLOCK * w, 0)
        nf = jnp.maximum(n, 1).astype(F32)
        scaled = (jnp.log(nf / max_exact) / math.log(MAX_DISTANCE / max_exact)
                  * (N_BUCKETS - max_exact))
        large = max_exact + jnp.floor(jnp.maximum(scaled, 0.0)).astype(jnp.int32)
        large = jnp.minimum(large, N_BUCKETS - 1)
        bucket = jnp.where(n < max_exact, n, large)
        acc = jnp.zeros((MOBA_BLOCK, MOBA_BLOCK), F32)
        for bk in range(N_BUCKETS):
            acc = jnp.where(bucket == bk, tab_ref[bk, h], acc)
        out_ref[0, w] = acc


def _bias_tiles(t5_bias):
    return pl.pallas_call(
        _bias_tile_kernel,
        grid=(HEADS,),
        in_specs=[pl.BlockSpec(memory_space=pltpu.SMEM)],
        out_specs=pl.BlockSpec((1, 2, MOBA_BLOCK, MOBA_BLOCK), lambda h: (h, 0, 0, 0)),
        out_shape=jax.ShapeDtypeStruct((HEADS, 2, MOBA_BLOCK, MOBA_BLOCK), F32),
        name="t5_bias_tiles",
    )(t5_bias)


def _proj_kernel(x_ref, w_ref, wqt_ref, wvt_ref, o_ref, f_ref, qt_ref, vt_ref, xb_ref):
    n = pl.program_id(1)

    def feature_major(wt_ref, t_ref):
        t = _nt_dot(wt_ref[...], xb_ref[...]).astype(BF16)
        for c in range(PROJ_TM // MOBA_BLOCK):
            t_ref[c] = t[:, c * MOBA_BLOCK:(c + 1) * MOBA_BLOCK]

    @pl.when(n == 0)
    def _():
        xb_ref[...] = x_ref[...].astype(BF16)
        feature_major(wqt_ref, qt_ref)

    acc = jnp.dot(xb_ref[...], w_ref[...], preferred_element_type=F32)
    o_ref[...] = acc.astype(BF16)

    @pl.when(n == 1)
    def _():
        feature_major(wvt_ref, vt_ref)

    @pl.when(n == COL_HF)
    def _():
        f_ref[...] = acc


def _proj(x2, w_in_b, w_qt_b, w_vt_b):
    tiles_per_seq = SEQ // PROJ_TM
    blocks_per_tile = PROJ_TM // MOBA_BLOCK
    t_spec = pl.BlockSpec((None, blocks_per_tile, WIDTH, MOBA_BLOCK),
                          lambda i, n: (i // tiles_per_seq, i % tiles_per_seq, 0, 0))
    t_shape = jax.ShapeDtypeStruct((BATCH, N_BLOCKS, WIDTH, MOBA_BLOCK), BF16)
    return pl.pallas_call(
        _proj_kernel,
        grid=(N_TOK // PROJ_TM, ROW_WIDTH // WIDTH),
        in_specs=[
            pl.BlockSpec((PROJ_TM, D_MODEL), lambda i, n: (i, 0)),
            pl.BlockSpec((D_MODEL, WIDTH), lambda i, n: (0, jnp.where(n == 0, 1, n + 2))),
            pl.BlockSpec((WIDTH, D_MODEL), lambda i, n: (0, 0)),
            pl.BlockSpec((WIDTH, D_MODEL), lambda i, n: (0, 0)),
        ],
        out_specs=[
            pl.BlockSpec((PROJ_TM, WIDTH), lambda i, n: (i, n)),
            pl.BlockSpec((PROJ_TM, WIDTH), lambda i, n: (i, 0)),
            t_spec,
            t_spec,
        ],
        out_shape=[
            jax.ShapeDtypeStruct((N_TOK, ROW_WIDTH), BF16),
            jax.ShapeDtypeStruct((N_TOK, WIDTH), F32),
            t_shape,
            t_shape,
        ],
        scratch_shapes=[pltpu.VMEM((PROJ_TM, D_MODEL), BF16)],
        compiler_params=pltpu.CompilerParams(
            dimension_semantics=("arbitrary", "arbitrary"), vmem_limit_bytes=VMEM_LIMIT),
        name="in_proj",
    )(x2, w_in_b, w_qt_b, w_vt_b)


def _colreduce(x, op):
    r = x[0:8]
    for i in range(1, x.shape[0] // 8):
        r = op(r, x[8 * i:8 * i + 8])
    for shift in (4, 2, 1):
        r = op(r, pltpu.roll(r, shift, 0))
    return r


def _attn_kernel(tab_ref, qt_ref, k_ref, vt_ref, bias_ref, o_ref,
                 kmh_ref, kml_ref, ext_ref, s_ref, near_ref):
    assert MOBA_BLOCK == 1 << 8
    pair = pl.program_id(1)
    step_i = pl.program_id(2)
    lane = lax.broadcasted_iota(jnp.int32, (MOBA_BLOCK, LANES), 1)

    @pl.when(step_i == 0)
    def _():
        r = lax.broadcasted_iota(jnp.int32, (LANES, SEQ), 1)
        nb = lax.broadcasted_iota(jnp.int32, (LANES, SEQ), 0)
        avg = jnp.where((r >> 8) == nb, 1.0 / MOBA_BLOCK, 0.0).astype(BF16)
        km = jnp.dot(avg, k_ref[...], preferred_element_type=F32)
        hi = km.astype(BF16)
        kmh_ref[...] = hi
        kml_ref[...] = (km - hi.astype(F32)).astype(BF16)
        kr = lax.broadcasted_iota(jnp.int32, (SEQ, LANES), 0) >> 8
        kl = lax.broadcasted_iota(jnp.int32, (SEQ, LANES), 1)
        ext_ref[...] = jnp.where((kl == kr) | (kl == BIAS_HI_LANE) | (kl == BIAS_LO_LANE),
                                 1.0, 0.0).astype(BF16)

    krow = lax.broadcasted_iota(jnp.int32, (MOBA_BLOCK, MOBA_BLOCK), 0)
    qcol = lax.broadcasted_iota(jnp.int32, (MOBA_BLOCK, MOBA_BLOCK), 1)
    frow = lax.broadcasted_iota(jnp.int32, (LANES, MOBA_BLOCK), 0)
    nrow = lax.broadcasted_iota(jnp.int32, (N_BLOCKS, MOBA_BLOCK), 0)
    nrow_f = nrow.astype(F32)
    brow = lax.broadcasted_iota(jnp.int32, (8, MOBA_BLOCK), 0)
    ones_rows = jnp.ones((BF16_SUBLANES, MOBA_BLOCK), BF16)
    zero_rows = jnp.zeros((LANES - N_BLOCKS - 8, MOBA_BLOCK), F32)
    km_hi = kmh_ref[0:N_BLOCKS, :]
    km_lo = kml_ref[0:N_BLOCKS, :]

    def v_aug(hh, block):
        vt = vt_ref[block, hh * HEAD_DIM:(hh + 1) * HEAD_DIM, :]
        return jnp.concatenate([vt, ones_rows], axis=0)

    def rows_to_tile(x8):
        return jnp.concatenate([x8] * (N_BLOCKS // 8), axis=0)

    q_far = []
    prev_masks = []
    carry = []
    for t in range(Q_TILES):
        own = step_i * Q_TILES + t
        prev = jnp.maximum(own - 1, 0)
        qt2 = qt_ref[t]
        k_own = k_ref[pl.ds(pl.multiple_of(own * MOBA_BLOCK, MOBA_BLOCK), MOBA_BLOCK), :]
        k_prev = k_ref[pl.ds(pl.multiple_of(prev * MOBA_BLOCK, MOBA_BLOCK), MOBA_BLOCK), :]
        for hh in range(2):
            head = pair * 2 + hh
            in_head = (frow >= hh * HEAD_DIM) & (frow < (hh + 1) * HEAD_DIM)
            qs = jnp.where(in_head, qt2, jnp.zeros_like(qt2)) * jnp.asarray(HEAD_DIM ** -0.5, BF16)

            gate = (jnp.dot(km_hi, qs, preferred_element_type=F32)
                    + jnp.dot(km_lo, qs, preferred_element_type=F32))
            eligible = nrow < own
            g = jnp.where(eligible, gate, -jnp.inf)
            sel = jnp.zeros((N_BLOCKS, MOBA_BLOCK), jnp.bool_)
            for _ in range(MOBA_TOPK):
                mx = rows_to_tile(_colreduce(g, jnp.maximum))
                idx = rows_to_tile(_colreduce(jnp.where(g == mx, nrow_f, float(LANES)),
                                              jnp.minimum))
                hit = nrow_f == idx
                sel = sel | hit
                g = jnp.where(hit, -jnp.inf, g)
            sel = sel & eligible

            b31 = jnp.full((8, MOBA_BLOCK), tab_ref[N_BUCKETS - 1, head], F32)
            b31_hi = b31.astype(BF16).astype(F32)
            bias_rows = jnp.where(brow == 0, b31_hi, jnp.where(brow == 1, b31 - b31_hi, 0.0))

            aug = jnp.concatenate([jnp.where(sel & (nrow != own - 1), 0.0, NEG), bias_rows,
                                   zero_rows], axis=0)
            q_far.append(jnp.concatenate([qs, aug.astype(BF16)], axis=0))

            prev_sel = _colreduce(jnp.where((nrow == own - 1) & jnp.logical_not(sel), NEG, 0.0),
                                  jnp.add)
            prev_masks.append(jnp.where(own >= 1, prev_sel[0:1], NEG))
            near_ref[2 * t + hh, 0:MOBA_BLOCK, :] = (
                jnp.dot(k_prev, qs, preferred_element_type=F32) + bias_ref[hh, 1])
            s_own = jnp.dot(k_own, qs, preferred_element_type=F32) + bias_ref[hh, 0]
            near_ref[2 * t + hh, MOBA_BLOCK:, :] = jnp.where(qcol >= krow, s_own, NEG)
            carry += [jnp.full((1, MOBA_BLOCK), NEG, F32),
                      jnp.zeros((HEAD_DIM + BF16_SUBLANES, MOBA_BLOCK), F32)]

    q_cat = jnp.concatenate(q_far, axis=1)
    n_tiles = 2 * Q_TILES

    def absorb_scores(s, va, m, acc):
        m_new = jnp.maximum(m, jnp.max(s, axis=0, keepdims=True))
        alpha = jnp.exp(m - m_new)
        p = jnp.exp(s - m_new).astype(BF16)
        return m_new, alpha * acc + jnp.dot(va, p, preferred_element_type=F32)

    def scores(step, slot):
        start = pl.multiple_of(jnp.minimum(step, N_FAR_STEPS - 1) * FAR_KEYS, FAR_KEYS)
        k_aug = jnp.concatenate([k_ref[pl.ds(start, FAR_KEYS), :],
                                 ext_ref[pl.ds(start, FAR_KEYS), :]], axis=1)
        s_ref[slot] = jnp.dot(k_aug, q_cat, preferred_element_type=F32)

    def absorb(step, slot, carry):
        block0 = step * FAR_BLOCKS
        out = []
        for j in range(n_tiles):
            hh = j % 2
            s = s_ref[slot, :, j * MOBA_BLOCK:(j + 1) * MOBA_BLOCK]
            va = jnp.concatenate([v_aug(hh, block0 + c) for c in range(FAR_BLOCKS)], axis=1)
            out += absorb_scores(s, va, carry[2 * j], carry[2 * j + 1])
        return tuple(out)

    n_steps = step_i
    scores(0, 0)

    def two_steps(base, carry):
        scores(base + 1, 1)
        carry = absorb(base, 0, carry)
        scores(base + 2, 0)
        return absorb(base + 1, 1, carry)

    def far_body(i, carry):
        return two_steps(4 * i + 2, two_steps(4 * i, carry))

    n_quads = n_steps // 4
    carry = lax.fori_loop(0, n_quads, far_body, tuple(carry))
    carry = lax.cond(n_steps % 4 >= 2,
                     lambda c: two_steps(4 * n_quads, c), lambda c: c, carry)
    carry = lax.cond(n_steps % 2 == 1,
                     lambda c: absorb(n_steps - 1, 0, c), lambda c: c, carry)

    for t in range(Q_TILES):
        own = step_i * Q_TILES + t
        outs = []
        for hh in range(2):
            j = 2 * t + hh
            va = jnp.concatenate([v_aug(hh, jnp.maximum(own - 1, 0)), v_aug(hh, own)], axis=1)
            s = jnp.concatenate([near_ref[j, 0:MOBA_BLOCK, :] + prev_masks[j],
                                 near_ref[j, MOBA_BLOCK:, :]], axis=0)
            _, acc = absorb_scores(s, va, carry[2 * j], carry[2 * j + 1])
            outs.append(acc[0:HEAD_DIM] / acc[HEAD_DIM:HEAD_DIM + 1])
        o_ref[t * MOBA_BLOCK:(t + 1) * MOBA_BLOCK, :] = (
            jnp.concatenate(outs, axis=0).T.astype(BF16))


def _attention(t5_bias, qt3, proj3, vt3, bias_tiles):
    n_pairs = HEADS // 2
    assert FAR_BLOCKS == Q_TILES
    return pl.pallas_call(
        _attn_kernel,
        grid=(BATCH, n_pairs, N_BLOCKS // Q_TILES),
        in_specs=[
            pl.BlockSpec(memory_space=pltpu.SMEM),
            pl.BlockSpec((None, Q_TILES, LANES, MOBA_BLOCK), lambda b, p, i: (b, i, p, 0)),
            pl.BlockSpec((None, SEQ, LANES), lambda b, p, i: (b, 0, p)),
            pl.BlockSpec((None, N_BLOCKS, LANES, MOBA_BLOCK), lambda b, p, i: (b, 0, p, 0)),
            pl.BlockSpec((2, 2, MOBA_BLOCK, MOBA_BLOCK), lambda b, p, i: (p, 0, 0, 0)),
        ],
        out_specs=pl.BlockSpec((None, Q_TILES * MOBA_BLOCK, LANES), lambda b, p, i: (b, i, p)),
        out_shape=jax.ShapeDtypeStruct((BATCH, SEQ, WIDTH), BF16),
        scratch_shapes=[
            pltpu.VMEM((LANES, LANES), BF16),
            pltpu.VMEM((LANES, LANES), BF16),
            pltpu.VMEM((SEQ, LANES), BF16),
            pltpu.VMEM((2, FAR_KEYS, 2 * Q_TILES * MOBA_BLOCK), F32),
            pltpu.VMEM((2 * Q_TILES, 2 * MOBA_BLOCK, MOBA_BLOCK), F32),
        ],
        compiler_params=pltpu.CompilerParams(
            dimension_semantics=("arbitrary", "arbitrary", "arbitrary"),
            vmem_limit_bytes=VMEM_LIMIT),
        name="moba_attention",
    )(t5_bias, qt3, proj3, vt3, bias_tiles)


def _hgrn_kernel(q_ref, f_ref, i_ref, g_ref, lbl_ref, nw_ref, o_ref,
                 state_ref, bpad_ref, kpad_ref, vpad_ref, shb_ref, shk_ref, shv_ref,
                 ones_ref, mask_ref):
    C = HGRN_CHUNK
    half = WIDTH // 2
    assert HEAD_DIM == 1 << 6

    @pl.when(pl.program_id(1) == 0)
    def _():
        state_ref[...] = jnp.zeros_like(state_ref)
        sr = lax.broadcasted_iota(jnp.int32, (half, half), 0) >> 6
        sc = lax.broadcasted_iota(jnp.int32, (half, half), 1) >> 6
        same = jnp.where(sr == sc, 1.0, 0.0)
        ones_ref[...] = same.astype(BF16)
        mask_ref[...] = same

    zpad = jnp.zeros((8, WIDTH), F32)
    bpad_ref[0:8, :] = zpad
    kpad_ref[0:8, :] = zpad
    vpad_ref[0:8, :] = zpad

    l0 = lbl_ref[0:1, :]
    l1 = lbl_ref[1:2, :]
    lmx = jnp.maximum(l0, l1)
    e0 = jnp.exp(l0 - lmx)
    lb = e0 / (e0 + jnp.exp(l1 - lmx))
    nw = nw_ref[...]

    ri = lax.broadcasted_iota(jnp.int32, (C, C), 0)
    ci = lax.broadcasted_iota(jnp.int32, (C, C), 1)
    tril = jnp.where(ri >= ci, 1.0, 0.0).astype(BF16)

    def head_sum(t):
        return jnp.concatenate(
            [jnp.dot(t[:, 0:half], ones_ref[...], preferred_element_type=F32),
             jnp.dot(t[:, half:], ones_ref[...], preferred_element_type=F32)], axis=1)

    def prepare(r0):
        qq = q_ref[r0:r0 + C, :].astype(F32)
        qq = qq * _sigmoid(qq)
        f = lb + (1.0 - lb) * _sigmoid(f_ref[r0:r0 + C, :])
        gl = jnp.log(f)
        kk = 1.0 - f
        vv = i_ref[r0:r0 + C, :].astype(F32)
        g_hi = gl.astype(BF16)
        g_r = gl - g_hi.astype(F32)
        g_mid = g_r.astype(BF16)
        g_lo = (g_r - g_mid.astype(F32)).astype(BF16)
        b = (jnp.dot(tril, g_hi, preferred_element_type=F32)
             + jnp.dot(tril, g_mid, preferred_element_type=F32)
             + jnp.dot(tril, g_lo, preferred_element_type=F32))
        return qq, kk, vv, b * LOG2E

    def intra_pairs(qq, kk, vv, b):
        bpad_ref[8:8 + C, :] = b
        kpad_ref[8:8 + C, :] = kk
        vpad_ref[8:8 + C, :] = vv
        for dd in range(8):
            shb_ref[dd] = bpad_ref[8 - dd:8 - dd + C, :]
            shk_ref[dd] = kpad_ref[8 - dd:8 - dd + C, :]
            shv_ref[dd] = vpad_ref[8 - dd:8 - dd + C, :]

        o_parts = []
        for j in range(C // 8):
            rows = C - 8 * j
            bt = b[8 * j:, :]
            qt = qq[8 * j:, :]
            terms = []
            vs_all = []
            for dd in range(8):
                terms.append(qt * shk_ref[dd, 0:rows, :] * jnp.exp2(bt - shb_ref[dd, 0:rows, :]))
                vs_all.append(shv_ref[dd, 0:rows, :])
            a = head_sum(jnp.concatenate(terms, axis=0).astype(BF16))
            av = a * jnp.concatenate(vs_all, axis=0)
            part = av[0:rows]
            for dd in range(1, 8):
                part = part + av[dd * rows:(dd + 1) * rows]
            o_parts.append(part)
        o = o_parts[0]
        for j in range(1, C // 8):
            o = o + jnp.concatenate([jnp.zeros((8 * j, WIDTH), F32), o_parts[j]], axis=0)
        return o

    pt_i = lax.broadcasted_iota(jnp.int32, (C, half), 0)
    ps_i = lax.broadcasted_iota(jnp.int32, (C, half), 1) & (C - 1)
    causal = jnp.where(ps_i <= pt_i, 1.0, 0.0)

    def intra_matmul(qq, kk, vv, b):
        qd = (qq * jnp.exp2(b)).astype(BF16)
        kd = (kk * jnp.exp2(-b)).astype(BF16)
        vb = vv.astype(BF16)
        outs = []
        for a_i in range(2):
            sl = slice(a_i * half, (a_i + 1) * half)
            k4 = jnp.concatenate([kd[:, sl]] * (half // C), axis=0) * ones_ref[...]
            v4 = jnp.concatenate([vb[:, sl]] * (half // C), axis=0) * ones_ref[...]
            pair = _nt_dot(qd[:, sl], k4) * causal
            outs.append(jnp.dot(pair.astype(BF16), v4, preferred_element_type=F32))
        return jnp.concatenate(outs, axis=1)

    def finish(r0, o, qq, kk, vv, b):
        qe = (qq * jnp.exp2(b)).astype(BF16)
        b_last = b[C - 1:C, :]
        kd = kk * jnp.exp2(b_last - b)
        dec = jnp.exp2(b_last)
        inter = []
        for a_i in range(2):
            sl = slice(a_i * half, (a_i + 1) * half)
            st = state_ref[a_i]
            inter.append(_nt_dot(qe[:, sl], st.astype(BF16)))
            upd = _tn_dot(vv[:, sl].astype(BF16), kd[:, sl].astype(BF16))
            state_ref[a_i] = st * dec[:, sl] + upd * mask_ref[...]
        o = o + jnp.concatenate(inter, axis=1)

        ms = head_sum((o * o).astype(BF16)) * (1.0 / HEAD_DIM)
        r = o * lax.rsqrt(ms + RMS_EPS) * nw
        r = r * _sigmoid(g_ref[r0:r0 + C, :].astype(F32))
        o_ref[r0:r0 + C, :] = r.astype(BF16)

    chunks = [prepare(u * C) for u in range(HGRN_ROWS // C)]
    lowest = chunks[0][3][C - 1:C, :]
    for ch in chunks[1:]:
        lowest = jnp.minimum(lowest, ch[3][C - 1:C, :])
    safe = jnp.min(lowest) > -HGRN_SAFE_LOG2

    def run(intra):
        def body():
            for u, (qq, kk, vv, b) in enumerate(chunks):
                finish(u * C, intra(qq, kk, vv, b), qq, kk, vv, b)
        return body

    lax.cond(safe, run(intra_matmul), run(intra_pairs))


def _hgrn(proj3, f3, lb_logits, norm_w):
    blk = lambda col: pl.BlockSpec((None, HGRN_ROWS, WIDTH), lambda b, i, col=col: (b, i, col))
    return pl.pallas_call(
        _hgrn_kernel,
        grid=(BATCH, SEQ // HGRN_ROWS),
        in_specs=[
            blk(COL_HQ),
            pl.BlockSpec((None, HGRN_ROWS, WIDTH), lambda b, i: (b, i, 0)),
            blk(COL_HI),
            blk(COL_HG),
            pl.BlockSpec((2, WIDTH), lambda b, i: (0, 0)),
            pl.BlockSpec((1, WIDTH), lambda b, i: (0, 0)),
        ],
        out_specs=pl.BlockSpec((None, HGRN_ROWS, WIDTH), lambda b, i: (b, i, 0)),
        out_shape=jax.ShapeDtypeStruct((BATCH, SEQ, WIDTH), BF16),
        scratch_shapes=[
            pltpu.VMEM((2, WIDTH // 2, WIDTH // 2), F32),
            pltpu.VMEM((8 + HGRN_CHUNK, WIDTH), F32),
            pltpu.VMEM((8 + HGRN_CHUNK, WIDTH), F32),
            pltpu.VMEM((8 + HGRN_CHUNK, WIDTH), F32),
            pltpu.VMEM((8, HGRN_CHUNK, WIDTH), F32),
            pltpu.VMEM((8, HGRN_CHUNK, WIDTH), F32),
            pltpu.VMEM((8, HGRN_CHUNK, WIDTH), F32),
            pltpu.VMEM((WIDTH // 2, WIDTH // 2), BF16),
            pltpu.VMEM((WIDTH // 2, WIDTH // 2), F32),
        ],
        compiler_params=pltpu.CompilerParams(
            dimension_semantics=("arbitrary", "arbitrary"), vmem_limit_bytes=VMEM_LIMIT),
        name="hgrn2",
    )(proj3, f3, proj3, proj3, lb_logits, norm_w)


def _out_kernel(a_ref, r_ref, x_ref, wo_ref, g_ref, b_ref, wr_ref, br_ref,
                x1_ref, comb_ref):
    mix = jnp.dot(a_ref[...], wo_ref[0:WIDTH, :], preferred_element_type=F32)
    mix = mix + jnp.dot(r_ref[...], wo_ref[WIDTH:, :], preferred_element_type=F32)
    x1 = _layer_norm(ALPHA * x_ref[...] + mix, g_ref[...], b_ref[...])
    x1_ref[...] = x1

    x_hi = x1.astype(BF16)
    x_lo = (x1 - x_hi.astype(F32)).astype(BF16)
    wr = wr_ref[...]
    w_hi = wr.astype(BF16)
    w_lo = (wr - w_hi.astype(F32)).astype(BF16)
    logits = (jnp.dot(x_hi, w_hi, preferred_element_type=F32)
              + jnp.dot(x_lo, w_hi, preferred_element_type=F32)
              + jnp.dot(x_hi, w_lo, preferred_element_type=F32)) + br_ref[...]
    assert EXPERTS_PER_GROUP == 1 << 2
    lane_i = lax.broadcasted_iota(jnp.int32, logits.shape, 1)
    lane = lane_i.astype(F32)
    grp_of_lane = (lane_i >> 2).astype(F32)
    none = float(LANES)
    is_g = (lane_i >= N_EXPERTS) & (lane_i < N_EXPERTS + N_GROUPS)
    gl = jnp.where(is_g, logits, -jnp.inf)
    ge = jnp.exp(gl - jnp.max(gl, axis=-1, keepdims=True))
    gp = ge / jnp.sum(ge, axis=-1, keepdims=True)
    g_w = jnp.max(gp, axis=-1, keepdims=True)
    g_lane = jnp.min(jnp.where(is_g & (gp == g_w), lane, none), axis=-1, keepdims=True)
    g_idx = g_lane - float(N_EXPERTS)

    in_grp = (lane_i < N_EXPERTS) & (grp_of_lane == g_idx)
    el = jnp.where(in_grp, logits, -jnp.inf)
    ee = jnp.exp(el - jnp.max(el, axis=-1, keepdims=True))
    ep = ee / jnp.sum(ee, axis=-1, keepdims=True)
    p1 = jnp.max(ep, axis=-1, keepdims=True)
    i1 = jnp.min(jnp.where(in_grp & (ep == p1), lane, none), axis=-1, keepdims=True)
    rest = in_grp & (lane != i1)
    ep2 = jnp.where(rest, ep, -1.0)
    p2 = jnp.max(ep2, axis=-1, keepdims=True)
    i2 = jnp.min(jnp.where(rest & (ep2 == p2), lane, none), axis=-1, keepdims=True)
    den = p1 + p2
    comb = jnp.where(lane == i1, g_w * (p1 / den), 0.0)
    comb = jnp.where(lane == i2, g_w * (p2 / den), comb)
    comb_ref[...] = jnp.where(lane_i == GROUP_LANE, g_idx, comb)


def _out_proj(a2, r2, x2, wo_b, g1, b1, w_route, b_route):
    row = lambda w: pl.BlockSpec((OUT_TM, w), lambda i: (i, 0))
    full = lambda s: pl.BlockSpec(s, lambda i: (0, 0))
    return pl.pallas_call(
        _out_kernel,
        grid=(N_TOK // OUT_TM,),
        in_specs=[row(WIDTH), row(WIDTH), row(D_MODEL), full((D_MODEL, D_MODEL)),
                  full((1, D_MODEL)), full((1, D_MODEL)), full((D_MODEL, LANES)), full((1, LANES))],
        out_specs=[row(D_MODEL), row(LANES)],
        out_shape=[jax.ShapeDtypeStruct((N_TOK, D_MODEL), F32),
                   jax.ShapeDtypeStruct((N_TOK, LANES), F32)],
        compiler_params=pltpu.CompilerParams(
            dimension_semantics=("arbitrary",), vmem_limit_bytes=VMEM_LIMIT),
        name="out_proj_ln_route",
    )(a2, r2, x2, wo_b, g1, b1, w_route, b_route)


def _moe_kernel(x1_ref, comb_ref, wg_ref, wu_ref, wd_ref, g_ref, b_ref, o_ref,
                pt_ref, xs_ref, cs_ref, acc_ref, win_ref):
    step = pl.program_id(1)
    assert MOE_ALIGN == 1 << 7 and EXPERTS_PER_GROUP == 1 << 2
    assert EXPERTS_PER_GROUP % MOE_EXPERTS_PER_STEP == 0

    @pl.when(step == 0)
    def _():
        comb = comb_ref[...]
        lane = lax.broadcasted_iota(jnp.int32, comb.shape, 1)
        lane_f = lane.astype(F32)
        gid = jnp.sum(jnp.where(lane == GROUP_LANE, comb, 0.0), axis=-1, keepdims=True)
        mine = lane_f == gid
        ti = lax.broadcasted_iota(jnp.int32, (MOE_TM, MOE_TM), 0)
        tj = lax.broadcasted_iota(jnp.int32, (MOE_TM, MOE_TM), 1)
        tril = jnp.where(ti >= tj, 1.0, 0.0).astype(BF16)
        incl = jnp.dot(tril, jnp.where(mine, 1.0, 0.0).astype(BF16),
                       preferred_element_type=F32)
        rank = jnp.sum(jnp.where(mine, incl, 0.0), axis=-1, keepdims=True) - 1.0
        counts = incl[MOE_TM - 1:MOE_TM, :]
        lane_row = lane[0:1, :]
        start = jnp.int32(0)
        base = jnp.zeros_like(gid)
        for g in range(N_GROUPS):
            cnt = jnp.sum(jnp.where(lane_row == g, counts, 0.0)).astype(jnp.int32)
            chunks = (cnt + (MOE_ALIGN - 1)) >> 7
            win_ref[0, g] = start
            win_ref[1, g] = chunks >> 1
            win_ref[2, g] = chunks & 1
            base = jnp.where(gid == float(g), start.astype(F32), base)
            start = start + chunks * MOE_ALIGN
        pos = base + rank
        r_iota = lax.broadcasted_iota(jnp.int32, (MOE_TM, MOE_ROWS), 1).astype(F32)
        pt = jnp.where(r_iota == pos, 1.0, 0.0).astype(BF16)
        pt_ref[...] = pt
        xs_ref[...] = _tn_dot(pt, x1_ref[...].astype(BF16)).astype(BF16)
        c_hi = comb.astype(BF16)
        c_lo = (comb - c_hi.astype(F32)).astype(BF16)
        cs_ref[...] = _tn_dot(pt, c_hi) + _tn_dot(pt, c_lo)
        acc_ref[...] = jnp.zeros_like(acc_ref)

    e0 = step * MOE_EXPERTS_PER_STEP
    g = e0 >> 2
    row0 = win_ref[0, g]
    n_full = win_ref[1, g]

    def window(start, rows):
        r0 = pl.multiple_of(start, MOE_ALIGN)
        xw = xs_ref[pl.ds(r0, rows), :]
        csw = cs_ref[pl.ds(r0, rows), :]
        lane_w = lax.broadcasted_iota(jnp.int32, (rows, LANES), 1)
        y = None
        for j in range(MOE_EXPERTS_PER_STEP):
            hg = jnp.dot(xw, wg_ref[j], preferred_element_type=F32)
            hu = jnp.dot(xw, wu_ref[j], preferred_element_type=F32)
            h = (hg * _sigmoid(hg) * hu).astype(BF16)
            c = jnp.sum(jnp.where(lane_w == e0 + j, csw, 0.0), axis=-1, keepdims=True)
            yj = c * jnp.dot(h, wd_ref[j], preferred_element_type=F32)
            y = yj if y is None else y + yj
        acc_ref[pl.ds(r0, rows), :] += y

    def full_window(k, _):
        window(row0 + k * MOE_WIN, MOE_WIN)
        return 0

    lax.fori_loop(0, n_full, full_window, 0)

    @pl.when(win_ref[2, g] == 1)
    def _():
        window(row0 + n_full * MOE_WIN, MOE_ALIGN)

    @pl.when(step == N_EXPERTS // MOE_EXPERTS_PER_STEP - 1)
    def _():
        moe = jnp.dot(pt_ref[...], acc_ref[...].astype(BF16), preferred_element_type=F32)
        o_ref[...] = _layer_norm(ALPHA * x1_ref[...] + moe, g_ref[...], b_ref[...])


def _moe(x1, comb, wg_b, wu_b, wd_b, g2, b2):
    return pl.pallas_call(
        _moe_kernel,
        grid=(N_TOK // MOE_TM, N_EXPERTS // MOE_EXPERTS_PER_STEP),
        in_specs=[
            pl.BlockSpec((MOE_TM, D_MODEL), lambda i, e: (i, 0)),
            pl.BlockSpec((MOE_TM, LANES), lambda i, e: (i, 0)),
            pl.BlockSpec((MOE_EXPERTS_PER_STEP, D_MODEL, D_EXPERT), lambda i, e: (e, 0, 0)),
            pl.BlockSpec((MOE_EXPERTS_PER_STEP, D_MODEL, D_EXPERT), lambda i, e: (e, 0, 0)),
            pl.BlockSpec((MOE_EXPERTS_PER_STEP, D_EXPERT, D_MODEL), lambda i, e: (e, 0, 0)),
            pl.BlockSpec((1, D_MODEL), lambda i, e: (0, 0)),
            pl.BlockSpec((1, D_MODEL), lambda i, e: (0, 0)),
        ],
        out_specs=pl.BlockSpec((MOE_TM, D_MODEL), lambda i, e: (i, 0)),
        out_shape=jax.ShapeDtypeStruct((N_TOK, D_MODEL), F32),
        scratch_shapes=[
            pltpu.VMEM((MOE_TM, MOE_ROWS), BF16),
            pltpu.VMEM((MOE_ROWS, D_MODEL), BF16),
            pltpu.VMEM((MOE_ROWS, LANES), F32),
            pltpu.VMEM((MOE_ROWS, D_MODEL), F32),
            pltpu.SMEM((3, N_GROUPS), jnp.int32),
        ],
        compiler_params=pltpu.CompilerParams(
            dimension_semantics=("arbitrary", "arbitrary"), vmem_limit_bytes=MOE_VMEM_LIMIT),
        name="moe_experts_ln",
    )(x1, comb, wg_b, wu_b, wd_b, g2, b2)


def kernel(x, w_in, t5_bias, hgrn_lb_logits, hgrn_norm_w, w_o, ln1_g, ln1_b, w_group, b_group,
           w_expert, b_expert, w_gate, w_up, w_down, ln2_g, ln2_b):
    x2 = x.reshape(N_TOK, D_MODEL)
    w_in_b = w_in[0].astype(BF16)
    w_qt_b = w_in_b[:, 0:WIDTH].T
    w_vt_b = w_in_b[:, 2 * WIDTH:3 * WIDTH].T
    proj, f_logits, qt3, vt3 = _proj(x2, w_in_b, w_qt_b, w_vt_b)
    proj3 = proj.reshape(BATCH, SEQ, ROW_WIDTH)

    a = _attention(t5_bias, qt3, proj3, vt3, _bias_tiles(t5_bias))
    r = _hgrn(proj3, f_logits.reshape(BATCH, SEQ, WIDTH), hgrn_lb_logits, hgrn_norm_w)

    pad = LANES - N_EXPERTS - N_GROUPS
    w_route = jnp.concatenate(
        [w_expert[0].transpose(1, 0, 2).reshape(D_MODEL, N_EXPERTS), w_group[0],
         jnp.zeros((D_MODEL, pad), F32)], axis=1)
    b_route = jnp.concatenate(
        [b_expert[0].reshape(N_EXPERTS), b_group[0], jnp.zeros((pad,), F32)]).reshape(1, LANES)
    x1, comb = _out_proj(
        a.reshape(N_TOK, WIDTH), r.reshape(N_TOK, WIDTH), x2, w_o[0].astype(BF16),
        ln1_g, ln1_b, w_route, b_route)

    out = _moe(x1, comb, w_gate[0].astype(BF16), w_up[0].astype(BF16),
               w_down[0].astype(BF16), ln2_g, ln2_b)
    return out.reshape(BATCH, SEQ, D_MODEL)
```

```python
import functools
import math

import jax
import jax.numpy as jnp
from jax import lax
from jax.experimental import pallas as pl
from jax.experimental.pallas import tpu as pltpu

F32 = jnp.float32
BF16 = jnp.bfloat16

D_MODEL = 1024
BATCH = 2
SEQ = 8192
N_TOK = BATCH * SEQ
HEADS = 8
HEAD_DIM = 64
WIDTH = HEADS * HEAD_DIM
IN_WIDTH = 7 * WIDTH
ROW_WIDTH = 5 * WIDTH
COL_HQ, COL_HF, COL_HI, COL_HG = 1, 2, 3, 4
MOBA_BLOCK = 256
N_BLOCKS = SEQ // MOBA_BLOCK
MOBA_TOPK = 3
HGRN_CHUNK = 64
HGRN_SAFE_LOG2 = 100.0
N_BUCKETS = 32
MAX_DISTANCE = 128
N_GROUPS = 4
EXPERTS_PER_GROUP = 4
N_EXPERTS = 16
D_EXPERT = 512
LN_EPS = 1e-5
RMS_EPS = 1e-6
ALPHA = 2.0 ** 0.25
LOG2E = math.log2(math.e)

LANES = 128
BF16_SUBLANES = 16
NEG = -1e30
VMEM_LIMIT = 48 * 1024 * 1024
MOE_VMEM_LIMIT = 56 * 1024 * 1024

PROJ_TM = 2048
OUT_TM = 1024
MOE_TM = 1024
MOE_ALIGN = 128
MOE_WIN = 256
MOE_EXPERTS_PER_STEP = 2
MOE_ROWS = MOE_TM + N_GROUPS * MOE_ALIGN
GROUP_LANE = N_EXPERTS
HGRN_ROWS = 512
Q_TILES = 2
FAR_BLOCKS = 2
FAR_KEYS = FAR_BLOCKS * MOBA_BLOCK
N_FAR_STEPS = N_BLOCKS // FAR_BLOCKS

BIAS_HI_LANE = N_BLOCKS
BIAS_LO_LANE = N_BLOCKS + 1


def _nt_dot(a, b):
    return lax.dot_general(a, b, (((1,), (1,)), ((), ())), preferred_element_type=F32)


def _tn_dot(a, b):
    return lax.dot_general(a, b, (((0,), (0,)), ((), ())), preferred_element_type=F32)


def _sigmoid(x):
    return 1.0 / (1.0 + jnp.exp2(x * (-LOG2E)))


def _layer_norm(y, g, b):
    mu = jnp.mean(y, axis=-1, keepdims=True)
    d = y - mu
    var = jnp.mean(d * d, axis=-1, keepdims=True)
    return d * lax.rsqrt(var + LN_EPS) * g + b


def _bias_tile_kernel(tab_ref, out_ref):
    h = pl.program_id(0)
    ki = lax.broadcasted_iota(jnp.int32, (MOBA_BLOCK, MOBA_BLOCK), 0)
    qi = lax.broadcasted_iota(jnp.int32, (MOBA_BLOCK, MOBA_BLOCK), 1)
    max_exact = N_BUCKETS // 2
    for w in range(2):
        n = jnp.maximum(qi - ki + MOBA_BLOCK * w, 0)
        nf = jnp.maximum(n, 1).astype(F32)
        scaled = (jnp.log(nf / max_exact) / math.log(MAX_DISTANCE / max_exact)
                  * (N_BUCKETS - max_exact))
        large = max_exact + jnp.floor(jnp.maximum(scaled, 0.0)).astype(jnp.int32)
        large = jnp.minimum(large, N_BUCKETS - 1)
        bucket = jnp.where(n < max_exact, n, large)
        acc = jnp.zeros((MOBA_BLOCK, MOBA_BLOCK), F32)
        for bk in range(N_BUCKETS):
            acc = jnp.where(bucket == bk, tab_ref[bk, h], acc)
        out_ref[0, w] = acc


def _bias_tiles(t5_bias):
    return pl.pallas_call(
        _bias_tile_kernel,
        grid=(HEADS,),
        in_specs=[pl.BlockSpec(memory_space=pltpu.SMEM)],
        out_specs=pl.BlockSpec((1, 2, MOBA_BLOCK, MOBA_BLOCK), lambda h: (h, 0, 0, 0)),
        out_shape=jax.ShapeDtypeStruct((HEADS, 2, MOBA_BLOCK, MOBA_BLOCK), F32),
        name="t5_bias_tiles",
    )(t5_bias)


def _proj_kernel(x_ref, w_ref, wqt_ref, wvt_ref, o_ref, f_ref, qt_ref, vt_ref, xb_ref):
    n = pl.program_id(1)

    def feature_major(wt_ref, t_ref):
        t = _nt_dot(wt_ref[...], xb_ref[...]).astype(BF16)
        for c in range(PROJ_TM // MOBA_BLOCK):
            t_ref[c] = t[:, c * MOBA_BLOCK:(c + 1) * MOBA_BLOCK]

    @pl.when(n == 0)
    def _():
        xb_ref[...] = x_ref[...].astype(BF16)
        feature_major(wqt_ref, qt_ref)

    acc = jnp.dot(xb_ref[...], w_ref[...], preferred_element_type=F32)
    o_ref[...] = acc.astype(BF16)

    @pl.when(n == 1)
    def _():
        feature_major(wvt_ref, vt_ref)

    @pl.when(n == COL_HF)
    def _():
        f_ref[...] = acc


def _proj(x2, w_in_b, w_qt_b, w_vt_b):
    tiles_per_seq = SEQ // PROJ_TM
    blocks_per_tile = PROJ_TM // MOBA_BLOCK
    t_spec = pl.BlockSpec((None, blocks_per_tile, WIDTH, MOBA_BLOCK),
                          lambda i, n: (i // tiles_per_seq, i % tiles_per_seq, 0, 0))
    t_shape = jax.ShapeDtypeStruct((BATCH, N_BLOCKS, WIDTH, MOBA_BLOCK), BF16)
    return pl.pallas_call(
        _proj_kernel,
        grid=(N_TOK // PROJ_TM, ROW_WIDTH // WIDTH),
        in_specs=[
            pl.BlockSpec((PROJ_TM, D_MODEL), lambda i, n: (i, 0)),
            pl.BlockSpec((D_MODEL, WIDTH), lambda i, n: (0, jnp.where(n == 0, 1, n + 2))),
            pl.BlockSpec((WIDTH, D_MODEL), lambda i, n: (0, 0)),
            pl.BlockSpec((WIDTH, D_MODEL), lambda i, n: (0, 0)),
        ],
        out_specs=[
            pl.BlockSpec((PROJ_TM, WIDTH), lambda i, n: (i, n)),
            pl.BlockSpec((PROJ_TM, WIDTH), lambda i, n: (i, 0)),
            t_spec,
            t_spec,
        ],
        out_shape=[
            jax.ShapeDtypeStruct((N_TOK, ROW_WIDTH), BF16),
            jax.ShapeDtypeStruct((N_TOK, WIDTH), F32),
            t_shape,
            t_shape,
        ],
        scratch_shapes=[pltpu.VMEM((PROJ_TM, D_MODEL), BF16)],
        compiler_params=pltpu.CompilerParams(
            dimension_semantics=("arbitrary", "arbitrary"), vmem_limit_bytes=VMEM_LIMIT),
        name="in_proj",
    )(x2, w_in_b, w_qt_b, w_vt_b)


def _colreduce(x, op):
    r = x[0:8]
    for i in range(1, x.shape[0] // 8):
        r = op(r, x[8 * i:8 * i + 8])
    for shift in (4, 2, 1):
        r = op(r, pltpu.roll(r, shift, 0))
    return r


def _attn_kernel(tab_ref, qt_ref, k_ref, vt_ref, bias_ref, o_ref,
                 kmh_ref, kml_ref, ext_ref, s_ref, near_ref):
    assert MOBA_BLOCK == 1 << 8
    pair = pl.program_id(1)
    step_i = pl.program_id(2)
    lane = lax.broadcasted_iota(jnp.int32, (MOBA_BLOCK, LANES), 1)

    @pl.when(step_i == 0)
    def _():
        r = lax.broadcasted_iota(jnp.int32, (LANES, SEQ), 1)
        nb = lax.broadcasted_iota(jnp.int32, (LANES, SEQ), 0)
        avg = jnp.where((r >> 8) == nb, 1.0 / MOBA_BLOCK, 0.0).astype(BF16)
        km = jnp.dot(avg, k_ref[...], preferred_element_type=F32)
        hi = km.astype(BF16)
        kmh_ref[...] = hi
        kml_ref[...] = (km - hi.astype(F32)).astype(BF16)
        kr = lax.broadcasted_iota(jnp.int32, (SEQ, LANES), 0) >> 8
        kl = lax.broadcasted_iota(jnp.int32, (SEQ, LANES), 1)
        ext_ref[...] = jnp.where((kl == kr) | (kl == BIAS_HI_LANE) | (kl == BIAS_LO_LANE),
                                 1.0, 0.0).astype(BF16)

    krow = lax.broadcasted_iota(jnp.int32, (MOBA_BLOCK, MOBA_BLOCK), 0)
    qcol = lax.broadcasted_iota(jnp.int32, (MOBA_BLOCK, MOBA_BLOCK), 1)
    frow = lax.broadcasted_iota(jnp.int32, (LANES, MOBA_BLOCK), 0)
    nrow = lax.broadcasted_iota(jnp.int32, (N_BLOCKS, MOBA_BLOCK), 0)
    nrow_f = nrow.astype(F32)
    brow = lax.broadcasted_iota(jnp.int32, (8, MOBA_BLOCK), 0)
    ones_rows = jnp.ones((BF16_SUBLANES, MOBA_BLOCK), BF16)
    zero_rows = jnp.zeros((LANES - N_BLOCKS - 8, MOBA_BLOCK), F32)
    km_hi = kmh_ref[0:N_BLOCKS, :]
    km_lo = kml_ref[0:N_BLOCKS, :]

    def v_aug(hh, block):
        vt = vt_ref[block, hh * HEAD_DIM:(hh + 1) * HEAD_DIM, :]
        return jnp.concatenate([vt, ones_rows], axis=0)

    def rows_to_tile(x8):
        return jnp.concatenate([x8] * (N_BLOCKS // 8), axis=0)

    q_far = []
    prev_masks = []
    carry = []
    for t in range(Q_TILES):
        own = step_i * Q_TILES + t
        prev = jnp.maximum(own - 1, 0)
        qt2 = qt_ref[t]
        k_own = k_ref[pl.ds(pl.multiple_of(own * MOBA_BLOCK, MOBA_BLOCK), MOBA_BLOCK), :]
        k_prev = k_ref[pl.ds(pl.multiple_of(prev * MOBA_BLOCK, MOBA_BLOCK), MOBA_BLOCK), :]
        for hh in range(2):
            head = pair * 2 + hh
            in_head = (frow >= hh * HEAD_DIM) & (frow < (hh + 1) * HEAD_DIM)
            qs = jnp.where(in_head, qt2, jnp.zeros_like(qt2)) * jnp.asarray(HEAD_DIM ** -0.5, BF16)

            gate = (jnp.dot(km_hi, qs, preferred_element_type=F32)
                    + jnp.dot(km_lo, qs, preferred_element_type=F32))
            eligible = nrow < own
            g = jnp.where(eligible, gate, -jnp.inf)
            sel = jnp.zeros((N_BLOCKS, MOBA_BLOCK), jnp.bool_)
            for _ in range(MOBA_TOPK):
                mx = rows_to_tile(_colreduce(g, jnp.maximum))
                idx = rows_to_tile(_colreduce(jnp.where(g == mx, nrow_f, float(LANES)),
                                              jnp.minimum))
                hit = nrow_f == idx
                sel = sel | hit
                g = jnp.where(hit, -jnp.inf, g)
            sel = sel & eligible

            b31 = jnp.full((8, MOBA_BLOCK), tab_ref[N_BUCKETS - 1, head], F32)
            b31_hi = b31.astype(BF16).astype(F32)
            bias_rows = jnp.where(brow == 0, b31_hi, jnp.where(brow == 1, b31 - b31_hi, 0.0))

            aug = jnp.concatenate([jnp.where(sel & (nrow != own - 1), 0.0, NEG), bias_rows,
                                   zero_rows], axis=0)
            q_far.append(jnp.concatenate([qs, aug.astype(BF16)], axis=0))

            prev_sel = _colreduce(jnp.where((nrow == own - 1) & jnp.logical_not(sel), NEG, 0.0),
                                  jnp.add)
            prev_masks.append(jnp.where(own >= 1, prev_sel[0:1], NEG))
            near_ref[2 * t + hh, 0:MOBA_BLOCK, :] = (
                jnp.dot(k_prev, qs, preferred_element_type=F32) + bias_ref[hh, 1])
            s_own = jnp.dot(k_own, qs, preferred_element_type=F32) + bias_ref[hh, 0]
            near_ref[2 * t + hh, MOBA_BLOCK:, :] = jnp.where(qcol >= krow, s_own, NEG)
            carry += [jnp.full((1, MOBA_BLOCK), NEG, F32),
                      jnp.zeros((HEAD_DIM + BF16_SUBLANES, MOBA_BLOCK), F32)]

    q_cat = jnp.concatenate(q_far, axis=1)
    n_tiles = 2 * Q_TILES

    def absorb_scores(s, va, m, acc):
        m_new = jnp.maximum(m, jnp.max(s, axis=0, keepdims=True))
        alpha = jnp.exp(m - m_new)
        p = jnp.exp(s - m_new).astype(BF16)
        return m_new, alpha * acc + jnp.dot(va, p, preferred_element_type=F32)

    def scores(step, slot):
        start = pl.multiple_of(jnp.minimum(step, N_FAR_STEPS - 1) * FAR_KEYS, FAR_KEYS)
        k_aug = jnp.concatenate([k_ref[pl.ds(start, FAR_KEYS), :],
                                 ext_ref[pl.ds(start, FAR_KEYS), :]], axis=1)
        s_ref[slot] = jnp.dot(k_aug, q_cat, preferred_element_type=F32)

    def absorb(step, slot, carry):
        block0 = step * FAR_BLOCKS
        out = []
        for j in range(n_tiles):
            hh = j % 2
            s = s_ref[slot, :, j * MOBA_BLOCK:(j + 1) * MOBA_BLOCK]
            va = jnp.concatenate([v_aug(hh, block0 + c) for c in range(FAR_BLOCKS)], axis=1)
            out += absorb_scores(s, va, carry[2 * j], carry[2 * j + 1])
        return tuple(out)

    n_steps = step_i
    scores(0, 0)

    def two_steps(base, carry):
        scores(base + 1, 1)
        carry = absorb(base, 0, carry)
        scores(base + 2, 0)
        return absorb(base + 1, 1, carry)

    def far_body(i, carry):
        return two_steps(4 * i + 2, two_steps(4 * i, carry))

    n_quads = n_steps // 4
    carry = lax.fori_loop(0, n_quads, far_body, tuple(carry))
    carry = lax.cond(n_steps % 4 >= 2,
                     lambda c: two_steps(4 * n_quads, c), lambda c: c, carry)
    carry = lax.cond(n_steps % 2 == 1,
                     lambda c: absorb(n_steps - 1, 0, c), lambda c: c, carry)

    for t in range(Q_TILES):
        own = step_i * Q_TILES + t
        outs = []
        for hh in range(2):
            j = 2 * t + hh
            va = jnp.concatenate([v_aug(hh, jnp.maximum(own - 1, 0)), v_aug(hh, own)], axis=1)
            s = jnp.concatenate([near_ref[j, 0:MOBA_BLOCK, :] + prev_masks[j],
                                 near_ref[j, MOBA_BLOCK:, :]], axis=0)
            _, acc = absorb_scores(s, va, carry[2 * j], carry[2 * j + 1])
            outs.append(acc[0:HEAD_DIM] / acc[HEAD_DIM:HEAD_DIM + 1])
        o_ref[t * MOBA_BLOCK:(t + 1) * MOBA_BLOCK, :] = (
            jnp.concatenate(outs, axis=0).T.astype(BF16))


def _attention(t5_bias, qt3, proj3, vt3, bias_tiles):
    n_pairs = HEADS // 2
    assert FAR_BLOCKS == Q_TILES
    return pl.pallas_call(
        _attn_kernel,
        grid=(BATCH, n_pairs, N_BLOCKS // Q_TILES),
        in_specs=[
            pl.BlockSpec(memory_space=pltpu.SMEM),
            pl.BlockSpec((None, Q_TILES, LANES, MOBA_BLOCK), lambda b, p, i: (b, i, p, 0)),
            pl.BlockSpec((None, SEQ, LANES), lambda b, p, i: (b, 0, p)),
            pl.BlockSpec((None, N_BLOCKS, LANES, MOBA_BLOCK), lambda b, p, i: (b, 0, p, 0)),
            pl.BlockSpec((2, 2, MOBA_BLOCK, MOBA_BLOCK), lambda b, p, i: (p, 0, 0, 0)),
        ],
        out_specs=pl.BlockSpec((None, Q_TILES * MOBA_BLOCK, LANES), lambda b, p, i: (b, i, p)),
        out_shape=jax.ShapeDtypeStruct((BATCH, SEQ, WIDTH), BF16),
        scratch_shapes=[
            pltpu.VMEM((LANES, LANES), BF16),
            pltpu.VMEM((LANES, LANES), BF16),
            pltpu.VMEM((SEQ, LANES), BF16),
            pltpu.VMEM((2, FAR_KEYS, 2 * Q_TILES * MOBA_BLOCK), F32),
            pltpu.VMEM((2 * Q_TILES, 2 * MOBA_BLOCK, MOBA_BLOCK), F32),
        ],
        compiler_params=pltpu.CompilerParams(
            dimension_semantics=("arbitrary", "arbitrary", "arbitrary"),
            vmem_limit_bytes=VMEM_LIMIT),
        name="moba_attention",
    )(t5_bias, qt3, proj3, vt3, bias_tiles)


def _hgrn_kernel(q_ref, f_ref, i_ref, g_ref, lbl_ref, nw_ref, o_ref,
                 state_ref, bpad_ref, kpad_ref, vpad_ref, shb_ref, shk_ref, shv_ref,
                 ones_ref, mask_ref):
    C = HGRN_CHUNK
    half = WIDTH // 2
    assert HEAD_DIM == 1 << 6

    @pl.when(pl.program_id(1) == 0)
    def _():
        state_ref[...] = jnp.zeros_like(state_ref)
        sr = lax.broadcasted_iota(jnp.int32, (half, half), 0) >> 6
        sc = lax.broadcasted_iota(jnp.int32, (half, half), 1) >> 6
        same = jnp.where(sr == sc, 1.0, 0.0)
        ones_ref[...] = same.astype(BF16)
        mask_ref[...] = same

    zpad = jnp.zeros((8, WIDTH), F32)
    bpad_ref[0:8, :] = zpad
    kpad_ref[0:8, :] = zpad
    vpad_ref[0:8, :] = zpad

    l0 = lbl_ref[0:1, :]
    l1 = lbl_ref[1:2, :]
    lmx = jnp.maximum(l0, l1)
    e0 = jnp.exp(l0 - lmx)
    lb = e0 / (e0 + jnp.exp(l1 - lmx))
    nw = nw_ref[...]

    ri = lax.broadcasted_iota(jnp.int32, (C, C), 0)
    ci = lax.broadcasted_iota(jnp.int32, (C, C), 1)
    tril = jnp.where(ri >= ci, 1.0, 0.0).astype(BF16)

    def head_sum(t):
        return jnp.concatenate(
            [jnp.dot(t[:, 0:half], ones_ref[...], preferred_element_type=F32),
             jnp.dot(t[:, half:], ones_ref[...], preferred_element_type=F32)], axis=1)

    def prepare(r0):
        qq = q_ref[r0:r0 + C, :].astype(F32)
        qq = qq * _sigmoid(qq)
        f = lb + (1.0 - lb) * _sigmoid(f_ref[r0:r0 + C, :])
        gl = jnp.log(f)
        kk = 1.0 - f
        vv = i_ref[r0:r0 + C, :].astype(F32)
        g_hi = gl.astype(BF16)
        g_r = gl - g_hi.astype(F32)
        g_mid = g_r.astype(BF16)
        g_lo = (g_r - g_mid.astype(F32)).astype(BF16)
        b = (jnp.dot(tril, g_hi, preferred_element_type=F32)
             + jnp.dot(tril, g_mid, preferred_element_type=F32)
             + jnp.dot(tril, g_lo, preferred_element_type=F32))
        return qq, kk, vv, b * LOG2E

    def intra_pairs(qq, kk, vv, b):
        bpad_ref[8:8 + C, :] = b
        kpad_ref[8:8 + C, :] = kk
        vpad_ref[8:8 + C, :] = vv
        for dd in range(8):
            shb_ref[dd] = bpad_ref[8 - dd:8 - dd + C, :]
            shk_ref[dd] = kpad_ref[8 - dd:8 - dd + C, :]
            shv_ref[dd] = vpad_ref[8 - dd:8 - dd + C, :]

        o_parts = []
        for j in range(C // 8):
            rows = C - 8 * j
            bt = b[8 * j:, :]
            qt = qq[8 * j:, :]
            terms = []
            vs_all = []
            for dd in range(8):
                terms.append(qt * shk_ref[dd, 0:rows, :] * jnp.exp2(bt - shb_ref[dd, 0:rows, :]))
                vs_all.append(shv_ref[dd, 0:rows, :])
            a = head_sum(jnp.concatenate(terms, axis=0).astype(BF16))
            av = a * jnp.concatenate(vs_all, axis=0)
            part = av[0:rows]
            for dd in range(1, 8):
                part = part + av[dd * rows:(dd + 1) * rows]
            o_parts.append(part)
        o = o_parts[0]
        for j in range(1, C // 8):
            o = o + jnp.concatenate([jnp.zeros((8 * j, WIDTH), F32), o_parts[j]], axis=0)
        return o

    pt_i = lax.broadcasted_iota(jnp.int32, (C, half), 0)
    ps_i = lax.broadcasted_iota(jnp.int32, (C, half), 1) & (C - 1)
    causal = jnp.where(ps_i <= pt_i, 1.0, 0.0)

    def intra_matmul(qq, kk, vv, b):
        qd = (qq * jnp.exp2(b)).astype(BF16)
        kd = (kk * jnp.exp2(-b)).astype(BF16)
        vb = vv.astype(BF16)
        outs = []
        for a_i in range(2):
            sl = slice(a_i * half, (a_i + 1) * half)
            k4 = jnp.concatenate([kd[:, sl]] * (half // C), axis=0) * ones_ref[...]
            v4 = jnp.concatenate([vb[:, sl]] * (half // C), axis=0) * ones_ref[...]
            pair = _nt_dot(qd[:, sl], k4) * causal
            outs.append(jnp.dot(pair.astype(BF16), v4, preferred_element_type=F32))
        return jnp.concatenate(outs, axis=1)

    def finish(r0, o, qq, kk, vv, b):
        qe = (qq * jnp.exp2(b)).astype(BF16)
        b_last = b[C - 1:C, :]
        kd = kk * jnp.exp2(b_last - b)
        dec = jnp.exp2(b_last)
        inter = []
        for a_i in range(2):
            sl = slice(a_i * half, (a_i + 1) * half)
            st = state_ref[a_i]
            inter.append(_nt_dot(qe[:, sl], st.astype(BF16)))
            upd = _tn_dot(vv[:, sl].astype(BF16), kd[:, sl].astype(BF16))
            state_ref[a_i] = st * dec[:, sl] + upd * mask_ref[...]
        o = o + jnp.concatenate(inter, axis=1)

        ms = head_sum((o * o).astype(BF16)) * (1.0 / HEAD_DIM)
        r = o * lax.rsqrt(ms + RMS_EPS) * nw
        r = r * _sigmoid(g_ref[r0:r0 + C, :].astype(F32))
        o_ref[r0:r0 + C, :] = r.astype(BF16)

    chunks = [prepare(u * C) for u in range(HGRN_ROWS // C)]
    lowest = chunks[0][3][C - 1:C, :]
    for ch in chunks[1:]:
        lowest = jnp.minimum(lowest, ch[3][C - 1:C, :])
    safe = jnp.min(lowest) > -HGRN_SAFE_LOG2

    def run(intra):
        def body():
            for u, (qq, kk, vv, b) in enumerate(chunks):
                finish(u * C, intra(qq, kk, vv, b), qq, kk, vv, b)
        return body

    lax.cond(safe, run(intra_matmul), run(intra_pairs))


def _hgrn(proj3, f3, lb_logits, norm_w):
    blk = lambda col: pl.BlockSpec((None, HGRN_ROWS, WIDTH), lambda b, i, col=col: (b, i, col))
    return pl.pallas_call(
        _hgrn_kernel,
        grid=(BATCH, SEQ // HGRN_ROWS),
        in_specs=[
            blk(COL_HQ),
            pl.BlockSpec((None, HGRN_ROWS, WIDTH), lambda b, i: (b, i, 0)),
            blk(COL_HI),
            blk(COL_HG),
            pl.BlockSpec((2, WIDTH), lambda b, i: (0, 0)),
            pl.BlockSpec((1, WIDTH), lambda b, i: (0, 0)),
        ],
        out_specs=pl.BlockSpec((None, HGRN_ROWS, WIDTH), lambda b, i: (b, i, 0)),
        out_shape=jax.ShapeDtypeStruct((BATCH, SEQ, WIDTH), BF16),
        scratch_shapes=[
            pltpu.VMEM((2, WIDTH // 2, WIDTH // 2), F32),
            pltpu.VMEM((8 + HGRN_CHUNK, WIDTH), F32),
            pltpu.VMEM((8 + HGRN_CHUNK, WIDTH), F32),
            pltpu.VMEM((8 + HGRN_CHUNK, WIDTH), F32),
            pltpu.VMEM((8, HGRN_CHUNK, WIDTH), F32),
            pltpu.VMEM((8, HGRN_CHUNK, WIDTH), F32),
            pltpu.VMEM((8, HGRN_CHUNK, WIDTH), F32),
            pltpu.VMEM((WIDTH // 2, WIDTH // 2), BF16),
            pltpu.VMEM((WIDTH // 2, WIDTH // 2), F32),
        ],
        compiler_params=pltpu.CompilerParams(
            dimension_semantics=("arbitrary", "arbitrary"), vmem_limit_bytes=VMEM_LIMIT),
        name="hgrn2",
    )(proj3, f3, proj3, proj3, lb_logits, norm_w)


def _out_kernel(a_ref, r_ref, x_ref, wo_ref, g_ref, b_ref, wr_ref, br_ref,
                x1_ref, comb_ref):
    mix = jnp.dot(a_ref[...], wo_ref[0:WIDTH, :], preferred_element_type=F32)
    mix = mix + jnp.dot(r_ref[...], wo_ref[WIDTH:, :], preferred_element_type=F32)
    x1 = _layer_norm(ALPHA * x_ref[...] + mix, g_ref[...], b_ref[...])
    x1_ref[...] = x1

    x_hi = x1.astype(BF16)
    x_lo = (x1 - x_hi.astype(F32)).astype(BF16)
    wr = wr_ref[...]
    w_hi = wr.astype(BF16)
    w_lo = (wr - w_hi.astype(F32)).astype(BF16)
    logits = (jnp.dot(x_hi, w_hi, preferred_element_type=F32)
              + jnp.dot(x_lo, w_hi, preferred_element_type=F32)
              + jnp.dot(x_hi, w_lo, preferred_element_type=F32)) + br_ref[...]
    assert EXPERTS_PER_GROUP == 1 << 2
    lane_i = lax.broadcasted_iota(jnp.int32, logits.shape, 1)
    lane = lane_i.astype(F32)
    grp_of_lane = (lane_i >> 2).astype(F32)
    none = float(LANES)
    is_g = (lane_i >= N_EXPERTS) & (lane_i < N_EXPERTS + N_GROUPS)
    gl = jnp.where(is_g, logits, -jnp.inf)
    ge = jnp.exp(gl - jnp.max(gl, axis=-1, keepdims=True))
    gp = ge / jnp.sum(ge, axis=-1, keepdims=True)
    g_w = jnp.max(gp, axis=-1, keepdims=True)
    g_lane = jnp.min(jnp.where(is_g & (gp == g_w), lane, none), axis=-1, keepdims=True)
    g_idx = g_lane - float(N_EXPERTS)

    in_grp = (lane_i < N_EXPERTS) & (grp_of_lane == g_idx)
    el = jnp.where(in_grp, logits, -jnp.inf)
    ee = jnp.exp(el - jnp.max(el, axis=-1, keepdims=True))
    ep = ee / jnp.sum(ee, axis=-1, keepdims=True)
    p1 = jnp.max(ep, axis=-1, keepdims=True)
    i1 = jnp.min(jnp.where(in_grp & (ep == p1), lane, none), axis=-1, keepdims=True)
    rest = in_grp & (lane != i1)
    ep2 = jnp.where(rest, ep, -1.0)
    p2 = jnp.max(ep2, axis=-1, keepdims=True)
    i2 = jnp.min(jnp.where(rest & (ep2 == p2), lane, none), axis=-1, keepdims=True)
    den = p1 + p2
    comb = jnp.where(lane == i1, g_w * (p1 / den), 0.0)
    comb = jnp.where(lane == i2, g_w * (p2 / den), comb)
    comb_ref[...] = jnp.where(lane_i == GROUP_LANE, g_idx, comb)


def _out_proj(a2, r2, x2, wo_b, g1, b1, w_route, b_route):
    row = lambda w: pl.BlockSpec((OUT_TM, w), lambda i: (i, 0))
    full = lambda s: pl.BlockSpec(s, lambda i: (0, 0))
    return pl.pallas_call(
        _out_kernel,
        grid=(N_TOK // OUT_TM,),
        in_specs=[row(WIDTH), row(WIDTH), row(D_MODEL), full((D_MODEL, D_MODEL)),
                  full((1, D_MODEL)), full((1, D_MODEL)), full((D_MODEL, LANES)), full((1, LANES))],
        out_specs=[row(D_MODEL), row(LANES)],
        out_shape=[jax.ShapeDtypeStruct((N_TOK, D_MODEL), F32),
                   jax.ShapeDtypeStruct((N_TOK, LANES), F32)],
        compiler_params=pltpu.CompilerParams(
            dimension_semantics=("arbitrary",), vmem_limit_bytes=VMEM_LIMIT),
        name="out_proj_ln_route",
    )(a2, r2, x2, wo_b, g1, b1, w_route, b_route)


def _moe_kernel(x1_ref, comb_ref, wg_ref, wu_ref, wd_ref, g_ref, b_ref, o_ref,
                pt_ref, xs_ref, cs_ref, acc_ref, win_ref):
    step = pl.program_id(1)
    assert MOE_ALIGN == 1 << 7 and EXPERTS_PER_GROUP == 1 << 2
    assert EXPERTS_PER_GROUP % MOE_EXPERTS_PER_STEP == 0

    @pl.when(step == 0)
    def _():
        comb = comb_ref[...]
        lane = lax.broadcasted_iota(jnp.int32, comb.shape, 1)
        lane_f = lane.astype(F32)
        gid = jnp.sum(jnp.where(lane == GROUP_LANE, comb, 0.0), axis=-1, keepdims=True)
        mine = lane_f == gid
        ti = lax.broadcasted_iota(jnp.int32, (MOE_TM, MOE_TM), 0)
        tj = lax.broadcasted_iota(jnp.int32, (MOE_TM, MOE_TM), 1)
        tril = jnp.where(ti >= tj, 1.0, 0.0).astype(BF16)
        incl = jnp.dot(tril, jnp.where(mine, 1.0, 0.0).astype(BF16),
                       preferred_element_type=F32)
        rank = jnp.sum(jnp.where(mine, incl, 0.0), axis=-1, keepdims=True) - 1.0
        counts = incl[MOE_TM - 1:MOE_TM, :]
        lane_row = lane[0:1, :]
        start = jnp.int32(0)
        base = jnp.zeros_like(gid)
        for g in range(N_GROUPS):
            cnt = jnp.sum(jnp.where(lane_row == g, counts, 0.0)).astype(jnp.int32)
            chunks = (cnt + (MOE_ALIGN - 1)) >> 7
            win_ref[0, g] = start
            win_ref[1, g] = chunks >> 1
            win_ref[2, g] = chunks & 1
            base = jnp.where(gid == float(g), start.astype(F32), base)
            start = start + chunks * MOE_ALIGN
        pos = base + rank
        r_iota = lax.broadcasted_iota(jnp.int32, (MOE_TM, MOE_ROWS), 1).astype(F32)
        pt = jnp.where(r_iota == pos, 1.0, 0.0).astype(BF16)
        pt_ref[...] = pt
        xs_ref[...] = _tn_dot(pt, x1_ref[...].astype(BF16)).astype(BF16)
        c_hi = comb.astype(BF16)
        c_lo = (comb - c_hi.astype(F32)).astype(BF16)
        cs_ref[...] = _tn_dot(pt, c_hi) + _tn_dot(pt, c_lo)
        acc_ref[...] = jnp.zeros_like(acc_ref)

    e0 = step * MOE_EXPERTS_PER_STEP
    g = e0 >> 2
    row0 = win_ref[0, g]
    n_full = win_ref[1, g]

    def window(start, rows):
        r0 = pl.multiple_of(start, MOE_ALIGN)
        xw = xs_ref[pl.ds(r0, rows), :]
        csw = cs_ref[pl.ds(r0, rows), :]
        lane_w = lax.broadcasted_iota(jnp.int32, (rows, LANES), 1)
        y = None
        for j in range(MOE_EXPERTS_PER_STEP):
            hg = jnp.dot(xw, wg_ref[j], preferred_element_type=F32)
            hu = jnp.dot(xw, wu_ref[j], preferred_element_type=F32)
            h = (hg * _sigmoid(hg) * hu).astype(BF16)
            c = jnp.sum(jnp.where(lane_w == e0 + j, csw, 0.0), axis=-1, keepdims=True)
            yj = c * jnp.dot(h, wd_ref[j], preferred_element_type=F32)
            y = yj if y is None else y + yj
        acc_ref[pl.ds(r0, rows), :] += y

    def full_window(k, _):
        window(row0 + k * MOE_WIN, MOE_WIN)
        return 0

    lax.fori_loop(0, n_full, full_window, 0)

    @pl.when(win_ref[2, g] == 1)
    def _():
        window(row0 + n_full * MOE_WIN, MOE_ALIGN)

    @pl.when(step == N_EXPERTS // MOE_EXPERTS_PER_STEP - 1)
    def _():
        moe = jnp.dot(pt_ref[...], acc_ref[...].astype(BF16), preferred_element_type=F32)
        o_ref[...] = _layer_norm(ALPHA * x1_ref[...] + moe, g_ref[...], b_ref[...])


def _moe(x1, comb, wg_b, wu_b, wd_b, g2, b2):
    return pl.pallas_call(
        _moe_kernel,
        grid=(N_TOK // MOE_TM, N_EXPERTS // MOE_EXPERTS_PER_STEP),
        in_specs=[
            pl.BlockSpec((MOE_TM, D_MODEL), lambda i, e: (i, 0)),
            pl.BlockSpec((MOE_TM, LANES), lambda i, e: (i, 0)),
            pl.BlockSpec((MOE_EXPERTS_PER_STEP, D_MODEL, D_EXPERT), lambda i, e: (e, 0, 0)),
            pl.BlockSpec((MOE_EXPERTS_PER_STEP, D_MODEL, D_EXPERT), lambda i, e: (e, 0, 0)),
            pl.BlockSpec((MOE_EXPERTS_PER_STEP, D_EXPERT, D_MODEL), lambda i, e: (e, 0, 0)),
            pl.BlockSpec((1, D_MODEL), lambda i, e: (0, 0)),
            pl.BlockSpec((1, D_MODEL), lambda i, e: (0, 0)),
        ],
        out_specs=pl.BlockSpec((MOE_TM, D_MODEL), lambda i, e: (i, 0)),
        out_shape=jax.ShapeDtypeStruct((N_TOK, D_MODEL), F32),
        scratch_shapes=[
            pltpu.VMEM((MOE_TM, MOE_ROWS), BF16),
            pltpu.VMEM((MOE_ROWS, D_MODEL), BF16),
            pltpu.VMEM((MOE_ROWS, LANES), F32),
            pltpu.VMEM((MOE_ROWS, D_MODEL), F32),
            pltpu.SMEM((3, N_GROUPS), jnp.int32),
        ],
        compiler_params=pltpu.CompilerParams(
            dimension_semantics=("arbitrary", "arbitrary"), vmem_limit_bytes=MOE_VMEM_LIMIT),
        name="moe_experts_ln",
    )(x1, comb, wg_b, wu_b, wd_b, g2, b2)


def kernel(x, w_in, t5_bias, hgrn_lb_logits, hgrn_norm_w, w_o, ln1_g, ln1_b, w_group, b_group,
           w_expert, b_expert, w_gate, w_up, w_down, ln2_g, ln2_b):
    x2 = x.reshape(N_TOK, D_MODEL)
    w_in_b = w_in[0].astype(BF16)
    w_qt_b = w_in_b[:, 0:WIDTH].T
    w_vt_b = w_in_b[:, 2 * WIDTH:3 * WIDTH].T
    proj, f_logits, qt3, vt3 = _proj(x2, w_in_b, w_qt_b, w_vt_b)
    proj3 = proj.reshape(BATCH, SEQ, ROW_WIDTH)

    a = _attention(t5_bias, qt3, proj3, vt3, _bias_tiles(t5_bias))
    r = _hgrn(proj3, f_logits.reshape(BATCH, SEQ, WIDTH), hgrn_lb_logits, hgrn_norm_w)

    pad = LANES - N_EXPERTS - N_GROUPS
    w_route = jnp.concatenate(
        [w_expert[0].transpose(1, 0, 2).reshape(D_MODEL, N_EXPERTS), w_group[0],
         jnp.zeros((D_MODEL, pad), F32)], axis=1)
    b_route = jnp.concatenate(
        [b_expert[0].reshape(N_EXPERTS), b_group[0], jnp.zeros((pad,), F32)]).reshape(1, LANES)
    x1, comb = _out_proj(
        a.reshape(N_TOK, WIDTH), r.reshape(N_TOK, WIDTH), x2, w_o[0].astype(BF16),
        ln1_g, ln1_b, w_route, b_route)

    out = _moe(x1, comb, w_gate[0].astype(BF16), w_up[0].astype(BF16),
               w_down[0].astype(BF16), ln2_g, ln2_b)
    return out.reshape(BATCH, SEQ, D_MODEL)
```

```python
import functools
import math

import jax
import jax.numpy as jnp
from jax import lax
from jax.experimental import pallas as pl
from jax.experimental.pallas import tpu as pltpu

F32 = jnp.float32
BF16 = jnp.bfloat16

D_MODEL = 1024
BATCH = 2
SEQ = 8192
N_TOK = BATCH * SEQ
HEADS = 8
HEAD_DIM = 64
WIDTH = HEADS * HEAD_DIM
IN_WIDTH = 7 * WIDTH
ROW_WIDTH = 5 * WIDTH
COL_HQ, COL_HF, COL_HI, COL_HG = 1, 2, 3, 4
MOBA_BLOCK = 256
N_BLOCKS = SEQ // MOBA_BLOCK
MOBA_TOPK = 3
HGRN_CHUNK = 64
HGRN_SAFE_LOG2 = 100.0
N_BUCKETS = 32
MAX_DISTANCE = 128
N_GROUPS = 4
EXPERTS_PER_GROUP = 4
N_EXPERTS = 16
D_EXPERT = 512
LN_EPS = 1e-5
RMS_EPS = 1e-6
ALPHA = 2.0 ** 0.25
LOG2E = math.log2(math.e)

LANES = 128
BF16_SUBLANES = 16
NEG = -1e30
VMEM_LIMIT = 48 * 1024 * 1024
MOE_VMEM_LIMIT = 56 * 1024 * 1024

PROJ_TM = 2048
OUT_TM = 1024
MOE_TM = 1024
MOE_ALIGN = 128
MOE_WIN = 256
MOE_EXPERTS_PER_STEP = 2
MOE_ROWS = MOE_TM + N_GROUPS * MOE_ALIGN
GROUP_LANE = N_EXPERTS
HGRN_ROWS = 512
Q_TILES = 2
FAR_BLOCKS = 2
FAR_KEYS = FAR_BLOCKS * MOBA_BLOCK
N_FAR_STEPS = N_BLOCKS // FAR_BLOCKS

BIAS_HI_LANE = N_BLOCKS
BIAS_LO_LANE = N_BLOCKS + 1


def _nt_dot(a, b):
    return lax.dot_general(a, b, (((1,), (1,)), ((), ())), preferred_element_type=F32)


def _tn_dot(a, b):
    return lax.dot_general(a, b, (((0,), (0,)), ((), ())), preferred_element_type=F32)


def _sigmoid(x):
    return 1.0 / (1.0 + jnp.exp2(x * (-LOG2E)))


def _layer_norm(y, g, b):
    mu = jnp.mean(y, axis=-1, keepdims=True)
    d = y - mu
    var = jnp.mean(d * d, axis=-1, keepdims=True)
    return d * lax.rsqrt(var + LN_EPS) * g + b


def _bias_tile_kernel(tab_ref, out_ref):
    h = pl.program_id(0)
    ki = lax.broadcasted_iota(jnp.int32, (MOBA_BLOCK, MOBA_BLOCK), 0)
    qi = lax.broadcasted_iota(jnp.int32, (MOBA_BLOCK, MOBA_BLOCK), 1)
    max_exact = N_BUCKETS // 2
    for w in range(2):
        n = jnp.maximum(qi - ki + MOBA_BLOCK * w, 0)
        nf = jnp.maximum(n, 1).astype(F32)
        scaled = (jnp.log(nf / max_exact) / math.log(MAX_DISTANCE / max_exact)
                  * (N_BUCKETS - max_exact))
        large = max_exact + jnp.floor(jnp.maximum(scaled, 0.0)).astype(jnp.int32)
        large = jnp.minimum(large, N_BUCKETS - 1)
        bucket = jnp.where(n < max_exact, n, large)
        acc = jnp.zeros((MOBA_BLOCK, MOBA_BLOCK), F32)
        for bk in range(N_BUCKETS):
            acc = jnp.where(bucket == bk, tab_ref[bk, h], acc)
        out_ref[0, w] = acc


def _bias_tiles(t5_bias):
    return pl.pallas_call(
        _bias_tile_kernel,
        grid=(HEADS,),
        in_specs=[pl.BlockSpec(memory_space=pltpu.SMEM)],
        out_specs=pl.BlockSpec((1, 2, MOBA_BLOCK, MOBA_BLOCK), lambda h: (h, 0, 0, 0)),
        out_shape=jax.ShapeDtypeStruct((HEADS, 2, MOBA_BLOCK, MOBA_BLOCK), F32),
        name="t5_bias_tiles",
    )(t5_bias)


def _proj_kernel(x_ref, w_ref, wqt_ref, wvt_ref, o_ref, f_ref, qt_ref, vt_ref, xb_ref):
    n = pl.program_id(1)

    def feature_major(wt_ref, t_ref):
        t = _nt_dot(wt_ref[...], xb_ref[...]).astype(BF16)
        for c in range(PROJ_TM // MOBA_BLOCK):
            t_ref[c] = t[:, c * MOBA_BLOCK:(c + 1) * MOBA_BLOCK]

    @pl.when(n == 0)
    def _():
        xb_ref[...] = x_ref[...].astype(BF16)
        feature_major(wqt_ref, qt_ref)

    acc = jnp.dot(xb_ref[...], w_ref[...], preferred_element_type=F32)
    o_ref[...] = acc.astype(BF16)

    @pl.when(n == 1)
    def _():
        feature_major(wvt_ref, vt_ref)

    @pl.when(n == COL_HF)
    def _():
        f_ref[...] = acc


def _proj(x2, w_in_b, w_qt_b, w_vt_b):
    tiles_per_seq = SEQ // PROJ_TM
    blocks_per_tile = PROJ_TM // MOBA_BLOCK
    t_spec = pl.BlockSpec((None, blocks_per_tile, WIDTH, MOBA_BLOCK),
                          lambda i, n: (i // tiles_per_seq, i % tiles_per_seq, 0, 0))
    t_shape = jax.ShapeDtypeStruct((BATCH, N_BLOCKS, WIDTH, MOBA_BLOCK), BF16)
    return pl.pallas_call(
        _proj_kernel,
        grid=(N_TOK // PROJ_TM, ROW_WIDTH // WIDTH),
        in_specs=[
            pl.BlockSpec((PROJ_TM, D_MODEL), lambda i, n: (i, 0)),
            pl.BlockSpec((D_MODEL, WIDTH), lambda i, n: (0, jnp.where(n == 0, 1, n + 2))),
            pl.BlockSpec((WIDTH, D_MODEL), lambda i, n: (0, 0)),
            pl.BlockSpec((WIDTH, D_MODEL), lambda i, n: (0, 0)),
        ],
        out_specs=[
            pl.BlockSpec((PROJ_TM, WIDTH), lambda i, n: (i, n)),
            pl.BlockSpec((PROJ_TM, WIDTH), lambda i, n: (i, 0)),
            t_spec,
            t_spec,
        ],
        out_shape=[
            jax.ShapeDtypeStruct((N_TOK, ROW_WIDTH), BF16),
            jax.ShapeDtypeStruct((N_TOK, WIDTH), F32),
            t_shape,
            t_shape,
        ],
        scratch_shapes=[pltpu.VMEM((PROJ_TM, D_MODEL), BF16)],
        compiler_params=pltpu.CompilerParams(
            dimension_semantics=("arbitrary", "arbitrary"), vmem_limit_bytes=VMEM_LIMIT),
        name="in_proj",
    )(x2, w_in_b, w_qt_b, w_vt_b)


def _colreduce(x, op):
    r = x[0:8]
    for i in range(1, x.shape[0] // 8):
        r = op(r, x[8 * i:8 * i + 8])
    for shift in (4, 2, 1):
        r = op(r, pltpu.roll(r, shift, 0))
    return r


def _attn_kernel(tab_ref, qt_ref, qn_ref, k_ref, vt_ref, bias_ref, o_ref,
                 kmh_ref, kml_ref, ext_ref, s_ref, near_a_ref, near_b_ref, qcat_ref, pmask_ref):
    assert MOBA_BLOCK == 1 << 8
    pair = pl.program_id(1)
    step_i = pl.program_id(2)
    lane = lax.broadcasted_iota(jnp.int32, (MOBA_BLOCK, LANES), 1)

    @pl.when(step_i == 0)
    def _():
        r = lax.broadcasted_iota(jnp.int32, (LANES, SEQ), 1)
        nb = lax.broadcasted_iota(jnp.int32, (LANES, SEQ), 0)
        avg = jnp.where((r >> 8) == nb, 1.0 / MOBA_BLOCK, 0.0).astype(BF16)
        km = jnp.dot(avg, k_ref[...], preferred_element_type=F32)
        hi = km.astype(BF16)
        kmh_ref[...] = hi
        kml_ref[...] = (km - hi.astype(F32)).astype(BF16)
        kr = lax.broadcasted_iota(jnp.int32, (SEQ, LANES), 0) >> 8
        kl = lax.broadcasted_iota(jnp.int32, (SEQ, LANES), 1)
        ext_ref[...] = jnp.where((kl == kr) | (kl == BIAS_HI_LANE) | (kl == BIAS_LO_LANE),
                                 1.0, 0.0).astype(BF16)

    krow = lax.broadcasted_iota(jnp.int32, (MOBA_BLOCK, MOBA_BLOCK), 0)
    qcol = lax.broadcasted_iota(jnp.int32, (MOBA_BLOCK, MOBA_BLOCK), 1)
    frow = lax.broadcasted_iota(jnp.int32, (LANES, MOBA_BLOCK), 0)
    nrow = lax.broadcasted_iota(jnp.int32, (N_BLOCKS, MOBA_BLOCK), 0)
    nrow_f = nrow.astype(F32)
    brow = lax.broadcasted_iota(jnp.int32, (8, MOBA_BLOCK), 0)
    ones_rows = jnp.ones((BF16_SUBLANES, MOBA_BLOCK), BF16)
    zero_rows = jnp.zeros((LANES - N_BLOCKS - 8, MOBA_BLOCK), F32)
    def v_aug(hh, block):
        vt = vt_ref[block, hh * HEAD_DIM:(hh + 1) * HEAD_DIM, :]
        return jnp.concatenate([vt, ones_rows], axis=0)

    def rows_to_tile(x8):
        return jnp.concatenate([x8] * (N_BLOCKS // 8), axis=0)

    n_tiles = 2 * Q_TILES
    n_q_steps = N_BLOCKS // Q_TILES

    def scores(step, slot):
        start = pl.multiple_of(jnp.minimum(step, N_FAR_STEPS - 1) * FAR_KEYS, FAR_KEYS)
        k_aug = jnp.concatenate([k_ref[pl.ds(start, FAR_KEYS), :],
                                 ext_ref[pl.ds(start, FAR_KEYS), :]], axis=1)
        s_ref[slot] = jnp.dot(k_aug, qcat_ref[...],
                              preferred_element_type=F32)

    def prologue(q_step, q_src, near_dst):
        km_hi = kmh_ref[0:N_BLOCKS, :]
        km_lo = kml_ref[0:N_BLOCKS, :]
        q_far = []
        for t in range(Q_TILES):
            own = q_step * Q_TILES + t
            prev = jnp.maximum(own - 1, 0)
            qt2 = q_src[t]
            k_own = k_ref[pl.ds(pl.multiple_of(own * MOBA_BLOCK, MOBA_BLOCK), MOBA_BLOCK), :]
            k_prev = k_ref[pl.ds(pl.multiple_of(prev * MOBA_BLOCK, MOBA_BLOCK), MOBA_BLOCK), :]
            for hh in range(2):
                head = pair * 2 + hh
                in_head = (frow >= hh * HEAD_DIM) & (frow < (hh + 1) * HEAD_DIM)
                qs = (jnp.where(in_head, qt2, jnp.zeros_like(qt2))
                      * jnp.asarray(HEAD_DIM ** -0.5, BF16))

                gate = (jnp.dot(km_hi, qs, preferred_element_type=F32)
                        + jnp.dot(km_lo, qs, preferred_element_type=F32))
                eligible = nrow < own
                g = jnp.where(eligible, gate, -jnp.inf)
                sel = jnp.zeros((N_BLOCKS, MOBA_BLOCK), jnp.bool_)
                for _ in range(MOBA_TOPK):
                    mx = rows_to_tile(_colreduce(g, jnp.maximum))
                    idx = rows_to_tile(_colreduce(jnp.where(g == mx, nrow_f, float(LANES)),
                                                  jnp.minimum))
                    hit = nrow_f == idx
                    sel = sel | hit
                    g = jnp.where(hit, -jnp.inf, g)
                sel = sel & eligible

                b31 = jnp.full((8, MOBA_BLOCK), tab_ref[N_BUCKETS - 1, head], F32)
                b31_hi = b31.astype(BF16).astype(F32)
                bias_rows = jnp.where(brow == 0, b31_hi, jnp.where(brow == 1, b31 - b31_hi, 0.0))

                aug = jnp.concatenate([jnp.where(sel & (nrow != own - 1), 0.0, NEG), bias_rows,
                                       zero_rows], axis=0)
                q_far.append(jnp.concatenate([qs, aug.astype(BF16)], axis=0))

                prev_sel = _colreduce(
                    jnp.where((nrow == own - 1) & jnp.logical_not(sel), NEG, 0.0), jnp.add)
                pmask_ref[2 * t + hh] = jnp.where(own >= 1, prev_sel, NEG)
                near_dst[2 * t + hh, 0:MOBA_BLOCK, :] = (
                    jnp.dot(k_prev, qs, preferred_element_type=F32) + bias_ref[hh, 1])
                s_own = jnp.dot(k_own, qs, preferred_element_type=F32) + bias_ref[hh, 0]
                near_dst[2 * t + hh, MOBA_BLOCK:, :] = jnp.where(qcol >= krow, s_own, NEG)
        qcat_ref[...] = jnp.concatenate(q_far, axis=1)
        scores(0, 0)

    @pl.when(step_i == 0)
    def _():
        prologue(0, qt_ref, near_a_ref)

    def absorb_scores(s, va, m, acc):
        m_new = jnp.maximum(m, jnp.max(s, axis=0, keepdims=True))
        alpha = jnp.exp(m - m_new)
        p = jnp.exp(s - m_new).astype(BF16)
        return m_new, alpha * acc + jnp.dot(va, p, preferred_element_type=F32)

    def absorb(step, slot, carry):
        block0 = step * FAR_BLOCKS
        out = []
        for j in range(n_tiles):
            hh = j % 2
            s = s_ref[slot, :, j * MOBA_BLOCK:(j + 1) * MOBA_BLOCK]
            va = jnp.concatenate([v_aug(hh, block0 + c) for c in range(FAR_BLOCKS)], axis=1)
            out += absorb_scores(s, va, carry[2 * j], carry[2 * j + 1])
        return tuple(out)

    n_steps = step_i
    carry = []
    for _ in range(n_tiles):
        carry += [jnp.full((1, MOBA_BLOCK), NEG, F32),
                  jnp.zeros((HEAD_DIM + BF16_SUBLANES, MOBA_BLOCK), F32)]

    def two_steps(base, carry):
        scores(base + 1, 1)
        carry = absorb(base, 0, carry)
        scores(base + 2, 0)
        return absorb(base + 1, 1, carry)

    def far_body(i, carry):
        return two_steps(4 * i + 2, two_steps(4 * i, carry))

    n_quads = n_steps // 4
    carry = lax.fori_loop(0, n_quads, far_body, tuple(carry))
    carry = lax.cond(n_steps % 4 >= 2,
                     lambda c: two_steps(4 * n_quads, c), lambda c: c, carry)
    carry = lax.cond(n_steps % 2 == 1,
                     lambda c: absorb(n_steps - 1, 0, c), lambda c: c, carry)

    prev_masks = [pmask_ref[j, 0:1, :] for j in range(n_tiles)]

    def epilogue(near_src):
        for t in range(Q_TILES):
            own = step_i * Q_TILES + t
            outs = []
            for hh in range(2):
                j = 2 * t + hh
                va = jnp.concatenate([v_aug(hh, jnp.maximum(own - 1, 0)), v_aug(hh, own)], axis=1)
                s = jnp.concatenate([near_src[j, 0:MOBA_BLOCK, :] + prev_masks[j],
                                     near_src[j, MOBA_BLOCK:, :]], axis=0)
                _, acc = absorb_scores(s, va, carry[2 * j], carry[2 * j + 1])
                outs.append(acc[0:HEAD_DIM] / acc[HEAD_DIM:HEAD_DIM + 1])
            o_ref[t * MOBA_BLOCK:(t + 1) * MOBA_BLOCK, :] = (
                jnp.concatenate(outs, axis=0).T.astype(BF16))

    next_step = jnp.minimum(step_i + 1, n_q_steps - 1)

    def tail(near_src, near_dst):
        def body():
            epilogue(near_src)
            prologue(next_step, qn_ref, near_dst)
        return body

    lax.cond(step_i % 2 == 0, tail(near_a_ref, near_b_ref), tail(near_b_ref, near_a_ref))


def _attention(t5_bias, qt3, proj3, vt3, bias_tiles):
    n_pairs = HEADS // 2
    n_q_steps = N_BLOCKS // Q_TILES
    assert FAR_BLOCKS == Q_TILES
    return pl.pallas_call(
        _attn_kernel,
        grid=(BATCH, n_pairs, N_BLOCKS // Q_TILES),
        in_specs=[
            pl.BlockSpec(memory_space=pltpu.SMEM),
            pl.BlockSpec((None, Q_TILES, LANES, MOBA_BLOCK), lambda b, p, i: (b, i, p, 0)),
            pl.BlockSpec((None, Q_TILES, LANES, MOBA_BLOCK),
                         lambda b, p, i: (b, jnp.minimum(i + 1, n_q_steps - 1), p, 0)),
            pl.BlockSpec((None, SEQ, LANES), lambda b, p, i: (b, 0, p)),
            pl.BlockSpec((None, N_BLOCKS, LANES, MOBA_BLOCK), lambda b, p, i: (b, 0, p, 0)),
            pl.BlockSpec((2, 2, MOBA_BLOCK, MOBA_BLOCK), lambda b, p, i: (p, 0, 0, 0)),
        ],
        out_specs=pl.BlockSpec((None, Q_TILES * MOBA_BLOCK, LANES), lambda b, p, i: (b, i, p)),
        out_shape=jax.ShapeDtypeStruct((BATCH, SEQ, WIDTH), BF16),
        scratch_shapes=[
            pltpu.VMEM((LANES, LANES), BF16),
            pltpu.VMEM((LANES, LANES), BF16),
            pltpu.VMEM((SEQ, LANES), BF16),
            pltpu.VMEM((2, FAR_KEYS, 2 * Q_TILES * MOBA_BLOCK), F32),
            pltpu.VMEM((2 * Q_TILES, 2 * MOBA_BLOCK, MOBA_BLOCK), F32),
            pltpu.VMEM((2 * Q_TILES, 2 * MOBA_BLOCK, MOBA_BLOCK), F32),
            pltpu.VMEM((2 * LANES, 2 * Q_TILES * MOBA_BLOCK), BF16),
            pltpu.VMEM((2 * Q_TILES, 8, MOBA_BLOCK), F32),
        ],
        compiler_params=pltpu.CompilerParams(
            dimension_semantics=("arbitrary", "arbitrary", "arbitrary"),
            vmem_limit_bytes=VMEM_LIMIT),
        name="moba_attention",
    )(t5_bias, qt3, qt3, proj3, vt3, bias_tiles)


def _hgrn_kernel(q_ref, f_ref, i_ref, g_ref, lbl_ref, nw_ref, o_ref,
                 state_ref, bpad_ref, kpad_ref, vpad_ref, shb_ref, shk_ref, shv_ref,
                 ones_ref, mask_ref):
    C = HGRN_CHUNK
    half = WIDTH // 2
    assert HEAD_DIM == 1 << 6

    @pl.when(pl.program_id(1) == 0)
    def _():
        state_ref[...] = jnp.zeros_like(state_ref)
        sr = lax.broadcasted_iota(jnp.int32, (half, half), 0) >> 6
        sc = lax.broadcasted_iota(jnp.int32, (half, half), 1) >> 6
        same = jnp.where(sr == sc, 1.0, 0.0)
        ones_ref[...] = same.astype(BF16)
        mask_ref[...] = same

    zpad = jnp.zeros((8, WIDTH), F32)
    bpad_ref[0:8, :] = zpad
    kpad_ref[0:8, :] = zpad
    vpad_ref[0:8, :] = zpad

    l0 = lbl_ref[0:1, :]
    l1 = lbl_ref[1:2, :]
    lmx = jnp.maximum(l0, l1)
    e0 = jnp.exp(l0 - lmx)
    lb = e0 / (e0 + jnp.exp(l1 - lmx))
    nw = nw_ref[...]

    ri = lax.broadcasted_iota(jnp.int32, (C, C), 0)
    ci = lax.broadcasted_iota(jnp.int32, (C, C), 1)
    tril = jnp.where(ri >= ci, 1.0, 0.0).astype(BF16)

    def head_sum(t):
        return jnp.concatenate(
            [jnp.dot(t[:, 0:half], ones_ref[...], preferred_element_type=F32),
             jnp.dot(t[:, half:], ones_ref[...], preferred_element_type=F32)], axis=1)

    def prepare(r0):
        qq = q_ref[r0:r0 + C, :].astype(F32)
        qq = qq * _sigmoid(qq)
        f = lb + (1.0 - lb) * _sigmoid(f_ref[r0:r0 + C, :])
        gl = jnp.log(f)
        kk = 1.0 - f
        vv = i_ref[r0:r0 + C, :].astype(F32)
        g_hi = gl.astype(BF16)
        g_r = gl - g_hi.astype(F32)
        g_mid = g_r.astype(BF16)
        g_lo = (g_r - g_mid.astype(F32)).astype(BF16)
        b = (jnp.dot(tril, g_hi, preferred_element_type=F32)
             + jnp.dot(tril, g_mid, preferred_element_type=F32)
             + jnp.dot(tril, g_lo, preferred_element_type=F32))
        return qq, kk, vv, b * LOG2E

    def intra_pairs(qq, kk, vv, b):
        bpad_ref[8:8 + C, :] = b
        kpad_ref[8:8 + C, :] = kk
        vpad_ref[8:8 + C, :] = vv
        for dd in range(8):
            shb_ref[dd] = bpad_ref[8 - dd:8 - dd + C, :]
            shk_ref[dd] = kpad_ref[8 - dd:8 - dd + C, :]
            shv_ref[dd] = vpad_ref[8 - dd:8 - dd + C, :]

        o_parts = []
        for j in range(C // 8):
            rows = C - 8 * j
            bt = b[8 * j:, :]
            qt = qq[8 * j:, :]
            terms = []
            vs_all = []
            for dd in range(8):
                terms.append(qt * shk_ref[dd, 0:rows, :] * jnp.exp2(bt - shb_ref[dd, 0:rows, :]))
                vs_all.append(shv_ref[dd, 0:rows, :])
            a = head_sum(jnp.concatenate(terms, axis=0).astype(BF16))
            av = a * jnp.concatenate(vs_all, axis=0)
            part = av[0:rows]
            for dd in range(1, 8):
                part = part + av[dd * rows:(dd + 1) * rows]
            o_parts.append(part)
        o = o_parts[0]
        for j in range(1, C // 8):
            o = o + jnp.concatenate([jnp.zeros((8 * j, WIDTH), F32), o_parts[j]], axis=0)
        return o

    pt_i = lax.broadcasted_iota(jnp.int32, (C, half), 0)
    ps_i = lax.broadcasted_iota(jnp.int32, (C, half), 1) & (C - 1)
    causal = jnp.where(ps_i <= pt_i, 1.0, 0.0)

    def intra_matmul(qq, kk, vv, b):
        qd = (qq * jnp.exp2(b)).astype(BF16)
        kd = (kk * jnp.exp2(-b)).astype(BF16)
        vb = vv.astype(BF16)
        outs = []
        for a_i in range(2):
            sl = slice(a_i * half, (a_i + 1) * half)
            k4 = jnp.concatenate([kd[:, sl]] * (half // C), axis=0) * ones_ref[...]
            v4 = jnp.concatenate([vb[:, sl]] * (half // C), axis=0) * ones_ref[...]
            pair = _nt_dot(qd[:, sl], k4) * causal
            outs.append(jnp.dot(pair.astype(BF16), v4, preferred_element_type=F32))
        return jnp.concatenate(outs, axis=1)

    def finish(r0, o, qq, kk, vv, b):
        qe = (qq * jnp.exp2(b)).astype(BF16)
        b_last = b[C - 1:C, :]
        kd = kk * jnp.exp2(b_last - b)
        dec = jnp.exp2(b_last)
        inter = []
        for a_i in range(2):
            sl = slice(a_i * half, (a_i + 1) * half)
            st = state_ref[a_i]
            inter.append(_nt_dot(qe[:, sl], st.astype(BF16)))
            upd = _tn_dot(vv[:, sl].astype(BF16), kd[:, sl].astype(BF16))
            state_ref[a_i] = st * dec[:, sl] + upd * mask_ref[...]
        o = o + jnp.concatenate(inter, axis=1)

        ms = head_sum((o * o).astype(BF16)) * (1.0 / HEAD_DIM)
        r = o * lax.rsqrt(ms + RMS_EPS) * nw
        r = r * _sigmoid(g_ref[r0:r0 + C, :].astype(F32))
        o_ref[r0:r0 + C, :] = r.astype(BF16)

    chunks = [prepare(u * C) for u in range(HGRN_ROWS // C)]
    lowest = chunks[0][3][C - 1:C, :]
    for ch in chunks[1:]:
        lowest = jnp.minimum(lowest, ch[3][C - 1:C, :])
    safe = jnp.min(lowest) > -HGRN_SAFE_LOG2

    def run(intra):
        def body():
            for u, (qq, kk, vv, b) in enumerate(chunks):
                finish(u * C, intra(qq, kk, vv, b), qq, kk, vv, b)
        return body

    lax.cond(safe, run(intra_matmul), run(intra_pairs))


def _hgrn(proj3, f3, lb_logits, norm_w):
    blk = lambda col: pl.BlockSpec((None, HGRN_ROWS, WIDTH), lambda b, i, col=col: (b, i, col))
    return pl.pallas_call(
        _hgrn_kernel,
        grid=(BATCH, SEQ // HGRN_ROWS),
        in_specs=[
            blk(COL_HQ),
            pl.BlockSpec((None, HGRN_ROWS, WIDTH), lambda b, i: (b, i, 0)),
            blk(COL_HI),
            blk(COL_HG),
            pl.BlockSpec((2, WIDTH), lambda b, i: (0, 0)),
            pl.BlockSpec((1, WIDTH), lambda b, i: (0, 0)),
        ],
        out_specs=pl.BlockSpec((None, HGRN_ROWS, WIDTH), lambda b, i: (b, i, 0)),
        out_shape=jax.ShapeDtypeStruct((BATCH, SEQ, WIDTH), BF16),
        scratch_shapes=[
            pltpu.VMEM((2, WIDTH // 2, WIDTH // 2), F32),
            pltpu.VMEM((8 + HGRN_CHUNK, WIDTH), F32),
            pltpu.VMEM((8 + HGRN_CHUNK, WIDTH), F32),
            pltpu.VMEM((8 + HGRN_CHUNK, WIDTH), F32),
            pltpu.VMEM((8, HGRN_CHUNK, WIDTH), F32),
            pltpu.VMEM((8, HGRN_CHUNK, WIDTH), F32),
            pltpu.VMEM((8, HGRN_CHUNK, WIDTH), F32),
            pltpu.VMEM((WIDTH // 2, WIDTH // 2), BF16),
            pltpu.VMEM((WIDTH // 2, WIDTH // 2), F32),
        ],
        compiler_params=pltpu.CompilerParams(
            dimension_semantics=("arbitrary", "arbitrary"), vmem_limit_bytes=VMEM_LIMIT),
        name="hgrn2",
    )(proj3, f3, proj3, proj3, lb_logits, norm_w)


def _out_kernel(a_ref, r_ref, x_ref, wo_ref, g_ref, b_ref, wr_ref, br_ref,
                x1_ref, comb_ref):
    mix = jnp.dot(a_ref[...], wo_ref[0:WIDTH, :], preferred_element_type=F32)
    mix = mix + jnp.dot(r_ref[...], wo_ref[WIDTH:, :], preferred_element_type=F32)
    x1 = _layer_norm(ALPHA * x_ref[...] + mix, g_ref[...], b_ref[...])
    x1_ref[...] = x1

    x_hi = x1.astype(BF16)
    x_lo = (x1 - x_hi.astype(F32)).astype(BF16)
    wr = wr_ref[...]
    w_hi = wr.astype(BF16)
    w_lo = (wr - w_hi.astype(F32)).astype(BF16)
    logits = (jnp.dot(x_hi, w_hi, preferred_element_type=F32)
              + jnp.dot(x_lo, w_hi, preferred_element_type=F32)
              + jnp.dot(x_hi, w_lo, preferred_element_type=F32)) + br_ref[...]
    assert EXPERTS_PER_GROUP == 1 << 2
    lane_i = lax.broadcasted_iota(jnp.int32, logits.shape, 1)
    lane = lane_i.astype(F32)
    grp_of_lane = (lane_i >> 2).astype(F32)
    none = float(LANES)
    is_g = (lane_i >= N_EXPERTS) & (lane_i < N_EXPERTS + N_GROUPS)
    gl = jnp.where(is_g, logits, -jnp.inf)
    ge = jnp.exp(gl - jnp.max(gl, axis=-1, keepdims=True))
    gp = ge / jnp.sum(ge, axis=-1, keepdims=True)
    g_w = jnp.max(gp, axis=-1, keepdims=True)
    g_lane = jnp.min(jnp.where(is_g & (gp == g_w), lane, none), axis=-1, keepdims=True)
    g_idx = g_lane - float(N_EXPERTS)

    in_grp = (lane_i < N_EXPERTS) & (grp_of_lane == g_idx)
    el = jnp.where(in_grp, logits, -jnp.inf)
    ee = jnp.exp(el - jnp.max(el, axis=-1, keepdims=True))
    ep = ee / jnp.sum(ee, axis=-1, keepdims=True)
    p1 = jnp.max(ep, axis=-1, keepdims=True)
    i1 = jnp.min(jnp.where(in_grp & (ep == p1), lane, none), axis=-1, keepdims=True)
    rest = in_grp & (lane != i1)
    ep2 = jnp.where(rest, ep, -1.0)
    p2 = jnp.max(ep2, axis=-1, keepdims=True)
    i2 = jnp.min(jnp.where(rest & (ep2 == p2), lane, none), axis=-1, keepdims=True)
    den = p1 + p2
    comb = jnp.where(lane == i1, g_w * (p1 / den), 0.0)
    comb = jnp.where(lane == i2, g_w * (p2 / den), comb)
    comb_ref[...] = jnp.where(lane_i == GROUP_LANE, g_idx, comb)


def _out_proj(a2, r2, x2, wo_b, g1, b1, w_route, b_route):
    row = lambda w: pl.BlockSpec((OUT_TM, w), lambda i: (i, 0))
    full = lambda s: pl.BlockSpec(s, lambda i: (0, 0))
    return pl.pallas_call(
        _out_kernel,
        grid=(N_TOK // OUT_TM,),
        in_specs=[row(WIDTH), row(WIDTH), row(D_MODEL), full((D_MODEL, D_MODEL)),
                  full((1, D_MODEL)), full((1, D_MODEL)), full((D_MODEL, LANES)), full((1, LANES))],
        out_specs=[row(D_MODEL), row(LANES)],
        out_shape=[jax.ShapeDtypeStruct((N_TOK, D_MODEL), F32),
                   jax.ShapeDtypeStruct((N_TOK, LANES), F32)],
        compiler_params=pltpu.CompilerParams(
            dimension_semantics=("arbitrary",), vmem_limit_bytes=VMEM_LIMIT),
        name="out_proj_ln_route",
    )(a2, r2, x2, wo_b, g1, b1, w_route, b_route)


def _moe_kernel(x1_ref, comb_ref, wg_ref, wu_ref, wd_ref, g_ref, b_ref, o_ref,
                pt_ref, xs_ref, cs_ref, acc_ref, win_ref):
    step = pl.program_id(1)
    assert MOE_ALIGN == 1 << 7 and EXPERTS_PER_GROUP == 1 << 2
    assert EXPERTS_PER_GROUP % MOE_EXPERTS_PER_STEP == 0

    @pl.when(step == 0)
    def _():
        comb = comb_ref[...]
        lane = lax.broadcasted_iota(jnp.int32, comb.shape, 1)
        lane_f = lane.astype(F32)
        gid = jnp.sum(jnp.where(lane == GROUP_LANE, comb, 0.0), axis=-1, keepdims=True)
        mine = lane_f == gid
        ti = lax.broadcasted_iota(jnp.int32, (MOE_TM, MOE_TM), 0)
        tj = lax.broadcasted_iota(jnp.int32, (MOE_TM, MOE_TM), 1)
        tril = jnp.where(ti >= tj, 1.0, 0.0).astype(BF16)
        incl = jnp.dot(tril, jnp.where(mine, 1.0, 0.0).astype(BF16),
                       preferred_element_type=F32)
        rank = jnp.sum(jnp.where(mine, incl, 0.0), axis=-1, keepdims=True) - 1.0
        counts = incl[MOE_TM - 1:MOE_TM, :]
        lane_row = lane[0:1, :]
        start = jnp.int32(0)
        base = jnp.zeros_like(gid)
        for g in range(N_GROUPS):
            cnt = jnp.sum(jnp.where(lane_row == g, counts, 0.0)).astype(jnp.int32)
            chunks = (cnt + (MOE_ALIGN - 1)) >> 7
            win_ref[0, g] = start
            win_ref[1, g] = chunks >> 1
            win_ref[2, g] = chunks & 1
            base = jnp.where(gid == float(g), start.astype(F32), base)
            start = start + chunks * MOE_ALIGN
        pos = base + rank
        r_iota = lax.broadcasted_iota(jnp.int32, (MOE_TM, MOE_ROWS), 1).astype(F32)
        pt = jnp.where(r_iota == pos, 1.0, 0.0).astype(BF16)
        pt_ref[...] = pt
        xs_ref[...] = _tn_dot(pt, x1_ref[...].astype(BF16)).astype(BF16)
        c_hi = comb.astype(BF16)
        c_lo = (comb - c_hi.astype(F32)).astype(BF16)
        cs_ref[...] = _tn_dot(pt, c_hi) + _tn_dot(pt, c_lo)
        acc_ref[...] = jnp.zeros_like(acc_ref)

    e0 = step * MOE_EXPERTS_PER_STEP
    g = e0 >> 2
    row0 = win_ref[0, g]
    n_full = win_ref[1, g]

    def window(start, rows):
        r0 = pl.multiple_of(start, MOE_ALIGN)
        xw = xs_ref[pl.ds(r0, rows), :]
        csw = cs_ref[pl.ds(r0, rows), :]
        lane_w = lax.broadcasted_iota(jnp.int32, (rows, LANES), 1)
        y = None
        for j in range(MOE_EXPERTS_PER_STEP):
            hg = jnp.dot(xw, wg_ref[j], preferred_element_type=F32)
            hu = jnp.dot(xw, wu_ref[j], preferred_element_type=F32)
            h = (hg * _sigmoid(hg) * hu).astype(BF16)
            c = jnp.sum(jnp.where(lane_w == e0 + j, csw, 0.0), axis=-1, keepdims=True)
            yj = c * jnp.dot(h, wd_ref[j], preferred_element_type=F32)
            y = yj if y is None else y + yj
        acc_ref[pl.ds(r0, rows), :] += y

    def full_window(k, _):
        window(row0 + k * MOE_WIN, MOE_WIN)
        return 0

    lax.fori_loop(0, n_full, full_window, 0)

    @pl.when(win_ref[2, g] == 1)
    def _():
        window(row0 + n_full * MOE_WIN, MOE_ALIGN)

    @pl.when(step == N_EXPERTS // MOE_EXPERTS_PER_STEP - 1)
    def _():
        moe = jnp.dot(pt_ref[...], acc_ref[...].astype(BF16), preferred_element_type=F32)
        o_ref[...] = _layer_norm(ALPHA * x1_ref[...] + moe, g_ref[...], b_ref[...])


def _moe(x1, comb, wg_b, wu_b, wd_b, g2, b2):
    return pl.pallas_call(
        _moe_kernel,
        grid=(N_TOK // MOE_TM, N_EXPERTS // MOE_EXPERTS_PER_STEP),
        in_specs=[
            pl.BlockSpec((MOE_TM, D_MODEL), lambda i, e: (i, 0)),
            pl.BlockSpec((MOE_TM, LANES), lambda i, e: (i, 0)),
            pl.BlockSpec((MOE_EXPERTS_PER_STEP, D_MODEL, D_EXPERT), lambda i, e: (e, 0, 0)),
            pl.BlockSpec((MOE_EXPERTS_PER_STEP, D_MODEL, D_EXPERT), lambda i, e: (e, 0, 0)),
            pl.BlockSpec((MOE_EXPERTS_PER_STEP, D_EXPERT, D_MODEL), lambda i, e: (e, 0, 0)),
            pl.BlockSpec((1, D_MODEL), lambda i, e: (0, 0)),
            pl.BlockSpec((1, D_MODEL), lambda i, e: (0, 0)),
        ],
        out_specs=pl.BlockSpec((MOE_TM, D_MODEL), lambda i, e: (i, 0)),
        out_shape=jax.ShapeDtypeStruct((N_TOK, D_MODEL), F32),
        scratch_shapes=[
            pltpu.VMEM((MOE_TM, MOE_ROWS), BF16),
            pltpu.VMEM((MOE_ROWS, D_MODEL), BF16),
            pltpu.VMEM((MOE_ROWS, LANES), F32),
            pltpu.VMEM((MOE_ROWS, D_MODEL), F32),
            pltpu.SMEM((3, N_GROUPS), jnp.int32),
        ],
        compiler_params=pltpu.CompilerParams(
            dimension_semantics=("arbitrary", "arbitrary"), vmem_limit_bytes=MOE_VMEM_LIMIT),
        name="moe_experts_ln",
    )(x1, comb, wg_b, wu_b, wd_b, g2, b2)


def kernel(x, w_in, t5_bias, hgrn_lb_logits, hgrn_norm_w, w_o, ln1_g, ln1_b, w_group, b_group,
           w_expert, b_expert, w_gate, w_up, w_down, ln2_g, ln2_b):
    x2 = x.reshape(N_TOK, D_MODEL)
    w_in_b = w_in[0].astype(BF16)
    w_qt_b = w_in_b[:, 0:WIDTH].T
    w_vt_b = w_in_b[:, 2 * WIDTH:3 * WIDTH].T
    proj, f_logits, qt3, vt3 = _proj(x2, w_in_b, w_qt_b, w_vt_b)
    proj3 = proj.reshape(BATCH, SEQ, ROW_WIDTH)

    a = _attention(t5_bias, qt3, proj3, vt3, _bias_tiles(t5_bias))
    r = _hgrn(proj3, f_logits.reshape(BATCH, SEQ, WIDTH), hgrn_lb_logits, hgrn_norm_w)

    pad = LANES - N_EXPERTS - N_GROUPS
    w_route = jnp.concatenate(
        [w_expert[0].transpose(1, 0, 2).reshape(D_MODEL, N_EXPERTS), w_group[0],
         jnp.zeros((D_MODEL, pad), F32)], axis=1)
    b_route = jnp.concatenate(
        [b_expert[0].reshape(N_EXPERTS), b_group[0], jnp.zeros((pad,), F32)]).reshape(1, LANES)
    x1, comb = _out_proj(
        a.reshape(N_TOK, WIDTH), r.reshape(N_TOK, WIDTH), x2, w_o[0].astype(BF16),
        ln1_g, ln1_b, w_route, b_route)

    out = _moe(x1, comb, w_gate[0].astype(BF16), w_up[0].astype(BF16),
               w_down[0].astype(BF16), ln2_g, ln2_b)
    return out.reshape(BATCH, SEQ, D_MODEL)
```

```python
import functools
import math

import jax
import jax.numpy as jnp
from jax import lax
from jax.experimental import pallas as pl
from jax.experimental.pallas import tpu as pltpu

F32 = jnp.float32
BF16 = jnp.bfloat16

D_MODEL = 1024
BATCH = 2
SEQ = 8192
N_TOK = BATCH * SEQ
HEADS = 8
HEAD_DIM = 64
WIDTH = HEADS * HEAD_DIM
IN_WIDTH = 7 * WIDTH
ROW_WIDTH = 5 * WIDTH
COL_HQ, COL_HF, COL_HI, COL_HG = 1, 2, 3, 4
MOBA_BLOCK = 256
N_BLOCKS = SEQ // MOBA_BLOCK
MOBA_TOPK = 3
HGRN_CHUNK = 64
HGRN_SAFE_LOG2 = 100.0
N_BUCKETS = 32
MAX_DISTANCE = 128
N_GROUPS = 4
EXPERTS_PER_GROUP = 4
N_EXPERTS = 16
D_EXPERT = 512
LN_EPS = 1e-5
RMS_EPS = 1e-6
ALPHA = 2.0 ** 0.25
LOG2E = math.log2(math.e)

LANES = 128
BF16_SUBLANES = 16
NEG = -1e30
VMEM_LIMIT = 48 * 1024 * 1024
MOE_VMEM_LIMIT = 56 * 1024 * 1024

PROJ_TM = 2048
OUT_TM = 1024
MOE_TM = 1024
MOE_ALIGN = 128
MOE_WIN = 256
MOE_EXPERTS_PER_STEP = 2
MOE_ROWS = MOE_TM + N_GROUPS * MOE_ALIGN
GROUP_LANE = N_EXPERTS
HGRN_ROWS = 512
Q_TILES = 2
FAR_BLOCKS = 2
FAR_KEYS = FAR_BLOCKS * MOBA_BLOCK
N_FAR_STEPS = N_BLOCKS // FAR_BLOCKS

BIAS_HI_LANE = N_BLOCKS
BIAS_LO_LANE = N_BLOCKS + 1


def _nt_dot(a, b):
    return lax.dot_general(a, b, (((1,), (1,)), ((), ())), preferred_element_type=F32)


def _tn_dot(a, b):
    return lax.dot_general(a, b, (((0,), (0,)), ((), ())), preferred_element_type=F32)


def _sigmoid(x):
    return 1.0 / (1.0 + jnp.exp2(x * (-LOG2E)))


def _layer_norm(y, g, b):
    mu = jnp.mean(y, axis=-1, keepdims=True)
    d = y - mu
    var = jnp.mean(d * d, axis=-1, keepdims=True)
    return d * lax.rsqrt(var + LN_EPS) * g + b


def _bias_tile_kernel(tab_ref, out_ref):
    h = pl.program_id(0)
    ki = lax.broadcasted_iota(jnp.int32, (MOBA_BLOCK, MOBA_BLOCK), 0)
    qi = lax.broadcasted_iota(jnp.int32, (MOBA_BLOCK, MOBA_BLOCK), 1)
    max_exact = N_BUCKETS // 2
    for w in range(2):
        n = jnp.maximum(qi - ki + MOBA_BLOCK * w, 0)
        nf = jnp.maximum(n, 1).astype(F32)
        scaled = (jnp.log(nf / max_exact) / math.log(MAX_DISTANCE / max_exact)
                  * (N_BUCKETS - max_exact))
        large = max_exact + jnp.floor(jnp.maximum(scaled, 0.0)).astype(jnp.int32)
        large = jnp.minimum(large, N_BUCKETS - 1)
        bucket = jnp.where(n < max_exact, n, large)
        acc = jnp.zeros((MOBA_BLOCK, MOBA_BLOCK), F32)
        for bk in range(N_BUCKETS):
            acc = jnp.where(bucket == bk, tab_ref[bk, h], acc)
        out_ref[0, w] = acc


def _bias_tiles(t5_bias):
    return pl.pallas_call(
        _bias_tile_kernel,
        grid=(HEADS,),
        in_specs=[pl.BlockSpec(memory_space=pltpu.SMEM)],
        out_specs=pl.BlockSpec((1, 2, MOBA_BLOCK, MOBA_BLOCK), lambda h: (h, 0, 0, 0)),
        out_shape=jax.ShapeDtypeStruct((HEADS, 2, MOBA_BLOCK, MOBA_BLOCK), F32),
        name="t5_bias_tiles",
    )(t5_bias)


def _proj_kernel(x_ref, w_ref, wqt_ref, wvt_ref, o_ref, f_ref, qt_ref, vt_ref, xb_ref):
    n = pl.program_id(1)

    def feature_major(wt_ref, t_ref):
        t = _nt_dot(wt_ref[...], xb_ref[...]).astype(BF16)
        for c in range(PROJ_TM // MOBA_BLOCK):
            t_ref[c] = t[:, c * MOBA_BLOCK:(c + 1) * MOBA_BLOCK]

    @pl.when(n == 0)
    def _():
        xb_ref[...] = x_ref[...].astype(BF16)
        feature_major(wqt_ref, qt_ref)

    acc = jnp.dot(xb_ref[...], w_ref[...], preferred_element_type=F32)
    o_ref[...] = acc.astype(BF16)

    @pl.when(n == 1)
    def _():
        feature_major(wvt_ref, vt_ref)

    @pl.when(n == COL_HF)
    def _():
        f_ref[...] = acc


def _proj(x2, w_in_b, w_qt_b, w_vt_b):
    tiles_per_seq = SEQ // PROJ_TM
    blocks_per_tile = PROJ_TM // MOBA_BLOCK
    t_spec = pl.BlockSpec((None, blocks_per_tile, WIDTH, MOBA_BLOCK),
                          lambda i, n: (i // tiles_per_seq, i % tiles_per_seq, 0, 0))
    t_shape = jax.ShapeDtypeStruct((BATCH, N_BLOCKS, WIDTH, MOBA_BLOCK), BF16)
    return pl.pallas_call(
        _proj_kernel,
        grid=(N_TOK // PROJ_TM, ROW_WIDTH // WIDTH),
        in_specs=[
            pl.BlockSpec((PROJ_TM, D_MODEL), lambda i, n: (i, 0)),
            pl.BlockSpec((D_MODEL, WIDTH), lambda i, n: (0, jnp.where(n == 0, 1, n + 2))),
            pl.BlockSpec((WIDTH, D_MODEL), lambda i, n: (0, 0)),
            pl.BlockSpec((WIDTH, D_MODEL), lambda i, n: (0, 0)),
        ],
        out_specs=[
            pl.BlockSpec((PROJ_TM, WIDTH), lambda i, n: (i, n)),
            pl.BlockSpec((PROJ_TM, WIDTH), lambda i, n: (i, 0)),
            t_spec,
            t_spec,
        ],
        out_shape=[
            jax.ShapeDtypeStruct((N_TOK, ROW_WIDTH), BF16),
            jax.ShapeDtypeStruct((N_TOK, WIDTH), F32),
            t_shape,
            t_shape,
        ],
        scratch_shapes=[pltpu.VMEM((PROJ_TM, D_MODEL), BF16)],
        compiler_params=pltpu.CompilerParams(
            dimension_semantics=("arbitrary", "arbitrary"), vmem_limit_bytes=VMEM_LIMIT),
        name="in_proj",
    )(x2, w_in_b, w_qt_b, w_vt_b)


def _colreduce(x, op):
    r = x[0:8]
    for i in range(1, x.shape[0] // 8):
        r = op(r, x[8 * i:8 * i + 8])
    for shift in (4, 2, 1):
        r = op(r, pltpu.roll(r, shift, 0))
    return r


def _attn_kernel(tab_ref, qt_ref, qn_ref, k_ref, vt_ref, bias_ref, o_ref,
                 kmh_ref, kml_ref, ext_ref, s_ref, near_a_ref, near_b_ref, qcat_ref, pmask_ref):
    assert MOBA_BLOCK == 1 << 8
    pair = pl.program_id(1)
    step_i = pl.program_id(2)
    lane = lax.broadcasted_iota(jnp.int32, (MOBA_BLOCK, LANES), 1)

    @pl.when(step_i == 0)
    def _():
        r = lax.broadcasted_iota(jnp.int32, (LANES, SEQ), 1)
        nb = lax.broadcasted_iota(jnp.int32, (LANES, SEQ), 0)
        avg = jnp.where((r >> 8) == nb, 1.0 / MOBA_BLOCK, 0.0).astype(BF16)
        km = jnp.dot(avg, k_ref[...], preferred_element_type=F32)
        hi = km.astype(BF16)
        kmh_ref[...] = hi
        kml_ref[...] = (km - hi.astype(F32)).astype(BF16)

    @pl.when((pl.program_id(0) == 0) & (pair == 0) & (step_i == 0))
    def _():
        kr = lax.broadcasted_iota(jnp.int32, (SEQ, LANES), 0) >> 8
        kl = lax.broadcasted_iota(jnp.int32, (SEQ, LANES), 1)
        ext_ref[...] = jnp.where((kl == kr) | (kl == BIAS_HI_LANE) | (kl == BIAS_LO_LANE),
                                 1.0, 0.0).astype(BF16)

    krow = lax.broadcasted_iota(jnp.int32, (MOBA_BLOCK, MOBA_BLOCK), 0)
    qcol = lax.broadcasted_iota(jnp.int32, (MOBA_BLOCK, MOBA_BLOCK), 1)
    frow = lax.broadcasted_iota(jnp.int32, (LANES, MOBA_BLOCK), 0)
    nrow = lax.broadcasted_iota(jnp.int32, (N_BLOCKS, MOBA_BLOCK), 0)
    nrow_f = nrow.astype(F32)
    brow = lax.broadcasted_iota(jnp.int32, (8, MOBA_BLOCK), 0)
    ones_rows = jnp.ones((BF16_SUBLANES, MOBA_BLOCK), BF16)
    zero_rows = jnp.zeros((LANES - N_BLOCKS - 8, MOBA_BLOCK), F32)
    def v_aug(hh, block):
        vt = vt_ref[block, hh * HEAD_DIM:(hh + 1) * HEAD_DIM, :]
        return jnp.concatenate([vt, ones_rows], axis=0)

    def rows_to_tile(x8):
        return jnp.concatenate([x8] * (N_BLOCKS // 8), axis=0)

    n_tiles = 2 * Q_TILES
    n_q_steps = N_BLOCKS // Q_TILES

    def scores(step, slot):
        start = pl.multiple_of(jnp.minimum(step, N_FAR_STEPS - 1) * FAR_KEYS, FAR_KEYS)
        k_aug = jnp.concatenate([k_ref[pl.ds(start, FAR_KEYS), :],
                                 ext_ref[pl.ds(start, FAR_KEYS), :]], axis=1)
        s_ref[slot] = jnp.dot(k_aug, qcat_ref[...],
                              preferred_element_type=F32)

    def prologue(q_step, q_src, near_dst):
        km_hi = kmh_ref[0:N_BLOCKS, :]
        km_lo = kml_ref[0:N_BLOCKS, :]
        q_far = []
        for t in range(Q_TILES):
            own = q_step * Q_TILES + t
            prev = jnp.maximum(own - 1, 0)
            qt2 = q_src[t]
            k_own = k_ref[pl.ds(pl.multiple_of(own * MOBA_BLOCK, MOBA_BLOCK), MOBA_BLOCK), :]
            k_prev = k_ref[pl.ds(pl.multiple_of(prev * MOBA_BLOCK, MOBA_BLOCK), MOBA_BLOCK), :]
            for hh in range(2):
                head = pair * 2 + hh
                in_head = (frow >= hh * HEAD_DIM) & (frow < (hh + 1) * HEAD_DIM)
                qs = (jnp.where(in_head, qt2, jnp.zeros_like(qt2))
                      * jnp.asarray(HEAD_DIM ** -0.5, BF16))

                gate = (jnp.dot(km_hi, qs, preferred_element_type=F32)
                        + jnp.dot(km_lo, qs, preferred_element_type=F32))
                eligible = nrow < own
                g = jnp.where(eligible, gate, -jnp.inf)
                sel = jnp.zeros((N_BLOCKS, MOBA_BLOCK), jnp.bool_)
                for _ in range(MOBA_TOPK):
                    mx = rows_to_tile(_colreduce(g, jnp.maximum))
                    idx = rows_to_tile(_colreduce(jnp.where(g == mx, nrow_f, float(LANES)),
                                                  jnp.minimum))
                    hit = nrow_f == idx
                    sel = sel | hit
                    g = jnp.where(hit, -jnp.inf, g)
                sel = sel & eligible

                b31 = jnp.full((8, MOBA_BLOCK), tab_ref[N_BUCKETS - 1, head], F32)
                b31_hi = b31.astype(BF16).astype(F32)
                bias_rows = jnp.where(brow == 0, b31_hi, jnp.where(brow == 1, b31 - b31_hi, 0.0))

                aug = jnp.concatenate([jnp.where(sel & (nrow != own - 1), 0.0, NEG), bias_rows,
                                       zero_rows], axis=0)
                q_far.append(jnp.concatenate([qs, aug.astype(BF16)], axis=0))

                prev_sel = _colreduce(
                    jnp.where((nrow == own - 1) & jnp.logical_not(sel), NEG, 0.0), jnp.add)
                pmask_ref[2 * t + hh] = jnp.where(own >= 1, prev_sel, NEG)
                near_dst[2 * t + hh, 0:MOBA_BLOCK, :] = (
                    jnp.dot(k_prev, qs, preferred_element_type=F32) + bias_ref[hh, 1])
                s_own = jnp.dot(k_own, qs, preferred_element_type=F32) + bias_ref[hh, 0]
                near_dst[2 * t + hh, MOBA_BLOCK:, :] = jnp.where(qcol >= krow, s_own, NEG)
        qcat_ref[...] = jnp.concatenate(q_far, axis=1)
        scores(0, 0)

    @pl.when(step_i == 0)
    def _():
        prologue(0, qt_ref, near_a_ref)

    def absorb_scores(s, va, m, acc):
        m_new = jnp.maximum(m, jnp.max(s, axis=0, keepdims=True))
        alpha = jnp.exp(m - m_new)
        p = jnp.exp(s - m_new).astype(BF16)
        return m_new, alpha * acc + jnp.dot(va, p, preferred_element_type=F32)

    def absorb(step, slot, carry):
        block0 = step * FAR_BLOCKS
        out = []
        for j in range(n_tiles):
            hh = j % 2
            s = s_ref[slot, :, j * MOBA_BLOCK:(j + 1) * MOBA_BLOCK]
            va = jnp.concatenate([v_aug(hh, block0 + c) for c in range(FAR_BLOCKS)], axis=1)
            out += absorb_scores(s, va, carry[2 * j], carry[2 * j + 1])
        return tuple(out)

    n_steps = step_i
    carry = []
    for _ in range(n_tiles):
        carry += [jnp.full((1, MOBA_BLOCK), NEG, F32),
                  jnp.zeros((HEAD_DIM + BF16_SUBLANES, MOBA_BLOCK), F32)]

    def two_steps(base, carry):
        scores(base + 1, 1)
        carry = absorb(base, 0, carry)
        scores(base + 2, 0)
        return absorb(base + 1, 1, carry)

    def far_body(i, carry):
        return two_steps(4 * i + 2, two_steps(4 * i, carry))

    n_quads = n_steps // 4
    carry = lax.fori_loop(0, n_quads, far_body, tuple(carry))
    carry = lax.cond(n_steps % 4 >= 2,
                     lambda c: two_steps(4 * n_quads, c), lambda c: c, carry)
    carry = lax.cond(n_steps % 2 == 1,
                     lambda c: absorb(n_steps - 1, 0, c), lambda c: c, carry)

    prev_masks = [pmask_ref[j, 0:1, :] for j in range(n_tiles)]

    def epilogue(near_src):
        for t in range(Q_TILES):
            own = step_i * Q_TILES + t
            outs = []
            for hh in range(2):
                j = 2 * t + hh
                va = jnp.concatenate([v_aug(hh, jnp.maximum(own - 1, 0)), v_aug(hh, own)], axis=1)
                s = jnp.concatenate([near_src[j, 0:MOBA_BLOCK, :] + prev_masks[j],
                                     near_src[j, MOBA_BLOCK:, :]], axis=0)
                _, acc = absorb_scores(s, va, carry[2 * j], carry[2 * j + 1])
                outs.append(acc[0:HEAD_DIM] / acc[HEAD_DIM:HEAD_DIM + 1])
            o_ref[t * MOBA_BLOCK:(t + 1) * MOBA_BLOCK, :] = (
                jnp.concatenate(outs, axis=0).T.astype(BF16))

    next_step = jnp.minimum(step_i + 1, n_q_steps - 1)

    def tail(near_src, near_dst):
        def body():
            epilogue(near_src)
            prologue(next_step, qn_ref, near_dst)
        return body

    lax.cond(step_i % 2 == 0, tail(near_a_ref, near_b_ref), tail(near_b_ref, near_a_ref))


def _attention(t5_bias, qt3, proj3, vt3, bias_tiles):
    n_pairs = HEADS // 2
    n_q_steps = N_BLOCKS // Q_TILES
    assert FAR_BLOCKS == Q_TILES
    return pl.pallas_call(
        _attn_kernel,
        grid=(BATCH, n_pairs, N_BLOCKS // Q_TILES),
        in_specs=[
            pl.BlockSpec(memory_space=pltpu.SMEM),
            pl.BlockSpec((None, Q_TILES, LANES, MOBA_BLOCK), lambda b, p, i: (b, i, p, 0)),
            pl.BlockSpec((None, Q_TILES, LANES, MOBA_BLOCK),
                         lambda b, p, i: (b, jnp.minimum(i + 1, n_q_steps - 1), p, 0)),
            pl.BlockSpec((None, SEQ, LANES), lambda b, p, i: (b, 0, p)),
            pl.BlockSpec((None, N_BLOCKS, LANES, MOBA_BLOCK), lambda b, p, i: (b, 0, p, 0)),
            pl.BlockSpec((2, 2, MOBA_BLOCK, MOBA_BLOCK), lambda b, p, i: (p, 0, 0, 0)),
        ],
        out_specs=pl.BlockSpec((None, Q_TILES * MOBA_BLOCK, LANES), lambda b, p, i: (b, i, p)),
        out_shape=jax.ShapeDtypeStruct((BATCH, SEQ, WIDTH), BF16),
        scratch_shapes=[
            pltpu.VMEM((LANES, LANES), BF16),
            pltpu.VMEM((LANES, LANES), BF16),
            pltpu.VMEM((SEQ, LANES), BF16),
            pltpu.VMEM((2, FAR_KEYS, 2 * Q_TILES * MOBA_BLOCK), F32),
            pltpu.VMEM((2 * Q_TILES, 2 * MOBA_BLOCK, MOBA_BLOCK), F32),
            pltpu.VMEM((2 * Q_TILES, 2 * MOBA_BLOCK, MOBA_BLOCK), F32),
            pltpu.VMEM((2 * LANES, 2 * Q_TILES * MOBA_BLOCK), BF16),
            pltpu.VMEM((2 * Q_TILES, 8, MOBA_BLOCK), F32),
        ],
        compiler_params=pltpu.CompilerParams(
            dimension_semantics=("arbitrary", "arbitrary", "arbitrary"),
            vmem_limit_bytes=VMEM_LIMIT),
        name="moba_attention",
    )(t5_bias, qt3, qt3, proj3, vt3, bias_tiles)


def _hgrn_kernel(q_ref, f_ref, i_ref, g_ref, lbl_ref, nw_ref, o_ref,
                 state_ref, bpad_ref, kpad_ref, vpad_ref, shb_ref, shk_ref, shv_ref,
                 ones_ref, mask_ref):
    C = HGRN_CHUNK
    half = WIDTH // 2
    assert HEAD_DIM == 1 << 6

    @pl.when(pl.program_id(1) == 0)
    def _():
        state_ref[...] = jnp.zeros_like(state_ref)
        sr = lax.broadcasted_iota(jnp.int32, (half, half), 0) >> 6
        sc = lax.broadcasted_iota(jnp.int32, (half, half), 1) >> 6
        same = jnp.where(sr == sc, 1.0, 0.0)
        ones_ref[...] = same.astype(BF16)
        mask_ref[...] = same

    zpad = jnp.zeros((8, WIDTH), F32)
    bpad_ref[0:8, :] = zpad
    kpad_ref[0:8, :] = zpad
    vpad_ref[0:8, :] = zpad

    l0 = lbl_ref[0:1, :]
    l1 = lbl_ref[1:2, :]
    lmx = jnp.maximum(l0, l1)
    e0 = jnp.exp(l0 - lmx)
    lb = e0 / (e0 + jnp.exp(l1 - lmx))
    nw = nw_ref[...]

    ri = lax.broadcasted_iota(jnp.int32, (C, C), 0)
    ci = lax.broadcasted_iota(jnp.int32, (C, C), 1)
    tril = jnp.where(ri >= ci, 1.0, 0.0).astype(BF16)

    def head_sum(t):
        return jnp.concatenate(
            [jnp.dot(t[:, 0:half], ones_ref[...], preferred_element_type=F32),
             jnp.dot(t[:, half:], ones_ref[...], preferred_element_type=F32)], axis=1)

    def prepare(r0):
        qq = q_ref[r0:r0 + C, :].astype(F32)
        qq = qq * _sigmoid(qq)
        f = lb + (1.0 - lb) * _sigmoid(f_ref[r0:r0 + C, :])
        gl = jnp.log(f)
        kk = 1.0 - f
        vv = i_ref[r0:r0 + C, :].astype(F32)
        g_hi = gl.astype(BF16)
        g_r = gl - g_hi.astype(F32)
        g_mid = g_r.astype(BF16)
        g_lo = (g_r - g_mid.astype(F32)).astype(BF16)
        b = (jnp.dot(tril, g_hi, preferred_element_type=F32)
             + jnp.dot(tril, g_mid, preferred_element_type=F32)
             + jnp.dot(tril, g_lo, preferred_element_type=F32))
        return qq, kk, vv, b * LOG2E

    def intra_pairs(qq, kk, vv, b):
        bpad_ref[8:8 + C, :] = b
        kpad_ref[8:8 + C, :] = kk
        vpad_ref[8:8 + C, :] = vv
        for dd in range(8):
            shb_ref[dd] = bpad_ref[8 - dd:8 - dd + C, :]
            shk_ref[dd] = kpad_ref[8 - dd:8 - dd + C, :]
            shv_ref[dd] = vpad_ref[8 - dd:8 - dd + C, :]

        o_parts = []
        for j in range(C // 8):
            rows = C - 8 * j
            bt = b[8 * j:, :]
            qt = qq[8 * j:, :]
            terms = []
            vs_all = []
            for dd in range(8):
                terms.append(qt * shk_ref[dd, 0:rows, :] * jnp.exp2(bt - shb_ref[dd, 0:rows, :]))
                vs_all.append(shv_ref[dd, 0:rows, :])
            a = head_sum(jnp.concatenate(terms, axis=0).astype(BF16))
            av = a * jnp.concatenate(vs_all, axis=0)
            part = av[0:rows]
            for dd in range(1, 8):
                part = part + av[dd * rows:(dd + 1) * rows]
            o_parts.append(part)
        o = o_parts[0]
        for j in range(1, C // 8):
            o = o + jnp.concatenate([jnp.zeros((8 * j, WIDTH), F32), o_parts[j]], axis=0)
        return o

    pt_i = lax.broadcasted_iota(jnp.int32, (C, half), 0)
    ps_i = lax.broadcasted_iota(jnp.int32, (C, half), 1) & (C - 1)
    causal = jnp.where(ps_i <= pt_i, 1.0, 0.0)

    def intra_matmul(qq, kk, vv, b):
        qd = (qq * jnp.exp2(b)).astype(BF16)
        kd = (kk * jnp.exp2(-b)).astype(BF16)
        vb = vv.astype(BF16)
        outs = []
        for a_i in range(2):
            sl = slice(a_i * half, (a_i + 1) * half)
            k4 = jnp.concatenate([kd[:, sl]] * (half // C), axis=0) * ones_ref[...]
            v4 = jnp.concatenate([vb[:, sl]] * (half // C), axis=0) * ones_ref[...]
            pair = _nt_dot(qd[:, sl], k4) * causal
            outs.append(jnp.dot(pair.astype(BF16), v4, preferred_element_type=F32))
        return jnp.concatenate(outs, axis=1)

    def finish(r0, o, qq, kk, vv, b):
        qe = (qq * jnp.exp2(b)).astype(BF16)
        b_last = b[C - 1:C, :]
        kd = kk * jnp.exp2(b_last - b)
        dec = jnp.exp2(b_last)
        inter = []
        for a_i in range(2):
            sl = slice(a_i * half, (a_i + 1) * half)
            st = state_ref[a_i]
            inter.append(_nt_dot(qe[:, sl], st.astype(BF16)))
            upd = _tn_dot(vv[:, sl].astype(BF16), kd[:, sl].astype(BF16))
            state_ref[a_i] = st * dec[:, sl] + upd * mask_ref[...]
        o = o + jnp.concatenate(inter, axis=1)

        ms = head_sum((o * o).astype(BF16)) * (1.0 / HEAD_DIM)
        r = o * lax.rsqrt(ms + RMS_EPS) * nw
        r = r * _sigmoid(g_ref[r0:r0 + C, :].astype(F32))
        o_ref[r0:r0 + C, :] = r.astype(BF16)

    chunks = [prepare(u * C) for u in range(HGRN_ROWS // C)]
    lowest = chunks[0][3][C - 1:C, :]
    for ch in chunks[1:]:
        lowest = jnp.minimum(lowest, ch[3][C - 1:C, :])
    safe = jnp.min(lowest) > -HGRN_SAFE_LOG2

    def run(intra):
        def body():
            for u, (qq, kk, vv, b) in enumerate(chunks):
                finish(u * C, intra(qq, kk, vv, b), qq, kk, vv, b)
        return body

    lax.cond(safe, run(intra_matmul), run(intra_pairs))


def _hgrn(proj3, f3, lb_logits, norm_w):
    blk = lambda col: pl.BlockSpec((None, HGRN_ROWS, WIDTH), lambda b, i, col=col: (b, i, col))
    return pl.pallas_call(
        _hgrn_kernel,
        grid=(BATCH, SEQ // HGRN_ROWS),
        in_specs=[
            blk(COL_HQ),
            pl.BlockSpec((None, HGRN_ROWS, WIDTH), lambda b, i: (b, i, 0)),
            blk(COL_HI),
            blk(COL_HG),
            pl.BlockSpec((2, WIDTH), lambda b, i: (0, 0)),
            pl.BlockSpec((1, WIDTH), lambda b, i: (0, 0)),
        ],
        out_specs=pl.BlockSpec((None, HGRN_ROWS, WIDTH), lambda b, i: (b, i, 0)),
        out_shape=jax.ShapeDtypeStruct((BATCH, SEQ, WIDTH), BF16),
        scratch_shapes=[
            pltpu.VMEM((2, WIDTH // 2, WIDTH // 2), F32),
            pltpu.VMEM((8 + HGRN_CHUNK, WIDTH), F32),
            pltpu.VMEM((8 + HGRN_CHUNK, WIDTH), F32),
            pltpu.VMEM((8 + HGRN_CHUNK, WIDTH), F32),
            pltpu.VMEM((8, HGRN_CHUNK, WIDTH), F32),
            pltpu.VMEM((8, HGRN_CHUNK, WIDTH), F32),
            pltpu.VMEM((8, HGRN_CHUNK, WIDTH), F32),
            pltpu.VMEM((WIDTH // 2, WIDTH // 2), BF16),
            pltpu.VMEM((WIDTH // 2, WIDTH // 2), F32),
        ],
        compiler_params=pltpu.CompilerParams(
            dimension_semantics=("arbitrary", "arbitrary"), vmem_limit_bytes=VMEM_LIMIT),
        name="hgrn2",
    )(proj3, f3, proj3, proj3, lb_logits, norm_w)


def _out_kernel(a_ref, r_ref, x_ref, wo_ref, g_ref, b_ref, wr_ref, br_ref,
                x1_ref, comb_ref):
    mix = jnp.dot(a_ref[...], wo_ref[0:WIDTH, :], preferred_element_type=F32)
    mix = mix + jnp.dot(r_ref[...], wo_ref[WIDTH:, :], preferred_element_type=F32)
    x1 = _layer_norm(ALPHA * x_ref[...] + mix, g_ref[...], b_ref[...])
    x1_ref[...] = x1

    x_hi = x1.astype(BF16)
    x_lo = (x1 - x_hi.astype(F32)).astype(BF16)
    wr = wr_ref[...]
    w_hi = wr.astype(BF16)
    w_lo = (wr - w_hi.astype(F32)).astype(BF16)
    logits = (jnp.dot(x_hi, w_hi, preferred_element_type=F32)
              + jnp.dot(x_lo, w_hi, preferred_element_type=F32)
              + jnp.dot(x_hi, w_lo, preferred_element_type=F32)) + br_ref[...]
    assert EXPERTS_PER_GROUP == 1 << 2
    lane_i = lax.broadcasted_iota(jnp.int32, logits.shape, 1)
    lane = lane_i.astype(F32)
    grp_of_lane = (lane_i >> 2).astype(F32)
    none = float(LANES)
    is_g = (lane_i >= N_EXPERTS) & (lane_i < N_EXPERTS + N_GROUPS)
    gl = jnp.where(is_g, logits, -jnp.inf)
    ge = jnp.exp(gl - jnp.max(gl, axis=-1, keepdims=True))
    gp = ge / jnp.sum(ge, axis=-1, keepdims=True)
    g_w = jnp.max(gp, axis=-1, keepdims=True)
    g_lane = jnp.min(jnp.where(is_g & (gp == g_w), lane, none), axis=-1, keepdims=True)
    g_idx = g_lane - float(N_EXPERTS)

    in_grp = (lane_i < N_EXPERTS) & (grp_of_lane == g_idx)
    el = jnp.where(in_grp, logits, -jnp.inf)
    ee = jnp.exp(el - jnp.max(el, axis=-1, keepdims=True))
    ep = ee / jnp.sum(ee, axis=-1, keepdims=True)
    p1 = jnp.max(ep, axis=-1, keepdims=True)
    i1 = jnp.min(jnp.where(in_grp & (ep == p1), lane, none), axis=-1, keepdims=True)
    rest = in_grp & (lane != i1)
    ep2 = jnp.where(rest, ep, -1.0)
    p2 = jnp.max(ep2, axis=-1, keepdims=True)
    i2 = jnp.min(jnp.where(rest & (ep2 == p2), lane, none), axis=-1, keepdims=True)
    den = p1 + p2
    comb = jnp.where(lane == i1, g_w * (p1 / den), 0.0)
    comb = jnp.where(lane == i2, g_w * (p2 / den), comb)
    comb_ref[...] = jnp.where(lane_i == GROUP_LANE, g_idx, comb)


def _out_proj(a2, r2, x2, wo_b, g1, b1, w_route, b_route):
    row = lambda w: pl.BlockSpec((OUT_TM, w), lambda i: (i, 0))
    full = lambda s: pl.BlockSpec(s, lambda i: (0, 0))
    return pl.pallas_call(
        _out_kernel,
        grid=(N_TOK // OUT_TM,),
        in_specs=[row(WIDTH), row(WIDTH), row(D_MODEL), full((D_MODEL, D_MODEL)),
                  full((1, D_MODEL)), full((1, D_MODEL)), full((D_MODEL, LANES)), full((1, LANES))],
        out_specs=[row(D_MODEL), row(LANES)],
        out_shape=[jax.ShapeDtypeStruct((N_TOK, D_MODEL), F32),
                   jax.ShapeDtypeStruct((N_TOK, LANES), F32)],
        compiler_params=pltpu.CompilerParams(
            dimension_semantics=("arbitrary",), vmem_limit_bytes=VMEM_LIMIT),
        name="out_proj_ln_route",
    )(a2, r2, x2, wo_b, g1, b1, w_route, b_route)


def _moe_kernel(x1_ref, comb_ref, wg_ref, wu_ref, wd_ref, g_ref, b_ref, o_ref,
                pt_ref, xs_ref, cs_ref, acc_ref, win_ref):
    step = pl.program_id(1)
    assert MOE_ALIGN == 1 << 7 and EXPERTS_PER_GROUP == 1 << 2
    assert EXPERTS_PER_GROUP % MOE_EXPERTS_PER_STEP == 0

    @pl.when(step == 0)
    def _():
        comb = comb_ref[...]
        lane = lax.broadcasted_iota(jnp.int32, comb.shape, 1)
        lane_f = lane.astype(F32)
        gid = jnp.sum(jnp.where(lane == GROUP_LANE, comb, 0.0), axis=-1, keepdims=True)
        mine = lane_f == gid
        ti = lax.broadcasted_iota(jnp.int32, (MOE_TM, MOE_TM), 0)
        tj = lax.broadcasted_iota(jnp.int32, (MOE_TM, MOE_TM), 1)
        tril = jnp.where(ti >= tj, 1.0, 0.0).astype(BF16)
        incl = jnp.dot(tril, jnp.where(mine, 1.0, 0.0).astype(BF16),
                       preferred_element_type=F32)
        rank = jnp.sum(jnp.where(mine, incl, 0.0), axis=-1, keepdims=True) - 1.0
        counts = incl[MOE_TM - 1:MOE_TM, :]
        lane_row = lane[0:1, :]
        start = jnp.int32(0)
        base = jnp.zeros_like(gid)
        for g in range(N_GROUPS):
            cnt = jnp.sum(jnp.where(lane_row == g, counts, 0.0)).astype(jnp.int32)
            chunks = (cnt + (MOE_ALIGN - 1)) >> 7
            win_ref[0, g] = start
            win_ref[1, g] = chunks >> 1
            win_ref[2, g] = chunks & 1
            base = jnp.where(gid == float(g), start.astype(F32), base)
            start = start + chunks * MOE_ALIGN
        pos = base + rank
        r_iota = lax.broadcasted_iota(jnp.int32, (MOE_TM, MOE_ROWS), 1).astype(F32)
        pt = jnp.where(r_iota == pos, 1.0, 0.0).astype(BF16)
        pt_ref[...] = pt
        xs_ref[...] = _tn_dot(pt, x1_ref[...].astype(BF16)).astype(BF16)
        c_hi = comb.astype(BF16)
        c_lo = (comb - c_hi.astype(F32)).astype(BF16)
        cs_ref[...] = _tn_dot(pt, c_hi) + _tn_dot(pt, c_lo)
        acc_ref[...] = jnp.zeros_like(acc_ref)

    e0 = step * MOE_EXPERTS_PER_STEP
    g = e0 >> 2
    row0 = win_ref[0, g]
    n_full = win_ref[1, g]

    def window(start, rows):
        r0 = pl.multiple_of(start, MOE_ALIGN)
        xw = xs_ref[pl.ds(r0, rows), :]
        csw = cs_ref[pl.ds(r0, rows), :]
        lane_w = lax.broadcasted_iota(jnp.int32, (rows, LANES), 1)
        y = None
        for j in range(MOE_EXPERTS_PER_STEP):
            hg = jnp.dot(xw, wg_ref[j], preferred_element_type=F32)
            hu = jnp.dot(xw, wu_ref[j], preferred_element_type=F32)
            h = (hg * _sigmoid(hg) * hu).astype(BF16)
            c = jnp.sum(jnp.where(lane_w == e0 + j, csw, 0.0), axis=-1, keepdims=True)
            yj = c * jnp.dot(h, wd_ref[j], preferred_element_type=F32)
            y = yj if y is None else y + yj
        acc_ref[pl.ds(r0, rows), :] += y

    def full_window(k, _):
        window(row0 + k * MOE_WIN, MOE_WIN)
        return 0

    lax.fori_loop(0, n_full, full_window, 0)

    @pl.when(win_ref[2, g] == 1)
    def _():
        window(row0 + n_full * MOE_WIN, MOE_ALIGN)

    @pl.when(step == N_EXPERTS // MOE_EXPERTS_PER_STEP - 1)
    def _():
        moe = jnp.dot(pt_ref[...], acc_ref[...].astype(BF16), preferred_element_type=F32)
        o_ref[...] = _layer_norm(ALPHA * x1_ref[...] + moe, g_ref[...], b_ref[...])


def _moe(x1, comb, wg_b, wu_b, wd_b, g2, b2):
    return pl.pallas_call(
        _moe_kernel,
        grid=(N_TOK // MOE_TM, N_EXPERTS // MOE_EXPERTS_PER_STEP),
        in_specs=[
            pl.BlockSpec((MOE_TM, D_MODEL), lambda i, e: (i, 0)),
            pl.BlockSpec((MOE_TM, LANES), lambda i, e: (i, 0)),
            pl.BlockSpec((MOE_EXPERTS_PER_STEP, D_MODEL, D_EXPERT), lambda i, e: (e, 0, 0)),
            pl.BlockSpec((MOE_EXPERTS_PER_STEP, D_MODEL, D_EXPERT), lambda i, e: (e, 0, 0)),
            pl.BlockSpec((MOE_EXPERTS_PER_STEP, D_EXPERT, D_MODEL), lambda i, e: (e, 0, 0)),
            pl.BlockSpec((1, D_MODEL), lambda i, e: (0, 0)),
            pl.BlockSpec((1, D_MODEL), lambda i, e: (0, 0)),
        ],
        out_specs=pl.BlockSpec((MOE_TM, D_MODEL), lambda i, e: (i, 0)),
        out_shape=jax.ShapeDtypeStruct((N_TOK, D_MODEL), F32),
        scratch_shapes=[
            pltpu.VMEM((MOE_TM, MOE_ROWS), BF16),
            pltpu.VMEM((MOE_ROWS, D_MODEL), BF16),
            pltpu.VMEM((MOE_ROWS, LANES), F32),
            pltpu.VMEM((MOE_ROWS, D_MODEL), F32),
            pltpu.SMEM((3, N_GROUPS), jnp.int32),
        ],
        compiler_params=pltpu.CompilerParams(
            dimension_semantics=("arbitrary", "arbitrary"), vmem_limit_bytes=MOE_VMEM_LIMIT),
        name="moe_experts_ln",
    )(x1, comb, wg_b, wu_b, wd_b, g2, b2)


def kernel(x, w_in, t5_bias, hgrn_lb_logits, hgrn_norm_w, w_o, ln1_g, ln1_b, w_group, b_group,
           w_expert, b_expert, w_gate, w_up, w_down, ln2_g, ln2_b):
    x2 = x.reshape(N_TOK, D_MODEL)
    w_in_b = w_in[0].astype(BF16)
    w_qt_b = w_in_b[:, 0:WIDTH].T
    w_vt_b = w_in_b[:, 2 * WIDTH:3 * WIDTH].T
    proj, f_logits, qt3, vt3 = _proj(x2, w_in_b, w_qt_b, w_vt_b)
    proj3 = proj.reshape(BATCH, SEQ, ROW_WIDTH)

    a = _attention(t5_bias, qt3, proj3, vt3, _bias_tiles(t5_bias))
    r = _hgrn(proj3, f_logits.reshape(BATCH, SEQ, WIDTH), hgrn_lb_logits, hgrn_norm_w)

    pad = LANES - N_EXPERTS - N_GROUPS
    w_route = jnp.concatenate(
        [w_expert[0].transpose(1, 0, 2).reshape(D_MODEL, N_EXPERTS), w_group[0],
         jnp.zeros((D_MODEL, pad), F32)], axis=1)
    b_route = jnp.concatenate(
        [b_expert[0].reshape(N_EXPERTS), b_group[0], jnp.zeros((pad,), F32)]).reshape(1, LANES)
    x1, comb = _out_proj(
        a.reshape(N_TOK, WIDTH), r.reshape(N_TOK, WIDTH), x2, w_o[0].astype(BF16),
        ln1_g, ln1_b, w_route, b_route)

    out = _moe(x1, comb, w_gate[0].astype(BF16), w_up[0].astype(BF16),
               w_down[0].astype(BF16), ln2_g, ln2_b)
    return out.reshape(BATCH, SEQ, D_MODEL)
```

```python
import functools
import math

import jax
import jax.numpy as jnp
from jax import lax
from jax.experimental import pallas as pl
from jax.experimental.pallas import tpu as pltpu

F32 = jnp.float32
BF16 = jnp.bfloat16

D_MODEL = 1024
BATCH = 2
SEQ = 8192
N_TOK = BATCH * SEQ
HEADS = 8
HEAD_DIM = 64
WIDTH = HEADS * HEAD_DIM
IN_WIDTH = 7 * WIDTH
ROW_WIDTH = 5 * WIDTH
COL_HQ, COL_HF, COL_HI, COL_HG = 1, 2, 3, 4
MOBA_BLOCK = 256
N_BLOCKS = SEQ // MOBA_BLOCK
MOBA_TOPK = 3
HGRN_CHUNK = 64
HGRN_SAFE_LOG2 = 100.0
N_BUCKETS = 32
MAX_DISTANCE = 128
N_GROUPS = 4
EXPERTS_PER_GROUP = 4
N_EXPERTS = 16
D_EXPERT = 512
LN_EPS = 1e-5
RMS_EPS = 1e-6
ALPHA = 2.0 ** 0.25
LOG2E = math.log2(math.e)

LANES = 128
BF16_SUBLANES = 16
NEG = -1e30
VMEM_LIMIT = 48 * 1024 * 1024
MOE_VMEM_LIMIT = 56 * 1024 * 1024

PROJ_TM = 2048
OUT_TM = 1024
MOE_TM = 1024
MOE_ALIGN = 128
MOE_WIN = 256
MOE_EXPERTS_PER_STEP = 2
MOE_ROWS = MOE_TM + N_GROUPS * MOE_ALIGN
GROUP_LANE = N_EXPERTS
HGRN_ROWS = 512
Q_TILES = 2
FAR_BLOCKS = 2
FAR_KEYS = FAR_BLOCKS * MOBA_BLOCK
N_FAR_STEPS = N_BLOCKS // FAR_BLOCKS

BIAS_HI_LANE = N_BLOCKS
BIAS_LO_LANE = N_BLOCKS + 1


def _nt_dot(a, b):
    return lax.dot_general(a, b, (((1,), (1,)), ((), ())), preferred_element_type=F32)


def _tn_dot(a, b):
    return lax.dot_general(a, b, (((0,), (0,)), ((), ())), preferred_element_type=F32)


def _sigmoid(x):
    return 1.0 / (1.0 + jnp.exp2(x * (-LOG2E)))


def _layer_norm(y, g, b):
    mu = jnp.mean(y, axis=-1, keepdims=True)
    d = y - mu
    var = jnp.mean(d * d, axis=-1, keepdims=True)
    return d * lax.rsqrt(var + LN_EPS) * g + b


def _bias_tile_kernel(tab_ref, out_ref):
    h = pl.program_id(0)
    ki = lax.broadcasted_iota(jnp.int32, (MOBA_BLOCK, MOBA_BLOCK), 0)
    qi = lax.broadcasted_iota(jnp.int32, (MOBA_BLOCK, MOBA_BLOCK), 1)
    max_exact = N_BUCKETS // 2
    for w in range(2):
        n = jnp.maximum(qi - ki + MOBA_BLOCK * w, 0)
        nf = jnp.maximum(n, 1).astype(F32)
        scaled = (jnp.log(nf / max_exact) / math.log(MAX_DISTANCE / max_exact)
                  * (N_BUCKETS - max_exact))
        large = max_exact + jnp.floor(jnp.maximum(scaled, 0.0)).astype(jnp.int32)
        large = jnp.minimum(large, N_BUCKETS - 1)
        bucket = jnp.where(n < max_exact, n, large)
        acc = jnp.zeros((MOBA_BLOCK, MOBA_BLOCK), F32)
        for bk in range(N_BUCKETS):
            acc = jnp.where(bucket == bk, tab_ref[bk, h], acc)
        out_ref[0, w] = acc


def _bias_tiles(t5_bias):
    return pl.pallas_call(
        _bias_tile_kernel,
        grid=(HEADS,),
        in_specs=[pl.BlockSpec(memory_space=pltpu.SMEM)],
        out_specs=pl.BlockSpec((1, 2, MOBA_BLOCK, MOBA_BLOCK), lambda h: (h, 0, 0, 0)),
        out_shape=jax.ShapeDtypeStruct((HEADS, 2, MOBA_BLOCK, MOBA_BLOCK), F32),
        name="t5_bias_tiles",
    )(t5_bias)


def _proj_kernel(x_ref, w_ref, wqt_ref, wvt_ref, o_ref, f_ref, qt_ref, vt_ref, xb_ref):
    n = pl.program_id(1)

    def feature_major(wt_ref, t_ref):
        t = _nt_dot(wt_ref[...], xb_ref[...]).astype(BF16)
        for c in range(PROJ_TM // MOBA_BLOCK):
            t_ref[c] = t[:, c * MOBA_BLOCK:(c + 1) * MOBA_BLOCK]

    @pl.when(n == 0)
    def _():
        xb_ref[...] = x_ref[...].astype(BF16)
        feature_major(wqt_ref, qt_ref)

    acc = jnp.dot(xb_ref[...], w_ref[...], preferred_element_type=F32)
    o_ref[...] = acc.astype(BF16)

    @pl.when(n == 1)
    def _():
        feature_major(wvt_ref, vt_ref)

    @pl.when(n == COL_HF)
    def _():
        f_ref[...] = acc


def _proj(x2, w_in_b, w_qt_b, w_vt_b):
    tiles_per_seq = SEQ // PROJ_TM
    blocks_per_tile = PROJ_TM // MOBA_BLOCK
    t_spec = pl.BlockSpec((None, blocks_per_tile, WIDTH, MOBA_BLOCK),
                          lambda i, n: (i // tiles_per_seq, i % tiles_per_seq, 0, 0))
    t_shape = jax.ShapeDtypeStruct((BATCH, N_BLOCKS, WIDTH, MOBA_BLOCK), BF16)
    return pl.pallas_call(
        _proj_kernel,
        grid=(N_TOK // PROJ_TM, ROW_WIDTH // WIDTH),
        in_specs=[
            pl.BlockSpec((PROJ_TM, D_MODEL), lambda i, n: (i, 0)),
            pl.BlockSpec((D_MODEL, WIDTH), lambda i, n: (0, jnp.where(n == 0, 1, n + 2))),
            pl.BlockSpec((WIDTH, D_MODEL), lambda i, n: (0, 0)),
            pl.BlockSpec((WIDTH, D_MODEL), lambda i, n: (0, 0)),
        ],
        out_specs=[
            pl.BlockSpec((PROJ_TM, WIDTH), lambda i, n: (i, n)),
            pl.BlockSpec((PROJ_TM, WIDTH), lambda i, n: (i, 0)),
            t_spec,
            t_spec,
        ],
        out_shape=[
            jax.ShapeDtypeStruct((N_TOK, ROW_WIDTH), BF16),
            jax.ShapeDtypeStruct((N_TOK, WIDTH), F32),
            t_shape,
            t_shape,
        ],
        scratch_shapes=[pltpu.VMEM((PROJ_TM, D_MODEL), BF16)],
        compiler_params=pltpu.CompilerParams(
            dimension_semantics=("arbitrary", "arbitrary"), vmem_limit_bytes=VMEM_LIMIT),
        name="in_proj",
    )(x2, w_in_b, w_qt_b, w_vt_b)


def _colreduce(x, op):
    r = x[0:8]
    for i in range(1, x.shape[0] // 8):
        r = op(r, x[8 * i:8 * i + 8])
    for shift in (4, 2, 1):
        r = op(r, pltpu.roll(r, shift, 0))
    return r


def _attn_kernel(tab_ref, qt_ref, qn_ref, k_ref, vt_ref, bias_ref, o_ref,
                 kmh_ref, kml_ref, ext_ref, s_ref, near_a_ref, near_b_ref, qcat_ref, pmask_ref):
    assert MOBA_BLOCK == 1 << 8
    pair = pl.program_id(1)
    step_i = pl.program_id(2)
    lane = lax.broadcasted_iota(jnp.int32, (MOBA_BLOCK, LANES), 1)

    @pl.when(step_i == 0)
    def _():
        r = lax.broadcasted_iota(jnp.int32, (LANES, SEQ), 1)
        nb = lax.broadcasted_iota(jnp.int32, (LANES, SEQ), 0)
        avg = jnp.where((r >> 8) == nb, 1.0 / MOBA_BLOCK, 0.0).astype(BF16)
        km = jnp.dot(avg, k_ref[...], preferred_element_type=F32)
        hi = km.astype(BF16)
        kmh_ref[...] = hi
        kml_ref[...] = (km - hi.astype(F32)).astype(BF16)

    @pl.when((pl.program_id(0) == 0) & (pair == 0) & (step_i == 0))
    def _():
        kr = lax.broadcasted_iota(jnp.int32, (SEQ, LANES), 0) >> 8
        kl = lax.broadcasted_iota(jnp.int32, (SEQ, LANES), 1)
        ext_ref[...] = jnp.where((kl == kr) | (kl == BIAS_HI_LANE) | (kl == BIAS_LO_LANE),
                                 1.0, 0.0).astype(BF16)

    krow = lax.broadcasted_iota(jnp.int32, (MOBA_BLOCK, MOBA_BLOCK), 0)
    qcol = lax.broadcasted_iota(jnp.int32, (MOBA_BLOCK, MOBA_BLOCK), 1)
    frow = lax.broadcasted_iota(jnp.int32, (LANES, MOBA_BLOCK), 0)
    nrow = lax.broadcasted_iota(jnp.int32, (N_BLOCKS, MOBA_BLOCK), 0)
    nrow_f = nrow.astype(F32)
    brow = lax.broadcasted_iota(jnp.int32, (8, MOBA_BLOCK), 0)
    ones_rows = jnp.ones((BF16_SUBLANES, MOBA_BLOCK), BF16)
    zero_rows = jnp.zeros((LANES - N_BLOCKS - 8, MOBA_BLOCK), F32)
    def v_aug(hh, block):
        vt = vt_ref[block, hh * HEAD_DIM:(hh + 1) * HEAD_DIM, :]
        return jnp.concatenate([vt, ones_rows], axis=0)

    def rows_to_tile(x8):
        return jnp.concatenate([x8] * (N_BLOCKS // 8), axis=0)

    n_tiles = 2 * Q_TILES
    n_q_steps = N_BLOCKS // Q_TILES

    def scores(step, slot):
        start = pl.multiple_of(jnp.minimum(step, N_FAR_STEPS - 1) * FAR_KEYS, FAR_KEYS)
        k_aug = jnp.concatenate([k_ref[pl.ds(start, FAR_KEYS), :],
                                 ext_ref[pl.ds(start, FAR_KEYS), :]], axis=1)
        s_ref[slot] = jnp.dot(k_aug, qcat_ref[...],
                              preferred_element_type=F32)

    def prologue(q_step, q_src, near_dst):
        km_hi = kmh_ref[0:N_BLOCKS, :]
        km_lo = kml_ref[0:N_BLOCKS, :]
        q_far = []
        for t in range(Q_TILES):
            own = q_step * Q_TILES + t
            prev = jnp.maximum(own - 1, 0)
            qt2 = q_src[t]
            k_own = k_ref[pl.ds(pl.multiple_of(own * MOBA_BLOCK, MOBA_BLOCK), MOBA_BLOCK), :]
            k_prev = k_ref[pl.ds(pl.multiple_of(prev * MOBA_BLOCK, MOBA_BLOCK), MOBA_BLOCK), :]
            for hh in range(2):
                head = pair * 2 + hh
                in_head = (frow >= hh * HEAD_DIM) & (frow < (hh + 1) * HEAD_DIM)
                qs = (jnp.where(in_head, qt2, jnp.zeros_like(qt2))
                      * jnp.asarray(HEAD_DIM ** -0.5, BF16))

                gate = (jnp.dot(km_hi, qs, preferred_element_type=F32)
                        + jnp.dot(km_lo, qs, preferred_element_type=F32))
                eligible = nrow < own
                g = jnp.where(eligible, gate, -jnp.inf)
                sel = jnp.zeros((N_BLOCKS, MOBA_BLOCK), jnp.bool_)
                for _ in range(MOBA_TOPK):
                    mx = rows_to_tile(_colreduce(g, jnp.maximum))
                    idx = rows_to_tile(_colreduce(jnp.where(g == mx, nrow_f, float(LANES)),
                                                  jnp.minimum))
                    hit = nrow_f == idx
                    sel = sel | hit
                    g = jnp.where(hit, -jnp.inf, g)
                sel = sel & eligible

                b31 = jnp.full((8, MOBA_BLOCK), tab_ref[N_BUCKETS - 1, head], F32)
                b31_hi = b31.astype(BF16).astype(F32)
                bias_rows = jnp.where(brow == 0, b31_hi, jnp.where(brow == 1, b31 - b31_hi, 0.0))

                aug = jnp.concatenate([jnp.where(sel & (nrow != own - 1), 0.0, NEG), bias_rows,
                                       zero_rows], axis=0)
                q_far.append(jnp.concatenate([qs, aug.astype(BF16)], axis=0))

                prev_sel = _colreduce(
                    jnp.where((nrow == own - 1) & jnp.logical_not(sel), NEG, 0.0), jnp.add)
                pmask_ref[2 * t + hh] = jnp.where(own >= 1, prev_sel, NEG)
                near_dst[2 * t + hh, 0:MOBA_BLOCK, :] = (
                    jnp.dot(k_prev, qs, preferred_element_type=F32) + bias_ref[hh, 1])
                s_own = jnp.dot(k_own, qs, preferred_element_type=F32) + bias_ref[hh, 0]
                near_dst[2 * t + hh, MOBA_BLOCK:, :] = jnp.where(qcol >= krow, s_own, NEG)
        qcat_ref[...] = jnp.concatenate(q_far, axis=1)
        scores(0, 0)

    @pl.when(step_i == 0)
    def _():
        prologue(0, qt_ref, near_a_ref)

    def absorb_scores(s, va, m, acc):
        m_new = jnp.maximum(m, jnp.max(s, axis=0, keepdims=True))
        alpha = jnp.exp(m - m_new)
        p = jnp.exp(s - m_new).astype(BF16)
        return m_new, alpha * acc + jnp.dot(va, p, preferred_element_type=F32)

    def absorb(step, slot, carry):
        block0 = step * FAR_BLOCKS
        out = []
        for j in range(n_tiles):
            hh = j % 2
            s = s_ref[slot, :, j * MOBA_BLOCK:(j + 1) * MOBA_BLOCK]
            va = jnp.concatenate([v_aug(hh, block0 + c) for c in range(FAR_BLOCKS)], axis=1)
            out += absorb_scores(s, va, carry[2 * j], carry[2 * j + 1])
        return tuple(out)

    n_steps = step_i
    carry = []
    for _ in range(n_tiles):
        carry += [jnp.full((1, MOBA_BLOCK), NEG, F32),
                  jnp.zeros((HEAD_DIM + BF16_SUBLANES, MOBA_BLOCK), F32)]

    def two_steps(base, carry):
        scores(base + 1, 1)
        carry = absorb(base, 0, carry)
        scores(base + 2, 0)
        return absorb(base + 1, 1, carry)

    def far_body(i, carry):
        return two_steps(4 * i + 2, two_steps(4 * i, carry))

    n_quads = n_steps // 4
    carry = lax.fori_loop(0, n_quads, far_body, tuple(carry))
    carry = lax.cond(n_steps % 4 >= 2,
                     lambda c: two_steps(4 * n_quads, c), lambda c: c, carry)
    carry = lax.cond(n_steps % 2 == 1,
                     lambda c: absorb(n_steps - 1, 0, c), lambda c: c, carry)

    prev_masks = [pmask_ref[j, 0:1, :] for j in range(n_tiles)]

    def epilogue(near_src):
        for t in range(Q_TILES):
            own = step_i * Q_TILES + t
            outs = []
            for hh in range(2):
                j = 2 * t + hh
                va = jnp.concatenate([v_aug(hh, jnp.maximum(own - 1, 0)), v_aug(hh, own)], axis=1)
                s = jnp.concatenate([near_src[j, 0:MOBA_BLOCK, :] + prev_masks[j],
                                     near_src[j, MOBA_BLOCK:, :]], axis=0)
                _, acc = absorb_scores(s, va, carry[2 * j], carry[2 * j + 1])
                outs.append(acc[0:HEAD_DIM] / acc[HEAD_DIM:HEAD_DIM + 1])
            o_ref[t * MOBA_BLOCK:(t + 1) * MOBA_BLOCK, :] = (
                jnp.concatenate(outs, axis=0).T.astype(BF16))

    next_step = jnp.minimum(step_i + 1, n_q_steps - 1)

    def tail(near_src, near_dst):
        def body():
            epilogue(near_src)
            prologue(next_step, qn_ref, near_dst)
        return body

    lax.cond(step_i % 2 == 0, tail(near_a_ref, near_b_ref), tail(near_b_ref, near_a_ref))


def _attention(t5_bias, qt3, proj3, vt3, bias_tiles):
    n_pairs = HEADS // 2
    n_q_steps = N_BLOCKS // Q_TILES
    assert FAR_BLOCKS == Q_TILES
    return pl.pallas_call(
        _attn_kernel,
        grid=(BATCH, n_pairs, N_BLOCKS // Q_TILES),
        in_specs=[
            pl.BlockSpec(memory_space=pltpu.SMEM),
            pl.BlockSpec((None, Q_TILES, LANES, MOBA_BLOCK), lambda b, p, i: (b, i, p, 0)),
            pl.BlockSpec((None, Q_TILES, LANES, MOBA_BLOCK),
                         lambda b, p, i: (b, jnp.minimum(i + 1, n_q_steps - 1), p, 0)),
            pl.BlockSpec((None, SEQ, LANES), lambda b, p, i: (b, 0, p)),
            pl.BlockSpec((None, N_BLOCKS, LANES, MOBA_BLOCK), lambda b, p, i: (b, 0, p, 0)),
            pl.BlockSpec((2, 2, MOBA_BLOCK, MOBA_BLOCK), lambda b, p, i: (p, 0, 0, 0)),
        ],
        out_specs=pl.BlockSpec((None, Q_TILES * MOBA_BLOCK, LANES), lambda b, p, i: (b, i, p)),
        out_shape=jax.ShapeDtypeStruct((BATCH, SEQ, WIDTH), BF16),
        scratch_shapes=[
            pltpu.VMEM((LANES, LANES), BF16),
            pltpu.VMEM((LANES, LANES), BF16),
            pltpu.VMEM((SEQ, LANES), BF16),
            pltpu.VMEM((2, FAR_KEYS, 2 * Q_TILES * MOBA_BLOCK), F32),
            pltpu.VMEM((2 * Q_TILES, 2 * MOBA_BLOCK, MOBA_BLOCK), F32),
            pltpu.VMEM((2 * Q_TILES, 2 * MOBA_BLOCK, MOBA_BLOCK), F32),
            pltpu.VMEM((2 * LANES, 2 * Q_TILES * MOBA_BLOCK), BF16),
            pltpu.VMEM((2 * Q_TILES, 8, MOBA_BLOCK), F32),
        ],
        compiler_params=pltpu.CompilerParams(
            dimension_semantics=("arbitrary", "arbitrary", "arbitrary"),
            vmem_limit_bytes=VMEM_LIMIT),
        name="moba_attention",
    )(t5_bias, qt3, qt3, proj3, vt3, bias_tiles)


def _hgrn_kernel(q_ref, f_ref, i_ref, g_ref, lbl_ref, nw_ref, o_ref,
                 state_ref, bpad_ref, kpad_ref, vpad_ref, shb_ref, shk_ref, shv_ref,
                 ones_ref, mask_ref):
    C = HGRN_CHUNK
    half = WIDTH // 2
    assert HEAD_DIM == 1 << 6

    @pl.when(pl.program_id(1) == 0)
    def _():
        state_ref[...] = jnp.zeros_like(state_ref)
        sr = lax.broadcasted_iota(jnp.int32, (half, half), 0) >> 6
        sc = lax.broadcasted_iota(jnp.int32, (half, half), 1) >> 6
        same = jnp.where(sr == sc, 1.0, 0.0)
        ones_ref[...] = same.astype(BF16)
        mask_ref[...] = same

    zpad = jnp.zeros((8, WIDTH), F32)
    bpad_ref[0:8, :] = zpad
    kpad_ref[0:8, :] = zpad
    vpad_ref[0:8, :] = zpad

    l0 = lbl_ref[0:1, :]
    l1 = lbl_ref[1:2, :]
    lmx = jnp.maximum(l0, l1)
    e0 = jnp.exp(l0 - lmx)
    lb = e0 / (e0 + jnp.exp(l1 - lmx))
    nw = nw_ref[...]

    ri = lax.broadcasted_iota(jnp.int32, (C, C), 0)
    ci = lax.broadcasted_iota(jnp.int32, (C, C), 1)
    tril = jnp.where(ri >= ci, 1.0, 0.0).astype(BF16)

    def head_sum(t):
        return jnp.concatenate(
            [jnp.dot(t[:, 0:half], ones_ref[...], preferred_element_type=F32),
             jnp.dot(t[:, half:], ones_ref[...], preferred_element_type=F32)], axis=1)

    def prepare(r0):
        qq = q_ref[r0:r0 + C, :].astype(F32)
        qq = qq * _sigmoid(qq)
        f = lb + (1.0 - lb) * _sigmoid(f_ref[r0:r0 + C, :])
        gl = jnp.log(f)
        kk = 1.0 - f
        vv = i_ref[r0:r0 + C, :].astype(F32)
        g_hi = gl.astype(BF16)
        g_r = gl - g_hi.astype(F32)
        g_mid = g_r.astype(BF16)
        g_lo = (g_r - g_mid.astype(F32)).astype(BF16)
        b = (jnp.dot(tril, g_hi, preferred_element_type=F32)
             + jnp.dot(tril, g_mid, preferred_element_type=F32)
             + jnp.dot(tril, g_lo, preferred_element_type=F32))
        return qq, kk, vv, b * LOG2E

    def intra_pairs(qq, kk, vv, b):
        bpad_ref[8:8 + C, :] = b
        kpad_ref[8:8 + C, :] = kk
        vpad_ref[8:8 + C, :] = vv
        for dd in range(8):
            shb_ref[dd] = bpad_ref[8 - dd:8 - dd + C, :]
            shk_ref[dd] = kpad_ref[8 - dd:8 - dd + C, :]
            shv_ref[dd] = vpad_ref[8 - dd:8 - dd + C, :]

        o_parts = []
        for j in range(C // 8):
            rows = C - 8 * j
            bt = b[8 * j:, :]
            qt = qq[8 * j:, :]
            terms = []
            vs_all = []
            for dd in range(8):
                terms.append(qt * shk_ref[dd, 0:rows, :] * jnp.exp2(bt - shb_ref[dd, 0:rows, :]))
                vs_all.append(shv_ref[dd, 0:rows, :])
            a = head_sum(jnp.concatenate(terms, axis=0).astype(BF16))
            av = a * jnp.concatenate(vs_all, axis=0)
            part = av[0:rows]
            for dd in range(1, 8):
                part = part + av[dd * rows:(dd + 1) * rows]
            o_parts.append(part)
        o = o_parts[0]
        for j in range(1, C // 8):
            o = o + jnp.concatenate([jnp.zeros((8 * j, WIDTH), F32), o_parts[j]], axis=0)
        return o

    pt_i = lax.broadcasted_iota(jnp.int32, (C, half), 0)
    ps_i = lax.broadcasted_iota(jnp.int32, (C, half), 1) & (C - 1)
    causal = jnp.where(ps_i <= pt_i, 1.0, 0.0)

    def intra_matmul(qq, kk, vv, b):
        qd = (qq * jnp.exp2(b)).astype(BF16)
        kd = (kk * jnp.exp2(-b)).astype(BF16)
        vb = vv.astype(BF16)
        outs = []
        for a_i in range(2):
            sl = slice(a_i * half, (a_i + 1) * half)
            k4 = jnp.concatenate([kd[:, sl]] * (half // C), axis=0) * ones_ref[...]
            v4 = jnp.concatenate([vb[:, sl]] * (half // C), axis=0) * ones_ref[...]
            pair = _nt_dot(qd[:, sl], k4) * causal
            outs.append(jnp.dot(pair.astype(BF16), v4, preferred_element_type=F32))
        return jnp.concatenate(outs, axis=1)

    def finish(r0, o, qq, kk, vv, b):
        qe = (qq * jnp.exp2(b)).astype(BF16)
        b_last = b[C - 1:C, :]
        kd = kk * jnp.exp2(b_last - b)
        dec = jnp.exp2(b_last)
        inter = []
        for a_i in range(2):
            sl = slice(a_i * half, (a_i + 1) * half)
            st = state_ref[a_i]
            inter.append(_nt_dot(qe[:, sl], st.astype(BF16)))
            upd = _tn_dot(vv[:, sl].astype(BF16), kd[:, sl].astype(BF16))
            state_ref[a_i] = st * dec[:, sl] + upd * mask_ref[...]
        o = o + jnp.concatenate(inter, axis=1)

        ms = head_sum((o * o).astype(BF16)) * (1.0 / HEAD_DIM)
        r = o * lax.rsqrt(ms + RMS_EPS) * nw
        r = r * _sigmoid(g_ref[r0:r0 + C, :].astype(F32))
        o_ref[r0:r0 + C, :] = r.astype(BF16)

    chunks = [prepare(u * C) for u in range(HGRN_ROWS // C)]
    lowest = chunks[0][3][C - 1:C, :]
    for ch in chunks[1:]:
        lowest = jnp.minimum(lowest, ch[3][C - 1:C, :])
    safe = jnp.min(lowest) > -HGRN_SAFE_LOG2

    def run(intra):
        def body():
            for u, (qq, kk, vv, b) in enumerate(chunks):
                finish(u * C, intra(qq, kk, vv, b), qq, kk, vv, b)
        return body

    lax.cond(safe, run(intra_matmul), run(intra_pairs))


def _hgrn(proj3, f3, lb_logits, norm_w):
    blk = lambda col: pl.BlockSpec((None, HGRN_ROWS, WIDTH), lambda b, i, col=col: (b, i, col))
    return pl.pallas_call(
        _hgrn_kernel,
        grid=(BATCH, SEQ // HGRN_ROWS),
        in_specs=[
            blk(COL_HQ),
            pl.BlockSpec((None, HGRN_ROWS, WIDTH), lambda b, i: (b, i, 0)),
            blk(COL_HI),
            blk(COL_HG),
            pl.BlockSpec((2, WIDTH), lambda b, i: (0, 0)),
            pl.BlockSpec((1, WIDTH), lambda b, i: (0, 0)),
        ],
        out_specs=pl.BlockSpec((None, HGRN_ROWS, WIDTH), lambda b, i: (b, i, 0)),
        out_shape=jax.ShapeDtypeStruct((BATCH, SEQ, WIDTH), BF16),
        scratch_shapes=[
            pltpu.VMEM((2, WIDTH // 2, WIDTH // 2), F32),
            pltpu.VMEM((8 + HGRN_CHUNK, WIDTH), F32),
            pltpu.VMEM((8 + HGRN_CHUNK, WIDTH), F32),
            pltpu.VMEM((8 + HGRN_CHUNK, WIDTH), F32),
            pltpu.VMEM((8, HGRN_CHUNK, WIDTH), F32),
            pltpu.VMEM((8, HGRN_CHUNK, WIDTH), F32),
            pltpu.VMEM((8, HGRN_CHUNK, WIDTH), F32),
            pltpu.VMEM((WIDTH // 2, WIDTH // 2), BF16),
            pltpu.VMEM((WIDTH // 2, WIDTH // 2), F32),
        ],
        compiler_params=pltpu.CompilerParams(
            dimension_semantics=("arbitrary", "arbitrary"), vmem_limit_bytes=VMEM_LIMIT),
        name="hgrn2",
    )(proj3, f3, proj3, proj3, lb_logits, norm_w)


def _out_kernel(a_ref, r_ref, x_ref, wo_ref, g_ref, b_ref, wr_ref, br_ref,
                x1_ref, comb_ref):
    mix = jnp.dot(a_ref[...], wo_ref[0:WIDTH, :], preferred_element_type=F32)
    mix = mix + jnp.dot(r_ref[...], wo_ref[WIDTH:, :], preferred_element_type=F32)
    x1 = _layer_norm(ALPHA * x_ref[...] + mix, g_ref[...], b_ref[...])
    x1_ref[...] = x1

    x_hi = x1.astype(BF16)
    x_lo = (x1 - x_hi.astype(F32)).astype(BF16)
    wr = wr_ref[...]
    w_hi = wr.astype(BF16)
    w_lo = (wr - w_hi.astype(F32)).astype(BF16)
    logits = (jnp.dot(x_hi, w_hi, preferred_element_type=F32)
              + jnp.dot(x_lo, w_hi, preferred_element_type=F32)
              + jnp.dot(x_hi, w_lo, preferred_element_type=F32)) + br_ref[...]
    assert EXPERTS_PER_GROUP == 1 << 2
    lane_i = lax.broadcasted_iota(jnp.int32, logits.shape, 1)
    lane = lane_i.astype(F32)
    grp_of_lane = (lane_i >> 2).astype(F32)
    none = float(LANES)
    is_g = (lane_i >= N_EXPERTS) & (lane_i < N_EXPERTS + N_GROUPS)
    gl = jnp.where(is_g, logits, -jnp.inf)
    ge = jnp.exp(gl - jnp.max(gl, axis=-1, keepdims=True))
    gp = ge / jnp.sum(ge, axis=-1, keepdims=True)
    g_w = jnp.max(gp, axis=-1, keepdims=True)
    g_lane = jnp.min(jnp.where(is_g & (gp == g_w), lane, none), axis=-1, keepdims=True)
    g_idx = g_lane - float(N_EXPERTS)

    in_grp = (lane_i < N_EXPERTS) & (grp_of_lane == g_idx)
    el = jnp.where(in_grp, logits, -jnp.inf)
    ee = jnp.exp(el - jnp.max(el, axis=-1, keepdims=True))
    ep = ee / jnp.sum(ee, axis=-1, keepdims=True)
    p1 = jnp.max(ep, axis=-1, keepdims=True)
    i1 = jnp.min(jnp.where(in_grp & (ep == p1), lane, none), axis=-1, keepdims=True)
    rest = in_grp & (lane != i1)
    ep2 = jnp.where(rest, ep, -1.0)
    p2 = jnp.max(ep2, axis=-1, keepdims=True)
    i2 = jnp.min(jnp.where(rest & (ep2 == p2), lane, none), axis=-1, keepdims=True)
    den = p1 + p2
    comb = jnp.where(lane == i1, g_w * (p1 / den), 0.0)
    comb = jnp.where(lane == i2, g_w * (p2 / den), comb)
    comb_ref[...] = jnp.where(lane_i == GROUP_LANE, g_idx, comb)


def _out_proj(a2, r2, x2, wo_b, g1, b1, w_route, b_route):
    row = lambda w: pl.BlockSpec((OUT_TM, w), lambda i: (i, 0))
    full = lambda s: pl.BlockSpec(s, lambda i: (0, 0))
    return pl.pallas_call(
        _out_kernel,
        grid=(N_TOK // OUT_TM,),
        in_specs=[row(WIDTH), row(WIDTH), row(D_MODEL), full((D_MODEL, D_MODEL)),
                  full((1, D_MODEL)), full((1, D_MODEL)), full((D_MODEL, LANES)), full((1, LANES))],
        out_specs=[row(D_MODEL), row(LANES)],
        out_shape=[jax.ShapeDtypeStruct((N_TOK, D_MODEL), F32),
                   jax.ShapeDtypeStruct((N_TOK, LANES), F32)],
        compiler_params=pltpu.CompilerParams(
            dimension_semantics=("arbitrary",), vmem_limit_bytes=VMEM_LIMIT),
        name="out_proj_ln_route",
    )(a2, r2, x2, wo_b, g1, b1, w_route, b_route)


def _moe_kernel(x1_ref, comb_ref, wg_ref, wu_ref, wd_ref, g_ref, b_ref, o_ref,
                pt_ref, xs_ref, cs_ref, acc_ref, win_ref):
    step = pl.program_id(1)
    assert MOE_ALIGN == 1 << 7 and EXPERTS_PER_GROUP == 1 << 2
    assert EXPERTS_PER_GROUP % MOE_EXPERTS_PER_STEP == 0

    @pl.when(step == 0)
    def _():
        comb = comb_ref[...]
        lane = lax.broadcasted_iota(jnp.int32, comb.shape, 1)
        lane_f = lane.astype(F32)
        gid = jnp.sum(jnp.where(lane == GROUP_LANE, comb, 0.0), axis=-1, keepdims=True)
        mine = lane_f == gid
        ti = lax.broadcasted_iota(jnp.int32, (MOE_TM, MOE_TM), 0)
        tj = lax.broadcasted_iota(jnp.int32, (MOE_TM, MOE_TM), 1)
        tril = jnp.where(ti >= tj, 1.0, 0.0).astype(BF16)
        incl = jnp.dot(tril, jnp.where(mine, 1.0, 0.0).astype(BF16),
                       preferred_element_type=F32)
        rank = jnp.sum(jnp.where(mine, incl, 0.0), axis=-1, keepdims=True) - 1.0
        counts = incl[MOE_TM - 1:MOE_TM, :]
        lane_row = lane[0:1, :]
        start = jnp.int32(0)
        base = jnp.zeros_like(gid)
        for g in range(N_GROUPS):
            cnt = jnp.sum(jnp.where(lane_row == g, counts, 0.0)).astype(jnp.int32)
            chunks = (cnt + (MOE_ALIGN - 1)) >> 7
            win_ref[0, g] = start
            win_ref[1, g] = chunks >> 1
            win_ref[2, g] = chunks & 1
            base = jnp.where(gid == float(g), start.astype(F32), base)
            start = start + chunks * MOE_ALIGN
        pos = base + rank
        r_iota = lax.broadcasted_iota(jnp.int32, (MOE_TM, MOE_ROWS), 1).astype(F32)
        pt = jnp.where(r_iota == pos, 1.0, 0.0).astype(BF16)
        pt_ref[...] = pt
        c_hi = comb.astype(BF16)
        c_lo = (comb - c_hi.astype(F32)).astype(BF16)
        srt = _tn_dot(pt, jnp.concatenate([x1_ref[...].astype(BF16), c_hi, c_lo], axis=1))
        xs_ref[...] = srt[:, 0:D_MODEL].astype(BF16)
        cs_ref[...] = srt[:, D_MODEL:D_MODEL + LANES] + srt[:, D_MODEL + LANES:]
        acc_ref[...] = jnp.zeros_like(acc_ref)

    e0 = step * MOE_EXPERTS_PER_STEP
    g = e0 >> 2
    row0 = win_ref[0, g]
    n_full = win_ref[1, g]

    def window(start, rows):
        r0 = pl.multiple_of(start, MOE_ALIGN)
        xw = xs_ref[pl.ds(r0, rows), :]
        csw = cs_ref[pl.ds(r0, rows), :]
        lane_w = lax.broadcasted_iota(jnp.int32, (rows, LANES), 1)
        y = None
        for j in range(MOE_EXPERTS_PER_STEP):
            hg = jnp.dot(xw, wg_ref[j], preferred_element_type=F32)
            hu = jnp.dot(xw, wu_ref[j], preferred_element_type=F32)
            h = (hg * _sigmoid(hg) * hu).astype(BF16)
            c = jnp.sum(jnp.where(lane_w == e0 + j, csw, 0.0), axis=-1, keepdims=True)
            yj = c * jnp.dot(h, wd_ref[j], preferred_element_type=F32)
            y = yj if y is None else y + yj
        acc_ref[pl.ds(r0, rows), :] += y

    def full_window(k, _):
        window(row0 + k * MOE_WIN, MOE_WIN)
        return 0

    lax.fori_loop(0, n_full, full_window, 0)

    @pl.when(win_ref[2, g] == 1)
    def _():
        window(row0 + n_full * MOE_WIN, MOE_ALIGN)

    @pl.when(step == N_EXPERTS // MOE_EXPERTS_PER_STEP - 1)
    def _():
        moe = jnp.dot(pt_ref[...], acc_ref[...].astype(BF16), preferred_element_type=F32)
        o_ref[...] = _layer_norm(ALPHA * x1_ref[...] + moe, g_ref[...], b_ref[...])


def _moe(x1, comb, wg_b, wu_b, wd_b, g2, b2):
    return pl.pallas_call(
        _moe_kernel,
        grid=(N_TOK // MOE_TM, N_EXPERTS // MOE_EXPERTS_PER_STEP),
        in_specs=[
            pl.BlockSpec((MOE_TM, D_MODEL), lambda i, e: (i, 0)),
            pl.BlockSpec((MOE_TM, LANES), lambda i, e: (i, 0)),
            pl.BlockSpec((MOE_EXPERTS_PER_STEP, D_MODEL, D_EXPERT), lambda i, e: (e, 0, 0)),
            pl.BlockSpec((MOE_EXPERTS_PER_STEP, D_MODEL, D_EXPERT), lambda i, e: (e, 0, 0)),
            pl.BlockSpec((MOE_EXPERTS_PER_STEP, D_EXPERT, D_MODEL), lambda i, e: (e, 0, 0)),
            pl.BlockSpec((1, D_MODEL), lambda i, e: (0, 0)),
            pl.BlockSpec((1, D_MODEL), lambda i, e: (0, 0)),
        ],
        out_specs=pl.BlockSpec((MOE_TM, D_MODEL), lambda i, e: (i, 0)),
        out_shape=jax.ShapeDtypeStruct((N_TOK, D_MODEL), F32),
        scratch_shapes=[
            pltpu.VMEM((MOE_TM, MOE_ROWS), BF16),
            pltpu.VMEM((MOE_ROWS, D_MODEL), BF16),
            pltpu.VMEM((MOE_ROWS, LANES), F32),
            pltpu.VMEM((MOE_ROWS, D_MODEL), F32),
            pltpu.SMEM((3, N_GROUPS), jnp.int32),
        ],
        compiler_params=pltpu.CompilerParams(
            dimension_semantics=("arbitrary", "arbitrary"), vmem_limit_bytes=MOE_VMEM_LIMIT),
        name="moe_experts_ln",
    )(x1, comb, wg_b, wu_b, wd_b, g2, b2)


def kernel(x, w_in, t5_bias, hgrn_lb_logits, hgrn_norm_w, w_o, ln1_g, ln1_b, w_group, b_group,
           w_expert, b_expert, w_gate, w_up, w_down, ln2_g, ln2_b):
    x2 = x.reshape(N_TOK, D_MODEL)
    w_in_b = w_in[0].astype(BF16)
    w_qt_b = w_in_b[:, 0:WIDTH].T
    w_vt_b = w_in_b[:, 2 * WIDTH:3 * WIDTH].T
    proj, f_logits, qt3, vt3 = _proj(x2, w_in_b, w_qt_b, w_vt_b)
    proj3 = proj.reshape(BATCH, SEQ, ROW_WIDTH)

    a = _attention(t5_bias, qt3, proj3, vt3, _bias_tiles(t5_bias))
    r = _hgrn(proj3, f_logits.reshape(BATCH, SEQ, WIDTH), hgrn_lb_logits, hgrn_norm_w)

    pad = LANES - N_EXPERTS - N_GROUPS
    w_route = jnp.concatenate(
        [w_expert[0].transpose(1, 0, 2).reshape(D_MODEL, N_EXPERTS), w_group[0],
         jnp.zeros((D_MODEL, pad), F32)], axis=1)
    b_route = jnp.concatenate(
        [b_expert[0].reshape(N_EXPERTS), b_group[0], jnp.zeros((pad,), F32)]).reshape(1, LANES)
    x1, comb = _out_proj(
        a.reshape(N_TOK, WIDTH), r.reshape(N_TOK, WIDTH), x2, w_o[0].astype(BF16),
        ln1_g, ln1_b, w_route, b_route)

    out = _moe(x1, comb, w_gate[0].astype(BF16), w_up[0].astype(BF16),
               w_down[0].astype(BF16), ln2_g, ln2_b)
    return out.reshape(BATCH, SEQ, D_MODEL)
```

```python
import functools
import math

import jax
import jax.numpy as jnp
from jax import lax
from jax.experimental import pallas as pl
from jax.experimental.pallas import tpu as pltpu

F32 = jnp.float32
BF16 = jnp.bfloat16

D_MODEL = 1024
BATCH = 2
SEQ = 8192
N_TOK = BATCH * SEQ
HEADS = 8
HEAD_DIM = 64
WIDTH = HEADS * HEAD_DIM
IN_WIDTH = 7 * WIDTH
ROW_WIDTH = 5 * WIDTH
COL_HQ, COL_HF, COL_HI, COL_HG = 1, 2, 3, 4
MOBA_BLOCK = 256
N_BLOCKS = SEQ // MOBA_BLOCK
MOBA_TOPK = 3
HGRN_CHUNK = 64
HGRN_SAFE_LOG2 = 100.0
N_BUCKETS = 32
MAX_DISTANCE = 128
N_GROUPS = 4
EXPERTS_PER_GROUP = 4
N_EXPERTS = 16
D_EXPERT = 512
LN_EPS = 1e-5
RMS_EPS = 1e-6
ALPHA = 2.0 ** 0.25
LOG2E = math.log2(math.e)

LANES = 128
BF16_SUBLANES = 16
NEG = -1e30
VMEM_LIMIT = 48 * 1024 * 1024
MOE_VMEM_LIMIT = 56 * 1024 * 1024

PROJ_TM = 2048
OUT_TM = 1024
MOE_TM = 1024
MOE_ALIGN = 128
MOE_WIN = 256
MOE_EXPERTS_PER_STEP = 2
MOE_ROWS = MOE_TM + N_GROUPS * MOE_ALIGN
GROUP_LANE = N_EXPERTS
HGRN_ROWS = 512
Q_TILES = 2
FAR_BLOCKS = 2
FAR_KEYS = FAR_BLOCKS * MOBA_BLOCK
N_FAR_STEPS = N_BLOCKS // FAR_BLOCKS

BIAS_HI_LANE = N_BLOCKS
BIAS_LO_LANE = N_BLOCKS + 1


def _nt_dot(a, b):
    return lax.dot_general(a, b, (((1,), (1,)), ((), ())), preferred_element_type=F32)


def _tn_dot(a, b):
    return lax.dot_general(a, b, (((0,), (0,)), ((), ())), preferred_element_type=F32)


def _sigmoid(x):
    return 1.0 / (1.0 + jnp.exp2(x * (-LOG2E)))


def _layer_norm(y, g, b):
    mu = jnp.mean(y, axis=-1, keepdims=True)
    d = y - mu
    var = jnp.mean(d * d, axis=-1, keepdims=True)
    return d * lax.rsqrt(var + LN_EPS) * g + b


def _bias_tile_kernel(tab_ref, out_ref):
    h = pl.program_id(0)
    ki = lax.broadcasted_iota(jnp.int32, (MOBA_BLOCK, MOBA_BLOCK), 0)
    qi = lax.broadcasted_iota(jnp.int32, (MOBA_BLOCK, MOBA_BLOCK), 1)
    max_exact = N_BUCKETS // 2
    for w in range(2):
        n = jnp.maximum(qi - ki + MOBA_BLOCK * w, 0)
        nf = jnp.maximum(n, 1).astype(F32)
        scaled = (jnp.log(nf / max_exact) / math.log(MAX_DISTANCE / max_exact)
                  * (N_BUCKETS - max_exact))
        large = max_exact + jnp.floor(jnp.maximum(scaled, 0.0)).astype(jnp.int32)
        large = jnp.minimum(large, N_BUCKETS - 1)
        bucket = jnp.where(n < max_exact, n, large)
        acc = jnp.zeros((MOBA_BLOCK, MOBA_BLOCK), F32)
        for bk in range(N_BUCKETS):
            acc = jnp.where(bucket == bk, tab_ref[bk, h], acc)
        out_ref[0, w] = acc


def _bias_tiles(t5_bias):
    return pl.pallas_call(
        _bias_tile_kernel,
        grid=(HEADS,),
        in_specs=[pl.BlockSpec(memory_space=pltpu.SMEM)],
        out_specs=pl.BlockSpec((1, 2, MOBA_BLOCK, MOBA_BLOCK), lambda h: (h, 0, 0, 0)),
        out_shape=jax.ShapeDtypeStruct((HEADS, 2, MOBA_BLOCK, MOBA_BLOCK), F32),
        name="t5_bias_tiles",
    )(t5_bias)


def _proj_kernel(x_ref, w_ref, wqt_ref, wvt_ref, o_ref, f_ref, qt_ref, vt_ref, xb_ref):
    n = pl.program_id(1)

    def feature_major(wt_ref, t_ref):
        t = _nt_dot(wt_ref[...], xb_ref[...]).astype(BF16)
        for c in range(PROJ_TM // MOBA_BLOCK):
            t_ref[c] = t[:, c * MOBA_BLOCK:(c + 1) * MOBA_BLOCK]

    @pl.when(n == 0)
    def _():
        xb_ref[...] = x_ref[...].astype(BF16)
        feature_major(wqt_ref, qt_ref)

    acc = jnp.dot(xb_ref[...], w_ref[...], preferred_element_type=F32)
    o_ref[...] = acc.astype(BF16)

    @pl.when(n == 1)
    def _():
        feature_major(wvt_ref, vt_ref)

    @pl.when(n == COL_HF)
    def _():
        f_ref[...] = acc


def _proj(x2, w_in_b, w_qt_b, w_vt_b):
    tiles_per_seq = SEQ // PROJ_TM
    blocks_per_tile = PROJ_TM // MOBA_BLOCK
    t_spec = pl.BlockSpec((None, blocks_per_tile, WIDTH, MOBA_BLOCK),
                          lambda i, n: (i // tiles_per_seq, i % tiles_per_seq, 0, 0))
    t_shape = jax.ShapeDtypeStruct((BATCH, N_BLOCKS, WIDTH, MOBA_BLOCK), BF16)
    return pl.pallas_call(
        _proj_kernel,
        grid=(N_TOK // PROJ_TM, ROW_WIDTH // WIDTH),
        in_specs=[
            pl.BlockSpec((PROJ_TM, D_MODEL), lambda i, n: (i, 0)),
            pl.BlockSpec((D_MODEL, WIDTH), lambda i, n: (0, jnp.where(n == 0, 1, n + 2))),
            pl.BlockSpec((WIDTH, D_MODEL), lambda i, n: (0, 0)),
            pl.BlockSpec((WIDTH, D_MODEL), lambda i, n: (0, 0)),
        ],
        out_specs=[
            pl.BlockSpec((PROJ_TM, WIDTH), lambda i, n: (i, n)),
            pl.BlockSpec((PROJ_TM, WIDTH), lambda i, n: (i, 0)),
            t_spec,
            t_spec,
        ],
        out_shape=[
            jax.ShapeDtypeStruct((N_TOK, ROW_WIDTH), BF16),
            jax.ShapeDtypeStruct((N_TOK, WIDTH), F32),
            t_shape,
            t_shape,
        ],
        scratch_shapes=[pltpu.VMEM((PROJ_TM, D_MODEL), BF16)],
        compiler_params=pltpu.CompilerParams(
            dimension_semantics=("arbitrary", "arbitrary"), vmem_limit_bytes=VMEM_LIMIT),
        name="in_proj",
    )(x2, w_in_b, w_qt_b, w_vt_b)


def _colreduce(x, op):
    r = x[0:8]
    for i in range(1, x.shape[0] // 8):
        r = op(r, x[8 * i:8 * i + 8])
    for shift in (4, 2, 1):
        r = op(r, pltpu.roll(r, shift, 0))
    return r


def _attn_kernel(tab_ref, qt_ref, qn_ref, k_ref, vt_ref, bias_ref, o_ref,
                 kmh_ref, kml_ref, ext_ref, s_ref, near_a_ref, near_b_ref, qcat_ref, pmask_ref):
    assert MOBA_BLOCK == 1 << 8
    pair = pl.program_id(1)
    step_i = pl.program_id(2)
    lane = lax.broadcasted_iota(jnp.int32, (MOBA_BLOCK, LANES), 1)

    @pl.when(step_i == 0)
    def _():
        r = lax.broadcasted_iota(jnp.int32, (LANES, SEQ), 1)
        nb = lax.broadcasted_iota(jnp.int32, (LANES, SEQ), 0)
        avg = jnp.where((r >> 8) == nb, 1.0 / MOBA_BLOCK, 0.0).astype(BF16)
        km = jnp.dot(avg, k_ref[...], preferred_element_type=F32)
        hi = km.astype(BF16)
        kmh_ref[...] = hi
        kml_ref[...] = (km - hi.astype(F32)).astype(BF16)

    @pl.when((pl.program_id(0) == 0) & (pair == 0) & (step_i == 0))
    def _():
        kr = lax.broadcasted_iota(jnp.int32, (SEQ, LANES), 0) >> 8
        kl = lax.broadcasted_iota(jnp.int32, (SEQ, LANES), 1)
        ext_ref[...] = jnp.where((kl == kr) | (kl == BIAS_HI_LANE) | (kl == BIAS_LO_LANE),
                                 1.0, 0.0).astype(BF16)

    krow = lax.broadcasted_iota(jnp.int32, (MOBA_BLOCK, MOBA_BLOCK), 0)
    qcol = lax.broadcasted_iota(jnp.int32, (MOBA_BLOCK, MOBA_BLOCK), 1)
    frow = lax.broadcasted_iota(jnp.int32, (LANES, MOBA_BLOCK), 0)
    nrow = lax.broadcasted_iota(jnp.int32, (N_BLOCKS, MOBA_BLOCK), 0)
    nrow_f = nrow.astype(F32)
    brow = lax.broadcasted_iota(jnp.int32, (8, MOBA_BLOCK), 0)
    ones_rows = jnp.ones((BF16_SUBLANES, MOBA_BLOCK), BF16)
    zero_rows = jnp.zeros((LANES - N_BLOCKS - 8, MOBA_BLOCK), F32)
    def v_aug(hh, block):
        vt = vt_ref[block, hh * HEAD_DIM:(hh + 1) * HEAD_DIM, :]
        return jnp.concatenate([vt, ones_rows], axis=0)

    def rows_to_tile(x8):
        return jnp.concatenate([x8] * (N_BLOCKS // 8), axis=0)

    n_tiles = 2 * Q_TILES
    n_q_steps = N_BLOCKS // Q_TILES

    def scores(step, slot):
        start = pl.multiple_of(jnp.minimum(step, N_FAR_STEPS - 1) * FAR_KEYS, FAR_KEYS)
        k_aug = jnp.concatenate([k_ref[pl.ds(start, FAR_KEYS), :],
                                 ext_ref[pl.ds(start, FAR_KEYS), :]], axis=1)
        s_ref[slot] = jnp.dot(k_aug, qcat_ref[...],
                              preferred_element_type=F32)

    def prologue(q_step, q_src, near_dst):
        km_hi = kmh_ref[0:N_BLOCKS, :]
        km_lo = kml_ref[0:N_BLOCKS, :]
        q_far = []
        for t in range(Q_TILES):
            own = q_step * Q_TILES + t
            prev = jnp.maximum(own - 1, 0)
            qt2 = q_src[t]
            k_own = k_ref[pl.ds(pl.multiple_of(own * MOBA_BLOCK, MOBA_BLOCK), MOBA_BLOCK), :]
            k_prev = k_ref[pl.ds(pl.multiple_of(prev * MOBA_BLOCK, MOBA_BLOCK), MOBA_BLOCK), :]
            for hh in range(2):
                head = pair * 2 + hh
                in_head = (frow >= hh * HEAD_DIM) & (frow < (hh + 1) * HEAD_DIM)
                qs = (jnp.where(in_head, qt2, jnp.zeros_like(qt2))
                      * jnp.asarray(HEAD_DIM ** -0.5, BF16))

                gate = (jnp.dot(km_hi, qs, preferred_element_type=F32)
                        + jnp.dot(km_lo, qs, preferred_element_type=F32))
                eligible = nrow < own
                g = jnp.where(eligible, gate, -jnp.inf)
                sel = jnp.zeros((N_BLOCKS, MOBA_BLOCK), jnp.bool_)
                for _ in range(MOBA_TOPK):
                    mx = rows_to_tile(_colreduce(g, jnp.maximum))
                    idx = rows_to_tile(_colreduce(jnp.where(g == mx, nrow_f, float(LANES)),
                                                  jnp.minimum))
                    hit = nrow_f == idx
                    sel = sel | hit
                    g = jnp.where(hit, -jnp.inf, g)
                sel = sel & eligible

                b31 = jnp.full((8, MOBA_BLOCK), tab_ref[N_BUCKETS - 1, head], F32)
                b31_hi = b31.astype(BF16).astype(F32)
                bias_rows = jnp.where(brow == 0, b31_hi, jnp.where(brow == 1, b31 - b31_hi, 0.0))

                aug = jnp.concatenate([jnp.where(sel & (nrow != own - 1), 0.0, NEG), bias_rows,
                                       zero_rows], axis=0)
                q_far.append(jnp.concatenate([qs, aug.astype(BF16)], axis=0))

                prev_sel = _colreduce(
                    jnp.where((nrow == own - 1) & jnp.logical_not(sel), NEG, 0.0), jnp.add)
                pmask_ref[2 * t + hh] = jnp.where(own >= 1, prev_sel, NEG)
                near_dst[2 * t + hh, 0:MOBA_BLOCK, :] = (
                    jnp.dot(k_prev, qs, preferred_element_type=F32) + bias_ref[hh, 1])
                s_own = jnp.dot(k_own, qs, preferred_element_type=F32) + bias_ref[hh, 0]
                near_dst[2 * t + hh, MOBA_BLOCK:, :] = jnp.where(qcol >= krow, s_own, NEG)
        qcat_ref[...] = jnp.concatenate(q_far, axis=1)
        scores(0, 0)

    @pl.when(step_i == 0)
    def _():
        prologue(0, qt_ref, near_a_ref)

    def absorb_scores(s, va, m, acc):
        m_new = jnp.maximum(m, jnp.max(s, axis=0, keepdims=True))
        alpha = jnp.exp(m - m_new)
        p = jnp.exp(s - m_new).astype(BF16)
        return m_new, alpha * acc + jnp.dot(va, p, preferred_element_type=F32)

    def absorb(step, slot, carry):
        block0 = step * FAR_BLOCKS
        out = []
        for j in range(n_tiles):
            hh = j % 2
            s = s_ref[slot, :, j * MOBA_BLOCK:(j + 1) * MOBA_BLOCK]
            va = jnp.concatenate([v_aug(hh, block0 + c) for c in range(FAR_BLOCKS)], axis=1)
            out += absorb_scores(s, va, carry[2 * j], carry[2 * j + 1])
        return tuple(out)

    n_steps = step_i
    carry = []
    for _ in range(n_tiles):
        carry += [jnp.full((1, MOBA_BLOCK), NEG, F32),
                  jnp.zeros((HEAD_DIM + BF16_SUBLANES, MOBA_BLOCK), F32)]

    def two_steps(base, carry):
        scores(base + 1, 1)
        carry = absorb(base, 0, carry)
        scores(base + 2, 0)
        return absorb(base + 1, 1, carry)

    def far_body(i, carry):
        return two_steps(4 * i + 2, two_steps(4 * i, carry))

    n_quads = n_steps // 4
    carry = lax.fori_loop(0, n_quads, far_body, tuple(carry))
    carry = lax.cond(n_steps % 4 >= 2,
                     lambda c: two_steps(4 * n_quads, c), lambda c: c, carry)
    carry = lax.cond(n_steps % 2 == 1,
                     lambda c: absorb(n_steps - 1, 0, c), lambda c: c, carry)

    prev_masks = [pmask_ref[j, 0:1, :] for j in range(n_tiles)]

    def epilogue(near_src):
        for t in range(Q_TILES):
            own = step_i * Q_TILES + t
            outs = []
            for hh in range(2):
                j = 2 * t + hh
                va = jnp.concatenate([v_aug(hh, jnp.maximum(own - 1, 0)), v_aug(hh, own)], axis=1)
                s = jnp.concatenate([near_src[j, 0:MOBA_BLOCK, :] + prev_masks[j],
                                     near_src[j, MOBA_BLOCK:, :]], axis=0)
                _, acc = absorb_scores(s, va, carry[2 * j], carry[2 * j + 1])
                outs.append(acc[0:HEAD_DIM] / acc[HEAD_DIM:HEAD_DIM + 1])
            o_ref[t * MOBA_BLOCK:(t + 1) * MOBA_BLOCK, :] = (
                jnp.concatenate(outs, axis=0).T.astype(BF16))

    next_step = jnp.minimum(step_i + 1, n_q_steps - 1)

    def tail(near_src, near_dst):
        def body():
            epilogue(near_src)
            prologue(next_step, qn_ref, near_dst)
        return body

    lax.cond(step_i % 2 == 0, tail(near_a_ref, near_b_ref), tail(near_b_ref, near_a_ref))


def _attention(t5_bias, qt3, proj3, vt3, bias_tiles):
    n_pairs = HEADS // 2
    n_q_steps = N_BLOCKS // Q_TILES
    assert FAR_BLOCKS == Q_TILES
    return pl.pallas_call(
        _attn_kernel,
        grid=(BATCH, n_pairs, N_BLOCKS // Q_TILES),
        in_specs=[
            pl.BlockSpec(memory_space=pltpu.SMEM),
            pl.BlockSpec((None, Q_TILES, LANES, MOBA_BLOCK), lambda b, p, i: (b, i, p, 0)),
            pl.BlockSpec((None, Q_TILES, LANES, MOBA_BLOCK),
                         lambda b, p, i: (b, jnp.minimum(i + 1, n_q_steps - 1), p, 0)),
            pl.BlockSpec((None, SEQ, LANES), lambda b, p, i: (b, 0, p)),
            pl.BlockSpec((None, N_BLOCKS, LANES, MOBA_BLOCK), lambda b, p, i: (b, 0, p, 0)),
            pl.BlockSpec((2, 2, MOBA_BLOCK, MOBA_BLOCK), lambda b, p, i: (p, 0, 0, 0)),
        ],
        out_specs=pl.BlockSpec((None, Q_TILES * MOBA_BLOCK, LANES), lambda b, p, i: (b, i, p)),
        out_shape=jax.ShapeDtypeStruct((BATCH, SEQ, WIDTH), BF16),
        scratch_shapes=[
            pltpu.VMEM((LANES, LANES), BF16),
            pltpu.VMEM((LANES, LANES), BF16),
            pltpu.VMEM((SEQ, LANES), BF16),
            pltpu.VMEM((2, FAR_KEYS, 2 * Q_TILES * MOBA_BLOCK), F32),
            pltpu.VMEM((2 * Q_TILES, 2 * MOBA_BLOCK, MOBA_BLOCK), F32),
            pltpu.VMEM((2 * Q_TILES, 2 * MOBA_BLOCK, MOBA_BLOCK), F32),
            pltpu.VMEM((2 * LANES, 2 * Q_TILES * MOBA_BLOCK), BF16),
            pltpu.VMEM((2 * Q_TILES, 8, MOBA_BLOCK), F32),
        ],
        compiler_params=pltpu.CompilerParams(
            dimension_semantics=("arbitrary", "arbitrary", "arbitrary"),
            vmem_limit_bytes=VMEM_LIMIT),
        name="moba_attention",
    )(t5_bias, qt3, qt3, proj3, vt3, bias_tiles)


def _hgrn_kernel(q_ref, f_ref, i_ref, g_ref, lbl_ref, nw_ref, o_ref,
                 state_ref, bpad_ref, kpad_ref, vpad_ref, shb_ref, shk_ref, shv_ref,
                 ones_ref, mask_ref):
    C = HGRN_CHUNK
    half = WIDTH // 2
    assert HEAD_DIM == 1 << 6

    @pl.when(pl.program_id(1) == 0)
    def _():
        state_ref[...] = jnp.zeros_like(state_ref)
        sr = lax.broadcasted_iota(jnp.int32, (half, half), 0) >> 6
        sc = lax.broadcasted_iota(jnp.int32, (half, half), 1) >> 6
        same = jnp.where(sr == sc, 1.0, 0.0)
        ones_ref[...] = same.astype(BF16)
        mask_ref[...] = same

    zpad = jnp.zeros((8, WIDTH), F32)
    bpad_ref[0:8, :] = zpad
    kpad_ref[0:8, :] = zpad
    vpad_ref[0:8, :] = zpad

    l0 = lbl_ref[0:1, :]
    l1 = lbl_ref[1:2, :]
    lmx = jnp.maximum(l0, l1)
    e0 = jnp.exp(l0 - lmx)
    lb = e0 / (e0 + jnp.exp(l1 - lmx))
    nw = nw_ref[...]

    ri = lax.broadcasted_iota(jnp.int32, (C, C), 0)
    ci = lax.broadcasted_iota(jnp.int32, (C, C), 1)
    tril = jnp.where(ri >= ci, 1.0, 0.0).astype(BF16)

    def head_sum(t):
        return jnp.concatenate(
            [jnp.dot(t[:, 0:half], ones_ref[...], preferred_element_type=F32),
             jnp.dot(t[:, half:], ones_ref[...], preferred_element_type=F32)], axis=1)

    def prepare(r0):
        qq = q_ref[r0:r0 + C, :].astype(F32)
        qq = qq * _sigmoid(qq)
        f = lb + (1.0 - lb) * _sigmoid(f_ref[r0:r0 + C, :])
        gl = jnp.log(f)
        kk = 1.0 - f
        vv = i_ref[r0:r0 + C, :].astype(F32)
        g_hi = gl.astype(BF16)
        g_r = gl - g_hi.astype(F32)
        g_mid = g_r.astype(BF16)
        g_lo = (g_r - g_mid.astype(F32)).astype(BF16)
        b = (jnp.dot(tril, g_hi, preferred_element_type=F32)
             + jnp.dot(tril, g_mid, preferred_element_type=F32)
             + jnp.dot(tril, g_lo, preferred_element_type=F32))
        return qq, kk, vv, b * LOG2E

    def intra_pairs(qq, kk, vv, b):
        bpad_ref[8:8 + C, :] = b
        kpad_ref[8:8 + C, :] = kk
        vpad_ref[8:8 + C, :] = vv
        for dd in range(8):
            shb_ref[dd] = bpad_ref[8 - dd:8 - dd + C, :]
            shk_ref[dd] = kpad_ref[8 - dd:8 - dd + C, :]
            shv_ref[dd] = vpad_ref[8 - dd:8 - dd + C, :]

        o_parts = []
        for j in range(C // 8):
            rows = C - 8 * j
            bt = b[8 * j:, :]
            qt = qq[8 * j:, :]
            terms = []
            vs_all = []
            for dd in range(8):
                terms.append(qt * shk_ref[dd, 0:rows, :] * jnp.exp2(bt - shb_ref[dd, 0:rows, :]))
                vs_all.append(shv_ref[dd, 0:rows, :])
            a = head_sum(jnp.concatenate(terms, axis=0).astype(BF16))
            av = a * jnp.concatenate(vs_all, axis=0)
            part = av[0:rows]
            for dd in range(1, 8):
                part = part + av[dd * rows:(dd + 1) * rows]
            o_parts.append(part)
        o = o_parts[0]
        for j in range(1, C // 8):
            o = o + jnp.concatenate([jnp.zeros((8 * j, WIDTH), F32), o_parts[j]], axis=0)
        return o

    pt_i = lax.broadcasted_iota(jnp.int32, (C, half), 0)
    ps_i = lax.broadcasted_iota(jnp.int32, (C, half), 1) & (C - 1)
    causal = jnp.where(ps_i <= pt_i, 1.0, 0.0)

    def intra_matmul(qq, kk, vv, b):
        qd = (qq * jnp.exp2(b)).astype(BF16)
        kd = (kk * jnp.exp2(-b)).astype(BF16)
        vb = vv.astype(BF16)
        outs = []
        for a_i in range(2):
            sl = slice(a_i * half, (a_i + 1) * half)
            k4 = jnp.concatenate([kd[:, sl]] * (half // C), axis=0) * ones_ref[...]
            v4 = jnp.concatenate([vb[:, sl]] * (half // C), axis=0) * ones_ref[...]
            pair = _nt_dot(qd[:, sl], k4) * causal
            outs.append(jnp.dot(pair.astype(BF16), v4, preferred_element_type=F32))
        return jnp.concatenate(outs, axis=1)

    def finish(r0, o, qq, kk, vv, b):
        qe = (qq * jnp.exp2(b)).astype(BF16)
        b_last = b[C - 1:C, :]
        kd = kk * jnp.exp2(b_last - b)
        dec = jnp.exp2(b_last)
        inter = []
        for a_i in range(2):
            sl = slice(a_i * half, (a_i + 1) * half)
            st = state_ref[a_i]
            inter.append(_nt_dot(qe[:, sl], st.astype(BF16)))
            upd = _tn_dot(vv[:, sl].astype(BF16), kd[:, sl].astype(BF16))
            state_ref[a_i] = st * dec[:, sl] + upd * mask_ref[...]
        o = o + jnp.concatenate(inter, axis=1)

        ms = head_sum((o * o).astype(BF16)) * (1.0 / HEAD_DIM)
        r = o * lax.rsqrt(ms + RMS_EPS) * nw
        r = r * _sigmoid(g_ref[r0:r0 + C, :].astype(F32))
        o_ref[r0:r0 + C, :] = r.astype(BF16)

    chunks = [prepare(u * C) for u in range(HGRN_ROWS // C)]
    lowest = chunks[0][3][C - 1:C, :]
    for ch in chunks[1:]:
        lowest = jnp.minimum(lowest, ch[3][C - 1:C, :])
    safe = jnp.min(lowest) > -HGRN_SAFE_LOG2

    def run(intra):
        def body():
            for u, (qq, kk, vv, b) in enumerate(chunks):
                finish(u * C, intra(qq, kk, vv, b), qq, kk, vv, b)
        return body

    lax.cond(safe, run(intra_matmul), run(intra_pairs))


def _hgrn(proj3, f3, lb_logits, norm_w):
    blk = lambda col: pl.BlockSpec((None, HGRN_ROWS, WIDTH), lambda b, i, col=col: (b, i, col))
    return pl.pallas_call(
        _hgrn_kernel,
        grid=(BATCH, SEQ // HGRN_ROWS),
        in_specs=[
            blk(COL_HQ),
            pl.BlockSpec((None, HGRN_ROWS, WIDTH), lambda b, i: (b, i, 0)),
            blk(COL_HI),
            blk(COL_HG),
            pl.BlockSpec((2, WIDTH), lambda b, i: (0, 0)),
            pl.BlockSpec((1, WIDTH), lambda b, i: (0, 0)),
        ],
        out_specs=pl.BlockSpec((None, HGRN_ROWS, WIDTH), lambda b, i: (b, i, 0)),
        out_shape=jax.ShapeDtypeStruct((BATCH, SEQ, WIDTH), BF16),
        scratch_shapes=[
            pltpu.VMEM((2, WIDTH // 2, WIDTH // 2), F32),
            pltpu.VMEM((8 + HGRN_CHUNK, WIDTH), F32),
            pltpu.VMEM((8 + HGRN_CHUNK, WIDTH), F32),
            pltpu.VMEM((8 + HGRN_CHUNK, WIDTH), F32),
            pltpu.VMEM((8, HGRN_CHUNK, WIDTH), F32),
            pltpu.VMEM((8, HGRN_CHUNK, WIDTH), F32),
            pltpu.VMEM((8, HGRN_CHUNK, WIDTH), F32),
            pltpu.VMEM((WIDTH // 2, WIDTH // 2), BF16),
            pltpu.VMEM((WIDTH // 2, WIDTH // 2), F32),
        ],
        compiler_params=pltpu.CompilerParams(
            dimension_semantics=("arbitrary", "arbitrary"), vmem_limit_bytes=VMEM_LIMIT),
        name="hgrn2",
    )(proj3, f3, proj3, proj3, lb_logits, norm_w)


def _out_kernel(a_ref, r_ref, x_ref, wo_ref, g_ref, b_ref, wr_ref, br_ref,
                x1_ref, comb_ref):
    mix = jnp.dot(a_ref[...], wo_ref[0:WIDTH, :], preferred_element_type=F32)
    mix = mix + jnp.dot(r_ref[...], wo_ref[WIDTH:, :], preferred_element_type=F32)
    x1 = _layer_norm(ALPHA * x_ref[...] + mix, g_ref[...], b_ref[...])
    x1_ref[...] = x1

    x_hi = x1.astype(BF16)
    x_lo = (x1 - x_hi.astype(F32)).astype(BF16)
    wr = wr_ref[...]
    w_hi = wr.astype(BF16)
    w_lo = (wr - w_hi.astype(F32)).astype(BF16)
    logits = (jnp.dot(x_hi, w_hi, preferred_element_type=F32)
              + jnp.dot(x_lo, w_hi, preferred_element_type=F32)
              + jnp.dot(x_hi, w_lo, preferred_element_type=F32)) + br_ref[...]
    assert EXPERTS_PER_GROUP == 1 << 2
    lane_i = lax.broadcasted_iota(jnp.int32, logits.shape, 1)
    lane = lane_i.astype(F32)
    grp_of_lane = (lane_i >> 2).astype(F32)
    none = float(LANES)
    is_g = (lane_i >= N_EXPERTS) & (lane_i < N_EXPERTS + N_GROUPS)
    gl = jnp.where(is_g, logits, -jnp.inf)
    ge = jnp.exp(gl - jnp.max(gl, axis=-1, keepdims=True))
    gp = ge / jnp.sum(ge, axis=-1, keepdims=True)
    g_w = jnp.max(gp, axis=-1, keepdims=True)
    g_lane = jnp.min(jnp.where(is_g & (gp == g_w), lane, none), axis=-1, keepdims=True)
    g_idx = g_lane - float(N_EXPERTS)

    in_grp = (lane_i < N_EXPERTS) & (grp_of_lane == g_idx)
    el = jnp.where(in_grp, logits, -jnp.inf)
    ee = jnp.exp(el - jnp.max(el, axis=-1, keepdims=True))
    ep = ee / jnp.sum(ee, axis=-1, keepdims=True)
    p1 = jnp.max(ep, axis=-1, keepdims=True)
    i1 = jnp.min(jnp.where(in_grp & (ep == p1), lane, none), axis=-1, keepdims=True)
    rest = in_grp & (lane != i1)
    ep2 = jnp.where(rest, ep, -1.0)
    p2 = jnp.max(ep2, axis=-1, keepdims=True)
    i2 = jnp.min(jnp.where(rest & (ep2 == p2), lane, none), axis=-1, keepdims=True)
    den = p1 + p2
    comb = jnp.where(lane == i1, g_w * (p1 / den), 0.0)
    comb = jnp.where(lane == i2, g_w * (p2 / den), comb)
    comb_ref[...] = jnp.where(lane_i == GROUP_LANE, g_idx, comb)


def _out_proj(a2, r2, x2, wo_b, g1, b1, w_route, b_route):
    row = lambda w: pl.BlockSpec((OUT_TM, w), lambda i: (i, 0))
    full = lambda s: pl.BlockSpec(s, lambda i: (0, 0))
    return pl.pallas_call(
        _out_kernel,
        grid=(N_TOK // OUT_TM,),
        in_specs=[row(WIDTH), row(WIDTH), row(D_MODEL), full((D_MODEL, D_MODEL)),
                  full((1, D_MODEL)), full((1, D_MODEL)), full((D_MODEL, LANES)), full((1, LANES))],
        out_specs=[row(D_MODEL), row(LANES)],
        out_shape=[jax.ShapeDtypeStruct((N_TOK, D_MODEL), F32),
                   jax.ShapeDtypeStruct((N_TOK, LANES), F32)],
        compiler_params=pltpu.CompilerParams(
            dimension_semantics=("arbitrary",), vmem_limit_bytes=VMEM_LIMIT),
        name="out_proj_ln_route",
    )(a2, r2, x2, wo_b, g1, b1, w_route, b_route)


def _moe_kernel(x1_ref, comb_ref, wg_ref, wu_ref, wd_ref, g_ref, b_ref, o_ref,
                pt_ref, xs_ref, cs_ref, acc_ref, win_ref):
    step = pl.program_id(1)
    assert MOE_ALIGN == 1 << 7 and EXPERTS_PER_GROUP == 1 << 2
    assert EXPERTS_PER_GROUP % MOE_EXPERTS_PER_STEP == 0

    @pl.when(step == 0)
    def _():
        comb = comb_ref[...]
        lane = lax.broadcasted_iota(jnp.int32, comb.shape, 1)
        lane_f = lane.astype(F32)
        gid = jnp.sum(jnp.where(lane == GROUP_LANE, comb, 0.0), axis=-1, keepdims=True)
        mine = lane_f == gid
        ti = lax.broadcasted_iota(jnp.int32, (MOE_ALIGN, MOE_ALIGN), 0)
        tj = lax.broadcasted_iota(jnp.int32, (MOE_ALIGN, MOE_ALIGN), 1)
        tril = jnp.where(ti >= tj, 1.0, 0.0).astype(BF16)
        member = jnp.where(mine, 1.0, 0.0).astype(BF16)
        before = jnp.zeros((1, LANES), F32)
        pieces = []
        for blk in range(MOE_TM // MOE_ALIGN):
            inc = jnp.dot(tril, member[blk * MOE_ALIGN:(blk + 1) * MOE_ALIGN, :],
                          preferred_element_type=F32)
            pieces.append(inc + before)
            before = before + inc[MOE_ALIGN - 1:MOE_ALIGN, :]
        incl = jnp.concatenate(pieces, axis=0)
        rank = jnp.sum(jnp.where(mine, incl, 0.0), axis=-1, keepdims=True) - 1.0
        counts = incl[MOE_TM - 1:MOE_TM, :]
        lane_row = lane[0:1, :]
        start = jnp.int32(0)
        base = jnp.zeros_like(gid)
        for g in range(N_GROUPS):
            cnt = jnp.sum(jnp.where(lane_row == g, counts, 0.0)).astype(jnp.int32)
            chunks = (cnt + (MOE_ALIGN - 1)) >> 7
            win_ref[0, g] = start
            win_ref[1, g] = chunks >> 1
            win_ref[2, g] = chunks & 1
            base = jnp.where(gid == float(g), start.astype(F32), base)
            start = start + chunks * MOE_ALIGN
        pos = base + rank
        r_iota = lax.broadcasted_iota(jnp.int32, (MOE_TM, MOE_ROWS), 1).astype(F32)
        pt = jnp.where(r_iota == pos, 1.0, 0.0).astype(BF16)
        pt_ref[...] = pt
        c_hi = comb.astype(BF16)
        c_lo = (comb - c_hi.astype(F32)).astype(BF16)
        srt = _tn_dot(pt, jnp.concatenate([x1_ref[...].astype(BF16), c_hi, c_lo], axis=1))
        xs_ref[...] = srt[:, 0:D_MODEL].astype(BF16)
        cs_ref[...] = srt[:, D_MODEL:D_MODEL + LANES] + srt[:, D_MODEL + LANES:]
        acc_ref[...] = jnp.zeros_like(acc_ref)

    e0 = step * MOE_EXPERTS_PER_STEP
    g = e0 >> 2
    row0 = win_ref[0, g]
    n_full = win_ref[1, g]

    def window(start, rows):
        r0 = pl.multiple_of(start, MOE_ALIGN)
        xw = xs_ref[pl.ds(r0, rows), :]
        csw = cs_ref[pl.ds(r0, rows), :]
        lane_w = lax.broadcasted_iota(jnp.int32, (rows, LANES), 1)
        y = None
        for j in range(MOE_EXPERTS_PER_STEP):
            hg = jnp.dot(xw, wg_ref[j], preferred_element_type=F32)
            hu = jnp.dot(xw, wu_ref[j], preferred_element_type=F32)
            h = (hg * _sigmoid(hg) * hu).astype(BF16)
            c = jnp.sum(jnp.where(lane_w == e0 + j, csw, 0.0), axis=-1, keepdims=True)
            yj = c * jnp.dot(h, wd_ref[j], preferred_element_type=F32)
            y = yj if y is None else y + yj
        acc_ref[pl.ds(r0, rows), :] += y

    def full_window(k, _):
        window(row0 + k * MOE_WIN, MOE_WIN)
        return 0

    lax.fori_loop(0, n_full, full_window, 0)

    @pl.when(win_ref[2, g] == 1)
    def _():
        window(row0 + n_full * MOE_WIN, MOE_ALIGN)

    @pl.when(step == N_EXPERTS // MOE_EXPERTS_PER_STEP - 1)
    def _():
        moe = jnp.dot(pt_ref[...], acc_ref[...].astype(BF16), preferred_element_type=F32)
        o_ref[...] = _layer_norm(ALPHA * x1_ref[...] + moe, g_ref[...], b_ref[...])


def _moe(x1, comb, wg_b, wu_b, wd_b, g2, b2):
    return pl.pallas_call(
        _moe_kernel,
        grid=(N_TOK // MOE_TM, N_EXPERTS // MOE_EXPERTS_PER_STEP),
        in_specs=[
            pl.BlockSpec((MOE_TM, D_MODEL), lambda i, e: (i, 0)),
            pl.BlockSpec((MOE_TM, LANES), lambda i, e: (i, 0)),
            pl.BlockSpec((MOE_EXPERTS_PER_STEP, D_MODEL, D_EXPERT), lambda i, e: (e, 0, 0)),
            pl.BlockSpec((MOE_EXPERTS_PER_STEP, D_MODEL, D_EXPERT), lambda i, e: (e, 0, 0)),
            pl.BlockSpec((MOE_EXPERTS_PER_STEP, D_EXPERT, D_MODEL), lambda i, e: (e, 0, 0)),
            pl.BlockSpec((1, D_MODEL), lambda i, e: (0, 0)),
            pl.BlockSpec((1, D_MODEL), lambda i, e: (0, 0)),
        ],
        out_specs=pl.BlockSpec((MOE_TM, D_MODEL), lambda i, e: (i, 0)),
        out_shape=jax.ShapeDtypeStruct((N_TOK, D_MODEL), F32),
        scratch_shapes=[
            pltpu.VMEM((MOE_TM, MOE_ROWS), BF16),
            pltpu.VMEM((MOE_ROWS, D_MODEL), BF16),
            pltpu.VMEM((MOE_ROWS, LANES), F32),
            pltpu.VMEM((MOE_ROWS, D_MODEL), F32),
            pltpu.SMEM((3, N_GROUPS), jnp.int32),
        ],
        compiler_params=pltpu.CompilerParams(
            dimension_semantics=("arbitrary", "arbitrary"), vmem_limit_bytes=MOE_VMEM_LIMIT),
        name="moe_experts_ln",
    )(x1, comb, wg_b, wu_b, wd_b, g2, b2)


def kernel(x, w_in, t5_bias, hgrn_lb_logits, hgrn_norm_w, w_o, ln1_g, ln1_b, w_group, b_group,
           w_expert, b_expert, w_gate, w_up, w_down, ln2_g, ln2_b):
    x2 = x.reshape(N_TOK, D_MODEL)
    w_in_b = w_in[0].astype(BF16)
    w_qt_b = w_in_b[:, 0:WIDTH].T
    w_vt_b = w_in_b[:, 2 * WIDTH:3 * WIDTH].T
    proj, f_logits, qt3, vt3 = _proj(x2, w_in_b, w_qt_b, w_vt_b)
    proj3 = proj.reshape(BATCH, SEQ, ROW_WIDTH)

    a = _attention(t5_bias, qt3, proj3, vt3, _bias_tiles(t5_bias))
    r = _hgrn(proj3, f_logits.reshape(BATCH, SEQ, WIDTH), hgrn_lb_logits, hgrn_norm_w)

    pad = LANES - N_EXPERTS - N_GROUPS
    w_route = jnp.concatenate(
        [w_expert[0].transpose(1, 0, 2).reshape(D_MODEL, N_EXPERTS), w_group[0],
         jnp.zeros((D_MODEL, pad), F32)], axis=1)
    b_route = jnp.concatenate(
        [b_expert[0].reshape(N_EXPERTS), b_group[0], jnp.zeros((pad,), F32)]).reshape(1, LANES)
    x1, comb = _out_proj(
        a.reshape(N_TOK, WIDTH), r.reshape(N_TOK, WIDTH), x2, w_o[0].astype(BF16),
        ln1_g, ln1_b, w_route, b_route)

    out = _moe(x1, comb, w_gate[0].astype(BF16), w_up[0].astype(BF16),
               w_down[0].astype(BF16), ln2_g, ln2_b)
    return out.reshape(BATCH, SEQ, D_MODEL)
```

```python
import functools
import math

import jax
import jax.numpy as jnp
from jax import lax
from jax.experimental import pallas as pl
from jax.experimental.pallas import tpu as pltpu

F32 = jnp.float32
BF16 = jnp.bfloat16

D_MODEL = 1024
BATCH = 2
SEQ = 8192
N_TOK = BATCH * SEQ
HEADS = 8
HEAD_DIM = 64
WIDTH = HEADS * HEAD_DIM
IN_WIDTH = 7 * WIDTH
ROW_WIDTH = 5 * WIDTH
COL_HQ, COL_HF, COL_HI, COL_HG = 1, 2, 3, 4
MOBA_BLOCK = 256
N_BLOCKS = SEQ // MOBA_BLOCK
MOBA_TOPK = 3
HGRN_CHUNK = 64
HGRN_SAFE_LOG2 = 100.0
N_BUCKETS = 32
MAX_DISTANCE = 128
N_GROUPS = 4
EXPERTS_PER_GROUP = 4
N_EXPERTS = 16
D_EXPERT = 512
LN_EPS = 1e-5
RMS_EPS = 1e-6
ALPHA = 2.0 ** 0.25
LOG2E = math.log2(math.e)

LANES = 128
BF16_SUBLANES = 16
NEG = -1e30
VMEM_LIMIT = 48 * 1024 * 1024
MOE_VMEM_LIMIT = 56 * 1024 * 1024

PROJ_TM = 2048
OUT_TM = 1024
MOE_TM = 1024
MOE_ALIGN = 128
MOE_WIN = 256
MOE_EXPERTS_PER_STEP = 2
MOE_ROWS = MOE_TM + N_GROUPS * MOE_ALIGN
GROUP_LANE = N_EXPERTS
HGRN_ROWS = 512
Q_TILES = 2
FAR_BLOCKS = 2
FAR_KEYS = FAR_BLOCKS * MOBA_BLOCK
N_FAR_STEPS = N_BLOCKS // FAR_BLOCKS

BIAS_HI_LANE = N_BLOCKS
BIAS_LO_LANE = N_BLOCKS + 1


def _nt_dot(a, b):
    return lax.dot_general(a, b, (((1,), (1,)), ((), ())), preferred_element_type=F32)


def _tn_dot(a, b):
    return lax.dot_general(a, b, (((0,), (0,)), ((), ())), preferred_element_type=F32)


def _sigmoid(x):
    return 1.0 / (1.0 + jnp.exp2(x * (-LOG2E)))


def _layer_norm(y, g, b):
    mu = jnp.mean(y, axis=-1, keepdims=True)
    d = y - mu
    var = jnp.mean(d * d, axis=-1, keepdims=True)
    return d * lax.rsqrt(var + LN_EPS) * g + b


def _bias_tile_kernel(tab_ref, out_ref):
    h = pl.program_id(0)
    ki = lax.broadcasted_iota(jnp.int32, (MOBA_BLOCK, MOBA_BLOCK), 0)
    qi = lax.broadcasted_iota(jnp.int32, (MOBA_BLOCK, MOBA_BLOCK), 1)
    max_exact = N_BUCKETS // 2
    for w in range(2):
        n = jnp.maximum(qi - ki + MOBA_BLOCK * w, 0)
        nf = jnp.maximum(n, 1).astype(F32)
        scaled = (jnp.log(nf / max_exact) / math.log(MAX_DISTANCE / max_exact)
                  * (N_BUCKETS - max_exact))
        large = max_exact + jnp.floor(jnp.maximum(scaled, 0.0)).astype(jnp.int32)
        large = jnp.minimum(large, N_BUCKETS - 1)
        bucket = jnp.where(n < max_exact, n, large)
        acc = jnp.zeros((MOBA_BLOCK, MOBA_BLOCK), F32)
        for bk in range(N_BUCKETS):
            acc = jnp.where(bucket == bk, tab_ref[bk, h], acc)
        out_ref[0, w] = acc


def _bias_tiles(t5_bias):
    return pl.pallas_call(
        _bias_tile_kernel,
        grid=(HEADS,),
        in_specs=[pl.BlockSpec(memory_space=pltpu.SMEM)],
        out_specs=pl.BlockSpec((1, 2, MOBA_BLOCK, MOBA_BLOCK), lambda h: (h, 0, 0, 0)),
        out_shape=jax.ShapeDtypeStruct((HEADS, 2, MOBA_BLOCK, MOBA_BLOCK), F32),
        name="t5_bias_tiles",
    )(t5_bias)


def _proj_kernel(x_ref, w_ref, wqt_ref, wvt_ref, o_ref, f_ref, qt_ref, vt_ref, xb_ref):
    n = pl.program_id(1)

    def feature_major(wt_ref, t_ref):
        t = _nt_dot(wt_ref[...], xb_ref[...]).astype(BF16)
        for c in range(PROJ_TM // MOBA_BLOCK):
            t_ref[c] = t[:, c * MOBA_BLOCK:(c + 1) * MOBA_BLOCK]

    @pl.when(n == 0)
    def _():
        xb_ref[...] = x_ref[...].astype(BF16)
        feature_major(wqt_ref, qt_ref)

    acc = jnp.dot(xb_ref[...], w_ref[...], preferred_element_type=F32)
    o_ref[...] = acc.astype(BF16)

    @pl.when(n == 1)
    def _():
        feature_major(wvt_ref, vt_ref)

    @pl.when(n == COL_HF)
    def _():
        f_ref[...] = acc


def _proj(x2, w_in_b, w_qt_b, w_vt_b):
    tiles_per_seq = SEQ // PROJ_TM
    blocks_per_tile = PROJ_TM // MOBA_BLOCK
    t_spec = pl.BlockSpec((None, blocks_per_tile, WIDTH, MOBA_BLOCK),
                          lambda i, n: (i // tiles_per_seq, i % tiles_per_seq, 0, 0))
    t_shape = jax.ShapeDtypeStruct((BATCH, N_BLOCKS, WIDTH, MOBA_BLOCK), BF16)
    return pl.pallas_call(
        _proj_kernel,
        grid=(N_TOK // PROJ_TM, ROW_WIDTH // WIDTH),
        in_specs=[
            pl.BlockSpec((PROJ_TM, D_MODEL), lambda i, n: (i, 0)),
            pl.BlockSpec((D_MODEL, WIDTH), lambda i, n: (0, jnp.where(n == 0, 1, n + 2))),
            pl.BlockSpec((WIDTH, D_MODEL), lambda i, n: (0, 0)),
            pl.BlockSpec((WIDTH, D_MODEL), lambda i, n: (0, 0)),
        ],
        out_specs=[
            pl.BlockSpec((PROJ_TM, WIDTH), lambda i, n: (i, n)),
            pl.BlockSpec((PROJ_TM, WIDTH), lambda i, n: (i, 0)),
            t_spec,
            t_spec,
        ],
        out_shape=[
            jax.ShapeDtypeStruct((N_TOK, ROW_WIDTH), BF16),
            jax.ShapeDtypeStruct((N_TOK, WIDTH), F32),
            t_shape,
            t_shape,
        ],
        scratch_shapes=[pltpu.VMEM((PROJ_TM, D_MODEL), BF16)],
        compiler_params=pltpu.CompilerParams(
            dimension_semantics=("arbitrary", "arbitrary"), vmem_limit_bytes=VMEM_LIMIT),
        name="in_proj",
    )(x2, w_in_b, w_qt_b, w_vt_b)


def _colreduce(x, op):
    r = x[0:8]
    for i in range(1, x.shape[0] // 8):
        r = op(r, x[8 * i:8 * i + 8])
    for shift in (4, 2, 1):
        r = op(r, pltpu.roll(r, shift, 0))
    return r


def _attn_kernel(tab_ref, qt_ref, qn_ref, k_ref, vt_ref, bias_ref, o_ref,
                 kmh_ref, kml_ref, ext_ref, s_ref, near_a_ref, near_b_ref, qcat_ref, pmask_ref):
    assert MOBA_BLOCK == 1 << 8
    pair = pl.program_id(1)
    step_i = pl.program_id(2)
    lane = lax.broadcasted_iota(jnp.int32, (MOBA_BLOCK, LANES), 1)

    @pl.when(step_i == 0)
    def _():
        r = lax.broadcasted_iota(jnp.int32, (LANES, SEQ), 1)
        nb = lax.broadcasted_iota(jnp.int32, (LANES, SEQ), 0)
        avg = jnp.where((r >> 8) == nb, 1.0 / MOBA_BLOCK, 0.0).astype(BF16)
        km = jnp.dot(avg, k_ref[...], preferred_element_type=F32)
        hi = km.astype(BF16)
        kmh_ref[...] = hi
        kml_ref[...] = (km - hi.astype(F32)).astype(BF16)

    @pl.when((pl.program_id(0) == 0) & (pair == 0) & (step_i == 0))
    def _():
        kr = lax.broadcasted_iota(jnp.int32, (SEQ, LANES), 0) >> 8
        kl = lax.broadcasted_iota(jnp.int32, (SEQ, LANES), 1)
        ext_ref[...] = jnp.where((kl == kr) | (kl == BIAS_HI_LANE) | (kl == BIAS_LO_LANE),
                                 1.0, 0.0).astype(BF16)

    krow = lax.broadcasted_iota(jnp.int32, (MOBA_BLOCK, MOBA_BLOCK), 0)
    qcol = lax.broadcasted_iota(jnp.int32, (MOBA_BLOCK, MOBA_BLOCK), 1)
    frow = lax.broadcasted_iota(jnp.int32, (LANES, MOBA_BLOCK), 0)
    nrow = lax.broadcasted_iota(jnp.int32, (N_BLOCKS, MOBA_BLOCK), 0)
    nrow_f = nrow.astype(F32)
    brow = lax.broadcasted_iota(jnp.int32, (8, MOBA_BLOCK), 0)
    ones_rows = jnp.ones((BF16_SUBLANES, MOBA_BLOCK), BF16)
    zero_rows = jnp.zeros((LANES - N_BLOCKS - 8, MOBA_BLOCK), F32)
    def v_aug(hh, block):
        vt = vt_ref[block, hh * HEAD_DIM:(hh + 1) * HEAD_DIM, :]
        return jnp.concatenate([vt, ones_rows], axis=0)

    def rows_to_tile(x8):
        return jnp.concatenate([x8] * (N_BLOCKS // 8), axis=0)

    n_tiles = 2 * Q_TILES
    n_q_steps = N_BLOCKS // Q_TILES

    def scores(step, slot):
        start = pl.multiple_of(jnp.minimum(step, N_FAR_STEPS - 1) * FAR_KEYS, FAR_KEYS)
        k_aug = jnp.concatenate([k_ref[pl.ds(start, FAR_KEYS), :],
                                 ext_ref[pl.ds(start, FAR_KEYS), :]], axis=1)
        s_ref[slot] = jnp.dot(k_aug, qcat_ref[...],
                              preferred_element_type=F32)

    def prologue(q_step, q_src, near_dst):
        km_hi = kmh_ref[0:N_BLOCKS, :]
        km_lo = kml_ref[0:N_BLOCKS, :]
        q_far = []
        for t in range(Q_TILES):
            own = q_step * Q_TILES + t
            prev = jnp.maximum(own - 1, 0)
            qt2 = q_src[t]
            k_own = k_ref[pl.ds(pl.multiple_of(own * MOBA_BLOCK, MOBA_BLOCK), MOBA_BLOCK), :]
            k_prev = k_ref[pl.ds(pl.multiple_of(prev * MOBA_BLOCK, MOBA_BLOCK), MOBA_BLOCK), :]
            for hh in range(2):
                head = pair * 2 + hh
                in_head = (frow >= hh * HEAD_DIM) & (frow < (hh + 1) * HEAD_DIM)
                qs = (jnp.where(in_head, qt2, jnp.zeros_like(qt2))
                      * jnp.asarray(HEAD_DIM ** -0.5, BF16))

                gate = (jnp.dot(km_hi, qs, preferred_element_type=F32)
                        + jnp.dot(km_lo, qs, preferred_element_type=F32))
                eligible = nrow < own
                g = jnp.where(eligible, gate, -jnp.inf)
                sel = jnp.zeros((N_BLOCKS, MOBA_BLOCK), jnp.bool_)
                for _ in range(MOBA_TOPK):
                    mx = rows_to_tile(_colreduce(g, jnp.maximum))
                    idx = rows_to_tile(_colreduce(jnp.where(g == mx, nrow_f, float(LANES)),
                                                  jnp.minimum))
                    hit = nrow_f == idx
                    sel = sel | hit
                    g = jnp.where(hit, -jnp.inf, g)
                sel = sel & eligible

                b31 = jnp.full((8, MOBA_BLOCK), tab_ref[N_BUCKETS - 1, head], F32)
                b31_hi = b31.astype(BF16).astype(F32)
                bias_rows = jnp.where(brow == 0, b31_hi, jnp.where(brow == 1, b31 - b31_hi, 0.0))

                aug = jnp.concatenate([jnp.where(sel & (nrow != own - 1), 0.0, NEG), bias_rows,
                                       zero_rows], axis=0)
                q_far.append(jnp.concatenate([qs, aug.astype(BF16)], axis=0))

                prev_sel = _colreduce(
                    jnp.where((nrow == own - 1) & jnp.logical_not(sel), NEG, 0.0), jnp.add)
                pmask_ref[2 * t + hh] = jnp.where(own >= 1, prev_sel, NEG)
                near_dst[2 * t + hh, 0:MOBA_BLOCK, :] = (
                    jnp.dot(k_prev, qs, preferred_element_type=F32) + bias_ref[hh, 1])
                s_own = jnp.dot(k_own, qs, preferred_element_type=F32) + bias_ref[hh, 0]
                near_dst[2 * t + hh, MOBA_BLOCK:, :] = jnp.where(qcol >= krow, s_own, NEG)
        qcat_ref[...] = jnp.concatenate(q_far, axis=1)
        scores(0, 0)

    @pl.when(step_i == 0)
    def _():
        prologue(0, qt_ref, near_a_ref)

    def absorb_scores(s, va, m, acc):
        m_new = jnp.maximum(m, jnp.max(s, axis=0, keepdims=True))
        alpha = jnp.exp(m - m_new)
        p = jnp.exp(s - m_new).astype(BF16)
        return m_new, alpha * acc + jnp.dot(va, p, preferred_element_type=F32)

    def absorb(step, slot, carry):
        block0 = step * FAR_BLOCKS
        out = []
        for j in range(n_tiles):
            hh = j % 2
            s = s_ref[slot, :, j * MOBA_BLOCK:(j + 1) * MOBA_BLOCK]
            va = jnp.concatenate([v_aug(hh, block0 + c) for c in range(FAR_BLOCKS)], axis=1)
            out += absorb_scores(s, va, carry[2 * j], carry[2 * j + 1])
        return tuple(out)

    n_steps = step_i
    carry = []
    for _ in range(n_tiles):
        carry += [jnp.full((1, MOBA_BLOCK), NEG, F32),
                  jnp.zeros((HEAD_DIM + BF16_SUBLANES, MOBA_BLOCK), F32)]

    def two_steps(base, carry):
        scores(base + 1, 1)
        carry = absorb(base, 0, carry)
        scores(base + 2, 0)
        return absorb(base + 1, 1, carry)

    def far_body(i, carry):
        return two_steps(4 * i + 2, two_steps(4 * i, carry))

    n_quads = n_steps // 4
    carry = lax.fori_loop(0, n_quads, far_body, tuple(carry))
    carry = lax.cond(n_steps % 4 >= 2,
                     lambda c: two_steps(4 * n_quads, c), lambda c: c, carry)
    carry = lax.cond(n_steps % 2 == 1,
                     lambda c: absorb(n_steps - 1, 0, c), lambda c: c, carry)

    prev_masks = [pmask_ref[j, 0:1, :] for j in range(n_tiles)]

    def epilogue(near_src):
        for t in range(Q_TILES):
            own = step_i * Q_TILES + t
            outs = []
            for hh in range(2):
                j = 2 * t + hh
                va = jnp.concatenate([v_aug(hh, jnp.maximum(own - 1, 0)), v_aug(hh, own)], axis=1)
                s = jnp.concatenate([near_src[j, 0:MOBA_BLOCK, :] + prev_masks[j],
                                     near_src[j, MOBA_BLOCK:, :]], axis=0)
                _, acc = absorb_scores(s, va, carry[2 * j], carry[2 * j + 1])
                outs.append(acc[0:HEAD_DIM] / acc[HEAD_DIM:HEAD_DIM + 1])
            o_ref[t * MOBA_BLOCK:(t + 1) * MOBA_BLOCK, :] = (
                jnp.concatenate(outs, axis=0).T.astype(BF16))

    next_step = jnp.minimum(step_i + 1, n_q_steps - 1)

    def tail(near_src, near_dst):
        def body():
            epilogue(near_src)
            prologue(next_step, qn_ref, near_dst)
        return body

    lax.cond(step_i % 2 == 0, tail(near_a_ref, near_b_ref), tail(near_b_ref, near_a_ref))


def _attention(t5_bias, qt3, proj3, vt3, bias_tiles):
    n_pairs = HEADS // 2
    n_q_steps = N_BLOCKS // Q_TILES
    assert FAR_BLOCKS == Q_TILES
    return pl.pallas_call(
        _attn_kernel,
        grid=(BATCH, n_pairs, N_BLOCKS // Q_TILES),
        in_specs=[
            pl.BlockSpec(memory_space=pltpu.SMEM),
            pl.BlockSpec((None, Q_TILES, LANES, MOBA_BLOCK), lambda b, p, i: (b, i, p, 0)),
            pl.BlockSpec((None, Q_TILES, LANES, MOBA_BLOCK),
                         lambda b, p, i: (b, jnp.minimum(i + 1, n_q_steps - 1), p, 0)),
            pl.BlockSpec((None, SEQ, LANES), lambda b, p, i: (b, 0, p)),
            pl.BlockSpec((None, N_BLOCKS, LANES, MOBA_BLOCK), lambda b, p, i: (b, 0, p, 0)),
            pl.BlockSpec((2, 2, MOBA_BLOCK, MOBA_BLOCK), lambda b, p, i: (p, 0, 0, 0)),
        ],
        out_specs=pl.BlockSpec((None, Q_TILES * MOBA_BLOCK, LANES), lambda b, p, i: (b, i, p)),
        out_shape=jax.ShapeDtypeStruct((BATCH, SEQ, WIDTH), BF16),
        scratch_shapes=[
            pltpu.VMEM((LANES, LANES), BF16),
            pltpu.VMEM((LANES, LANES), BF16),
            pltpu.VMEM((SEQ, LANES), BF16),
            pltpu.VMEM((2, FAR_KEYS, 2 * Q_TILES * MOBA_BLOCK), F32),
            pltpu.VMEM((2 * Q_TILES, 2 * MOBA_BLOCK, MOBA_BLOCK), F32),
            pltpu.VMEM((2 * Q_TILES, 2 * MOBA_BLOCK, MOBA_BLOCK), F32),
            pltpu.VMEM((2 * LANES, 2 * Q_TILES * MOBA_BLOCK), BF16),
            pltpu.VMEM((2 * Q_TILES, 8, MOBA_BLOCK), F32),
        ],
        compiler_params=pltpu.CompilerParams(
            dimension_semantics=("arbitrary", "arbitrary", "arbitrary"),
            vmem_limit_bytes=VMEM_LIMIT),
        name="moba_attention",
    )(t5_bias, qt3, qt3, proj3, vt3, bias_tiles)


def _hgrn_kernel(q_ref, f_ref, i_ref, g_ref, lbl_ref, nw_ref, o_ref,
                 state_ref, bpad_ref, kpad_ref, vpad_ref, shb_ref, shk_ref, shv_ref,
                 ones_ref, mask_ref):
    C = HGRN_CHUNK
    half = WIDTH // 2
    assert HEAD_DIM == 1 << 6

    @pl.when(pl.program_id(1) == 0)
    def _():
        state_ref[...] = jnp.zeros_like(state_ref)
        sr = lax.broadcasted_iota(jnp.int32, (half, half), 0) >> 6
        sc = lax.broadcasted_iota(jnp.int32, (half, half), 1) >> 6
        same = jnp.where(sr == sc, 1.0, 0.0)
        ones_ref[...] = same.astype(BF16)
        mask_ref[...] = same

    zpad = jnp.zeros((8, WIDTH), F32)
    bpad_ref[0:8, :] = zpad
    kpad_ref[0:8, :] = zpad
    vpad_ref[0:8, :] = zpad

    l0 = lbl_ref[0:1, :]
    l1 = lbl_ref[1:2, :]
    lmx = jnp.maximum(l0, l1)
    e0 = jnp.exp(l0 - lmx)
    lb = e0 / (e0 + jnp.exp(l1 - lmx))
    nw = nw_ref[...]

    ri = lax.broadcasted_iota(jnp.int32, (C, C), 0)
    ci = lax.broadcasted_iota(jnp.int32, (C, C), 1)
    tril = jnp.where(ri >= ci, 1.0, 0.0).astype(BF16)

    def head_sum(t):
        return jnp.concatenate(
            [jnp.dot(t[:, 0:half], ones_ref[...], preferred_element_type=F32),
             jnp.dot(t[:, half:], ones_ref[...], preferred_element_type=F32)], axis=1)

    def prepare(r0):
        qq = q_ref[r0:r0 + C, :].astype(F32)
        qq = qq * _sigmoid(qq)
        f = lb + (1.0 - lb) * _sigmoid(f_ref[r0:r0 + C, :])
        gl = jnp.log(f)
        kk = 1.0 - f
        vv = i_ref[r0:r0 + C, :].astype(F32)
        g_hi = gl.astype(BF16)
        g_r = gl - g_hi.astype(F32)
        g_mid = g_r.astype(BF16)
        g_lo = (g_r - g_mid.astype(F32)).astype(BF16)
        b = (jnp.dot(tril, g_hi, preferred_element_type=F32)
             + jnp.dot(tril, g_mid, preferred_element_type=F32)
             + jnp.dot(tril, g_lo, preferred_element_type=F32))
        return qq, kk, vv, b * LOG2E

    def intra_pairs(qq, kk, vv, b):
        bpad_ref[8:8 + C, :] = b
        kpad_ref[8:8 + C, :] = kk
        vpad_ref[8:8 + C, :] = vv
        for dd in range(8):
            shb_ref[dd] = bpad_ref[8 - dd:8 - dd + C, :]
            shk_ref[dd] = kpad_ref[8 - dd:8 - dd + C, :]
            shv_ref[dd] = vpad_ref[8 - dd:8 - dd + C, :]

        o_parts = []
        for j in range(C // 8):
            rows = C - 8 * j
            bt = b[8 * j:, :]
            qt = qq[8 * j:, :]
            terms = []
            vs_all = []
            for dd in range(8):
                terms.append(qt * shk_ref[dd, 0:rows, :] * jnp.exp2(bt - shb_ref[dd, 0:rows, :]))
                vs_all.append(shv_ref[dd, 0:rows, :])
            a = head_sum(jnp.concatenate(terms, axis=0).astype(BF16))
            av = a * jnp.concatenate(vs_all, axis=0)
            part = av[0:rows]
            for dd in range(1, 8):
                part = part + av[dd * rows:(dd + 1) * rows]
            o_parts.append(part)
        o = o_parts[0]
        for j in range(1, C // 8):
            o = o + jnp.concatenate([jnp.zeros((8 * j, WIDTH), F32), o_parts[j]], axis=0)
        return o

    pt_i = lax.broadcasted_iota(jnp.int32, (C, half), 0)
    ps_i = lax.broadcasted_iota(jnp.int32, (C, half), 1) & (C - 1)
    causal = jnp.where(ps_i <= pt_i, 1.0, 0.0)

    def intra_matmul(qq, kk, vv, b):
        qd = (qq * jnp.exp2(b)).astype(BF16)
        kd = (kk * jnp.exp2(-b)).astype(BF16)
        vb = vv.astype(BF16)
        outs = []
        for a_i in range(2):
            sl = slice(a_i * half, (a_i + 1) * half)
            k4 = jnp.concatenate([kd[:, sl]] * (half // C), axis=0) * ones_ref[...]
            v4 = jnp.concatenate([vb[:, sl]] * (half // C), axis=0) * ones_ref[...]
            pair = _nt_dot(qd[:, sl], k4) * causal
            outs.append(jnp.dot(pair.astype(BF16), v4, preferred_element_type=F32))
        return jnp.concatenate(outs, axis=1)

    def finish(r0, o, qq, kk, vv, b):
        qe = (qq * jnp.exp2(b)).astype(BF16)
        b_last = b[C - 1:C, :]
        kd = kk * jnp.exp2(b_last - b)
        dec = jnp.exp2(b_last)
        inter = []
        for a_i in range(2):
            sl = slice(a_i * half, (a_i + 1) * half)
            st = state_ref[a_i]
            inter.append(_nt_dot(qe[:, sl], st.astype(BF16)))
            upd = _tn_dot(vv[:, sl].astype(BF16), kd[:, sl].astype(BF16))
            state_ref[a_i] = st * dec[:, sl] + upd * mask_ref[...]
        o = o + jnp.concatenate(inter, axis=1)

        ms = head_sum((o * o).astype(BF16)) * (1.0 / HEAD_DIM)
        r = o * lax.rsqrt(ms + RMS_EPS) * nw
        r = r * _sigmoid(g_ref[r0:r0 + C, :].astype(F32))
        o_ref[r0:r0 + C, :] = r.astype(BF16)

    chunks = [prepare(u * C) for u in range(HGRN_ROWS // C)]
    lowest = chunks[0][3][C - 1:C, :]
    for ch in chunks[1:]:
        lowest = jnp.minimum(lowest, ch[3][C - 1:C, :])
    safe = jnp.min(lowest) > -HGRN_SAFE_LOG2

    def run(intra):
        def body():
            for u, (qq, kk, vv, b) in enumerate(chunks):
                finish(u * C, intra(qq, kk, vv, b), qq, kk, vv, b)
        return body

    lax.cond(safe, run(intra_matmul), run(intra_pairs))


def _hgrn(proj3, f3, lb_logits, norm_w):
    blk = lambda col: pl.BlockSpec((None, HGRN_ROWS, WIDTH), lambda b, i, col=col: (b, i, col))
    return pl.pallas_call(
        _hgrn_kernel,
        grid=(BATCH, SEQ // HGRN_ROWS),
        in_specs=[
            blk(COL_HQ),
            pl.BlockSpec((None, HGRN_ROWS, WIDTH), lambda b, i: (b, i, 0)),
            blk(COL_HI),
            blk(COL_HG),
            pl.BlockSpec((2, WIDTH), lambda b, i: (0, 0)),
            pl.BlockSpec((1, WIDTH), lambda b, i: (0, 0)),
        ],
        out_specs=pl.BlockSpec((None, HGRN_ROWS, WIDTH), lambda b, i: (b, i, 0)),
        out_shape=jax.ShapeDtypeStruct((BATCH, SEQ, WIDTH), BF16),
        scratch_shapes=[
            pltpu.VMEM((2, WIDTH // 2, WIDTH // 2), F32),
            pltpu.VMEM((8 + HGRN_CHUNK, WIDTH), F32),
            pltpu.VMEM((8 + HGRN_CHUNK, WIDTH), F32),
            pltpu.VMEM((8 + HGRN_CHUNK, WIDTH), F32),
            pltpu.VMEM((8, HGRN_CHUNK, WIDTH), F32),
            pltpu.VMEM((8, HGRN_CHUNK, WIDTH), F32),
            pltpu.VMEM((8, HGRN_CHUNK, WIDTH), F32),
            pltpu.VMEM((WIDTH // 2, WIDTH // 2), BF16),
            pltpu.VMEM((WIDTH // 2, WIDTH // 2), F32),
        ],
        compiler_params=pltpu.CompilerParams(
            dimension_semantics=("arbitrary", "arbitrary"), vmem_limit_bytes=VMEM_LIMIT),
        name="hgrn2",
    )(proj3, f3, proj3, proj3, lb_logits, norm_w)


def _out_kernel(a_ref, r_ref, x_ref, wo_ref, g_ref, b_ref, wr_ref, br_ref,
                x1_ref, comb_ref):
    mix = jnp.dot(a_ref[...], wo_ref[0:WIDTH, :], preferred_element_type=F32)
    mix = mix + jnp.dot(r_ref[...], wo_ref[WIDTH:, :], preferred_element_type=F32)
    x1 = _layer_norm(ALPHA * x_ref[...] + mix, g_ref[...], b_ref[...])
    x1_ref[...] = x1

    x_hi = x1.astype(BF16)
    x_lo = (x1 - x_hi.astype(F32)).astype(BF16)
    wr = wr_ref[...]
    w_hi = wr.astype(BF16)
    w_lo = (wr - w_hi.astype(F32)).astype(BF16)
    logits = (jnp.dot(x_hi, w_hi, preferred_element_type=F32)
              + jnp.dot(x_lo, w_hi, preferred_element_type=F32)
              + jnp.dot(x_hi, w_lo, preferred_element_type=F32)) + br_ref[...]
    assert EXPERTS_PER_GROUP == 1 << 2
    lane_i = lax.broadcasted_iota(jnp.int32, logits.shape, 1)
    lane = lane_i.astype(F32)
    grp_of_lane = (lane_i >> 2).astype(F32)
    none = float(LANES)
    is_g = (lane_i >= N_EXPERTS) & (lane_i < N_EXPERTS + N_GROUPS)
    gl = jnp.where(is_g, logits, -jnp.inf)
    ge = jnp.exp(gl - jnp.max(gl, axis=-1, keepdims=True))
    gp = ge / jnp.sum(ge, axis=-1, keepdims=True)
    g_w = jnp.max(gp, axis=-1, keepdims=True)
    g_lane = jnp.min(jnp.where(is_g & (gp == g_w), lane, none), axis=-1, keepdims=True)
    g_idx = g_lane - float(N_EXPERTS)

    in_grp = (lane_i < N_EXPERTS) & (grp_of_lane == g_idx)
    el = jnp.where(in_grp, logits, -jnp.inf)
    ee = jnp.exp(el - jnp.max(el, axis=-1, keepdims=True))
    ep = ee / jnp.sum(ee, axis=-1, keepdims=True)
    p1 = jnp.max(ep, axis=-1, keepdims=True)
    i1 = jnp.min(jnp.where(in_grp & (ep == p1), lane, none), axis=-1, keepdims=True)
    rest = in_grp & (lane != i1)
    ep2 = jnp.where(rest, ep, -1.0)
    p2 = jnp.max(ep2, axis=-1, keepdims=True)
    i2 = jnp.min(jnp.where(rest & (ep2 == p2), lane, none), axis=-1, keepdims=True)
    den = p1 + p2
    comb = jnp.where(lane == i1, g_w * (p1 / den), 0.0)
    comb = jnp.where(lane == i2, g_w * (p2 / den), comb)
    comb_ref[...] = jnp.where(lane_i == GROUP_LANE, g_idx, comb)


def _out_proj(a2, r2, x2, wo_b, g1, b1, w_route, b_route):
    row = lambda w: pl.BlockSpec((OUT_TM, w), lambda i: (i, 0))
    full = lambda s: pl.BlockSpec(s, lambda i: (0, 0))
    return pl.pallas_call(
        _out_kernel,
        grid=(N_TOK // OUT_TM,),
        in_specs=[row(WIDTH), row(WIDTH), row(D_MODEL), full((D_MODEL, D_MODEL)),
                  full((1, D_MODEL)), full((1, D_MODEL)), full((D_MODEL, LANES)), full((1, LANES))],
        out_specs=[row(D_MODEL), row(LANES)],
        out_shape=[jax.ShapeDtypeStruct((N_TOK, D_MODEL), F32),
                   jax.ShapeDtypeStruct((N_TOK, LANES), F32)],
        compiler_params=pltpu.CompilerParams(
            dimension_semantics=("arbitrary",), vmem_limit_bytes=VMEM_LIMIT),
        name="out_proj_ln_route",
    )(a2, r2, x2, wo_b, g1, b1, w_route, b_route)


def _moe_kernel(x1_ref, comb_ref, wg_ref, wu_ref, wd_ref, g_ref, b_ref, o_ref,
                pt_ref, xs_ref, cs_ref, acc_ref, win_ref):
    step = pl.program_id(1)
    assert MOE_ALIGN == 1 << 7 and EXPERTS_PER_GROUP == 1 << 2
    assert EXPERTS_PER_GROUP % MOE_EXPERTS_PER_STEP == 0

    @pl.when(step == 0)
    def _():
        comb = comb_ref[...]
        lane = lax.broadcasted_iota(jnp.int32, comb.shape, 1)
        lane_f = lane.astype(F32)
        gid = jnp.sum(jnp.where(lane == GROUP_LANE, comb, 0.0), axis=-1, keepdims=True)
        mine = lane_f == gid
        ti = lax.broadcasted_iota(jnp.int32, (MOE_ALIGN, MOE_ALIGN), 0)
        tj = lax.broadcasted_iota(jnp.int32, (MOE_ALIGN, MOE_ALIGN), 1)
        tril = jnp.where(ti >= tj, 1.0, 0.0).astype(BF16)
        member = jnp.where(mine, 1.0, 0.0).astype(BF16)
        before = jnp.zeros((1, LANES), F32)
        pieces = []
        for blk in range(MOE_TM // MOE_ALIGN):
            inc = jnp.dot(tril, member[blk * MOE_ALIGN:(blk + 1) * MOE_ALIGN, :],
                          preferred_element_type=F32)
            pieces.append(inc + before)
            before = before + inc[MOE_ALIGN - 1:MOE_ALIGN, :]
        incl = jnp.concatenate(pieces, axis=0)
        rank = jnp.sum(jnp.where(mine, incl, 0.0), axis=-1, keepdims=True) - 1.0
        counts = incl[MOE_TM - 1:MOE_TM, :]
        lane_row = lane[0:1, :]
        start = jnp.int32(0)
        base = jnp.zeros_like(gid)
        for g in range(N_GROUPS):
            cnt = jnp.sum(jnp.where(lane_row == g, counts, 0.0)).astype(jnp.int32)
            chunks = (cnt + (MOE_ALIGN - 1)) >> 7
            win_ref[0, g] = start
            tail = jnp.where((chunks & 1) == 1, jnp.minimum(chunks, 3), 0)
            win_ref[1, g] = (chunks - tail) >> 1
            win_ref[2, g] = tail
            base = jnp.where(gid == float(g), start.astype(F32), base)
            start = start + chunks * MOE_ALIGN
        pos = base + rank
        r_iota = lax.broadcasted_iota(jnp.int32, (MOE_TM, MOE_ROWS), 1).astype(F32)
        pt = jnp.where(r_iota == pos, 1.0, 0.0).astype(BF16)
        pt_ref[...] = pt
        c_hi = comb.astype(BF16)
        c_lo = (comb - c_hi.astype(F32)).astype(BF16)
        srt = _tn_dot(pt, jnp.concatenate([x1_ref[...].astype(BF16), c_hi, c_lo], axis=1))
        xs_ref[...] = srt[:, 0:D_MODEL].astype(BF16)
        cs_ref[...] = srt[:, D_MODEL:D_MODEL + LANES] + srt[:, D_MODEL + LANES:]
        acc_ref[...] = jnp.zeros_like(acc_ref)

    e0 = step * MOE_EXPERTS_PER_STEP
    g = e0 >> 2
    row0 = win_ref[0, g]
    n_full = win_ref[1, g]

    def window(start, rows):
        r0 = pl.multiple_of(start, MOE_ALIGN)
        xw = xs_ref[pl.ds(r0, rows), :]
        csw = cs_ref[pl.ds(r0, rows), :]
        lane_w = lax.broadcasted_iota(jnp.int32, (rows, LANES), 1)
        y = None
        for j in range(MOE_EXPERTS_PER_STEP):
            hg = jnp.dot(xw, wg_ref[j], preferred_element_type=F32)
            hu = jnp.dot(xw, wu_ref[j], preferred_element_type=F32)
            h = (hg * _sigmoid(hg) * hu).astype(BF16)
            c = jnp.sum(jnp.where(lane_w == e0 + j, csw, 0.0), axis=-1, keepdims=True)
            yj = c * jnp.dot(h, wd_ref[j], preferred_element_type=F32)
            y = yj if y is None else y + yj
        acc_ref[pl.ds(r0, rows), :] += y

    def full_window(k, _):
        window(row0 + k * MOE_WIN, MOE_WIN)
        return 0

    lax.fori_loop(0, n_full, full_window, 0)

    @pl.when(win_ref[2, g] == 1)
    def _():
        window(row0 + n_full * MOE_WIN, MOE_ALIGN)

    @pl.when(win_ref[2, g] == 3)
    def _():
        window(row0 + n_full * MOE_WIN, 3 * MOE_ALIGN)

    @pl.when(step == N_EXPERTS // MOE_EXPERTS_PER_STEP - 1)
    def _():
        moe = jnp.dot(pt_ref[...], acc_ref[...].astype(BF16), preferred_element_type=F32)
        o_ref[...] = _layer_norm(ALPHA * x1_ref[...] + moe, g_ref[...], b_ref[...])


def _moe(x1, comb, wg_b, wu_b, wd_b, g2, b2):
    return pl.pallas_call(
        _moe_kernel,
        grid=(N_TOK // MOE_TM, N_EXPERTS // MOE_EXPERTS_PER_STEP),
        in_specs=[
            pl.BlockSpec((MOE_TM, D_MODEL), lambda i, e: (i, 0)),
            pl.BlockSpec((MOE_TM, LANES), lambda i, e: (i, 0)),
            pl.BlockSpec((MOE_EXPERTS_PER_STEP, D_MODEL, D_EXPERT), lambda i, e: (e, 0, 0)),
            pl.BlockSpec((MOE_EXPERTS_PER_STEP, D_MODEL, D_EXPERT), lambda i, e: (e, 0, 0)),
            pl.BlockSpec((MOE_EXPERTS_PER_STEP, D_EXPERT, D_MODEL), lambda i, e: (e, 0, 0)),
            pl.BlockSpec((1, D_MODEL), lambda i, e: (0, 0)),
            pl.BlockSpec((1, D_MODEL), lambda i, e: (0, 0)),
        ],
        out_specs=pl.BlockSpec((MOE_TM, D_MODEL), lambda i, e: (i, 0)),
        out_shape=jax.ShapeDtypeStruct((N_TOK, D_MODEL), F32),
        scratch_shapes=[
            pltpu.VMEM((MOE_TM, MOE_ROWS), BF16),
            pltpu.VMEM((MOE_ROWS, D_MODEL), BF16),
            pltpu.VMEM((MOE_ROWS, LANES), F32),
            pltpu.VMEM((MOE_ROWS, D_MODEL), F32),
            pltpu.SMEM((3, N_GROUPS), jnp.int32),
        ],
        compiler_params=pltpu.CompilerParams(
            dimension_semantics=("arbitrary", "arbitrary"), vmem_limit_bytes=MOE_VMEM_LIMIT),
        name="moe_experts_ln",
    )(x1, comb, wg_b, wu_b, wd_b, g2, b2)


def kernel(x, w_in, t5_bias, hgrn_lb_logits, hgrn_norm_w, w_o, ln1_g, ln1_b, w_group, b_group,
           w_expert, b_expert, w_gate, w_up, w_down, ln2_g, ln2_b):
    x2 = x.reshape(N_TOK, D_MODEL)
    w_in_b = w_in[0].astype(BF16)
    w_qt_b = w_in_b[:, 0:WIDTH].T
    w_vt_b = w_in_b[:, 2 * WIDTH:3 * WIDTH].T
    proj, f_logits, qt3, vt3 = _proj(x2, w_in_b, w_qt_b, w_vt_b)
    proj3 = proj.reshape(BATCH, SEQ, ROW_WIDTH)

    a = _attention(t5_bias, qt3, proj3, vt3, _bias_tiles(t5_bias))
    r = _hgrn(proj3, f_logits.reshape(BATCH, SEQ, WIDTH), hgrn_lb_logits, hgrn_norm_w)

    pad = LANES - N_EXPERTS - N_GROUPS
    w_route = jnp.concatenate(
        [w_expert[0].transpose(1, 0, 2).reshape(D_MODEL, N_EXPERTS), w_group[0],
         jnp.zeros((D_MODEL, pad), F32)], axis=1)
    b_route = jnp.concatenate(
        [b_expert[0].reshape(N_EXPERTS), b_group[0], jnp.zeros((pad,), F32)]).reshape(1, LANES)
    x1, comb = _out_proj(
        a.reshape(N_TOK, WIDTH), r.reshape(N_TOK, WIDTH), x2, w_o[0].astype(BF16),
        ln1_g, ln1_b, w_route, b_route)

    out = _moe(x1, comb, w_gate[0].astype(BF16), w_up[0].astype(BF16),
               w_down[0].astype(BF16), ln2_g, ln2_b)
    return out.reshape(BATCH, SEQ, D_MODEL)
```
